```python
import jax
import jax.numpy as jnp
from jax import lax
import numpy as np

D_MODEL = 2048
BATCH = 2
SEQ = 4096
DEPTH = 1
DEC_BATCH = 32
DEC_SEQ = 64
PAST_LEN = 4096

CHUNK = 64
HEAD_DIM = 128
N_HEADS_SB = 8
N_HEADS_BAND = 8
W_SB = N_HEADS_SB * HEAD_DIM
W_BAND = N_HEADS_BAND * HEAD_DIM
BAND_LEFT_CHUNKS = 8
BAND_PAST = BAND_LEFT_CHUNKS * CHUNK
REL_CLIP = 128
Q_BLOCK = 128
N_GROUPS = 4
EXPERTS_PER_GROUP = 8
N_EXPERTS = N_GROUPS * EXPERTS_PER_GROUP
TOP_K_INNER = 2
D_EXPERT = 512
EPS = 1e-6
SPLITS = [W_SB, 2 * W_SB, 3 * W_SB, 3 * W_SB + W_BAND, 3 * W_SB + 2 * W_BAND, 3 * W_SB + 3 * W_BAND, 3 * W_SB + 3 * W_BAND + D_MODEL]
D_IN = 3 * W_SB + 3 * W_BAND + 2 * D_MODEL

kernel_name = 'streaming_stickbreak_band_hmoe_step'


def rms_norm(x, g):
    xf = x.astype(jnp.float32)
    y = xf * lax.rsqrt(jnp.mean(xf * xf, axis=-1, keepdims=True) + EPS)
    return (y * g.astype(jnp.float32)).astype(x.dtype)


def ada_modulation(c, w_ada, b_ada):
    m = jax.nn.silu(c) @ w_ada + b_ada
    return jnp.split(m[:, None, :], 6, axis=-1)


def stick_breaking_block(q, k, v, q_pos, k_pos):
    z = jnp.einsum('bqhd,bkhd->bhqk', q, k).astype(jnp.float32) * (HEAD_DIM ** -0.5)
    valid = k_pos[None, :] < q_pos[:, None]
    log_keep = jnp.where(valid, jax.nn.log_sigmoid(-z), 0.0)
    tail = lax.cumsum(log_keep, axis=3, reverse=True) - log_keep
    a = jnp.where(valid, jnp.exp(jax.nn.log_sigmoid(z) + tail), 0.0)
    return jnp.einsum('bhqk,bkhd->bqhd', a, v.astype(jnp.float32))


def stick_breaking_prompt(q, k, v):
    b, s, h, d = q.shape
    nb = s // Q_BLOCK
    pos = jnp.arange(s)
    q_blocks = jnp.moveaxis(q.reshape(b, nb, Q_BLOCK, h, d), 1, 0)
    out = lax.map(lambda qp: stick_breaking_block(qp[0], k, v, qp[1], pos), (q_blocks, pos.reshape(nb, Q_BLOCK)))
    return jnp.moveaxis(out, 0, 1).reshape(b, s, h * d)


def stick_breaking_step(q, k_new, v_new, cache_k, cache_v):
    b, t, h, d = q.shape
    p = cache_k.shape[1]
    k = jnp.concatenate([cache_k, k_new.astype(cache_k.dtype)], axis=1)
    v = jnp.concatenate([cache_v, v_new.astype(cache_v.dtype)], axis=1)
    out = stick_breaking_block(q, k, v, p + jnp.arange(t), jnp.arange(p + t))
    return out.reshape(b, t, h * d)


def rel_bias(table, rel):
    return table[:, jnp.clip(rel, -REL_CLIP, REL_CLIP) + REL_CLIP].astype(jnp.float32)


def biased_attention(q, k, v, bias, valid):
    s = jnp.einsum('...qhd,...khd->...hqk', q, k).astype(jnp.float32) * (HEAD_DIM ** -0.5) + bias
    if valid is not None:
        s = jnp.where(valid, s, -jnp.inf)
    p = jax.nn.softmax(s, axis=-1)
    return jnp.einsum('...hqk,...khd->...qhd', p, v.astype(jnp.float32))


def band_prompt(q, k, v, table):
    b, s, h, d = q.shape
    nc = s // CHUNK
    span = (BAND_LEFT_CHUNKS + 1) * CHUNK

    def gather_band(x):
        xc = jnp.pad(x.reshape(b, nc, CHUNK, h, d), ((0, 0), (BAND_LEFT_CHUNKS, 0), (0, 0), (0, 0), (0, 0)))
        return jnp.concatenate([xc[:, i:i + nc] for i in range(BAND_LEFT_CHUNKS + 1)], axis=2)

    j = jnp.arange(span)
    rel = jnp.arange(CHUNK)[:, None] - j[None, :] + BAND_PAST
    k_pos = (jnp.arange(nc)[:, None] - BAND_LEFT_CHUNKS) * CHUNK + j[None, :]
    valid = (k_pos >= 0)[None, :, None, None, :]
    out = biased_attention(q.reshape(b, nc, CHUNK, h, d), gather_band(k), gather_band(v), rel_bias(table, rel), valid)
    return out.reshape(b, s, h * d)


def band_step(q, k_new, v_new, cache_k, cache_v, table):
    b, t, h, d = q.shape
    lb = cache_k.shape[1]
    k = jnp.concatenate([cache_k, k_new.astype(cache_k.dtype)], axis=1)
    v = jnp.concatenate([cache_v, v_new.astype(cache_v.dtype)], axis=1)
    rel = jnp.arange(t)[:, None] - jnp.arange(lb + t)[None, :] + lb
    out = biased_attention(q, k, v, rel_bias(table, rel), None)
    return out.reshape(b, t, h * d)


def hier_moe(h, w_rg, b_rg, w_re, b_re, w_gate, w_up, w_down):
    lead = h.shape[:-1]
    x = h.reshape(-1, D_MODEL)
    g_logits = (x @ w_rg).astype(jnp.float32) + b_rg.astype(jnp.float32)
    g_onehot = jax.nn.one_hot(jnp.argmax(g_logits, axis=-1), N_GROUPS, dtype=jnp.float32)
    g_weight = jnp.sum(jax.nn.softmax(g_logits, axis=-1) * g_onehot, axis=-1, keepdims=True)
    e_logits = (x @ w_re).astype(jnp.float32).reshape(-1, N_GROUPS, EXPERTS_PER_GROUP) + b_re.astype(jnp.float32)
    e_sel = jnp.einsum('ng,nge->ne', g_onehot, e_logits)
    top_v, top_i = lax.top_k(e_sel, TOP_K_INNER)
    inner = jnp.einsum('nk,nke->ne', jax.nn.softmax(top_v, axis=-1), jax.nn.one_hot(top_i, EXPERTS_PER_GROUP, dtype=jnp.float32))
    combine = (g_onehot[:, :, None] * (g_weight * inner)[:, None, :]).reshape(-1, N_EXPERTS)
    hidden = jax.nn.silu(jnp.einsum('nd,edf->nef', x, w_gate)) * jnp.einsum('nd,edf->nef', x, w_up)
    y = jnp.einsum('nef,efd->nd', hidden * combine[:, :, None].astype(hidden.dtype), w_down)
    return y.reshape(*lead, D_MODEL)


def trunk_layer(x, c, attend, norm_mix, norm_ffn, w_ada, b_ada, w_in, q_norm_band, k_norm_band, w_proj_sb, w_proj_band, w_out, w_rg, b_rg, w_re, b_re, w_gate, w_up, w_down):
    shift_m, scale_m, gate_m, shift_f, scale_f, gate_f = ada_modulation(c, w_ada, b_ada)
    h = rms_norm(x, norm_mix) * (1 + scale_m) + shift_m
    lead = h.shape[:-1]
    qa, ka, va, qb, kb, vb, ga, gb = jnp.split(h @ w_in, SPLITS, axis=-1)
    qa = qa.reshape(*lead, N_HEADS_SB, HEAD_DIM)
    ka = ka.reshape(*lead, N_HEADS_SB, HEAD_DIM)
    va = va.reshape(*lead, N_HEADS_SB, HEAD_DIM)
    qb = rms_norm(qb.reshape(*lead, N_HEADS_BAND, HEAD_DIM), q_norm_band)
    kb = rms_norm(kb.reshape(*lead, N_HEADS_BAND, HEAD_DIM), k_norm_band)
    vb = vb.reshape(*lead, N_HEADS_BAND, HEAD_DIM)
    o_sb, o_band = attend(qa, ka, va, qb, kb, vb)
    merged = jax.nn.sigmoid(ga) * (o_sb.astype(x.dtype) @ w_proj_sb) + jax.nn.sigmoid(gb) * (o_band.astype(x.dtype) @ w_proj_band)
    x = x + gate_m * (merged @ w_out)
    h2 = rms_norm(x, norm_ffn) * (1 + scale_f) + shift_f
    x = x + gate_f * hier_moe(h2, w_rg, b_rg, w_re, b_re, w_gate, w_up, w_down)
    return x, ka, va, kb, vb


def setup_inputs(seed: int = 0) -> dict:
    key = jax.random.key(seed)
    ks = jax.random.split(key, 26)

    def nrm(k, shape, scale):
        return jax.random.normal(k, shape, jnp.float32) * scale

    lb = min(BAND_PAST, PAST_LEN)
    L = DEPTH
    return {
        'x_prompt': nrm(ks[0], (BATCH, SEQ, D_MODEL), 1.0),
        'x_sample': nrm(ks[1], (DEC_BATCH, DEC_SEQ, D_MODEL), 1.0),
        'cache_sb_k': nrm(ks[2], (L, DEC_BATCH, PAST_LEN, N_HEADS_SB, HEAD_DIM), 1.0),
        'cache_sb_v': nrm(ks[3], (L, DEC_BATCH, PAST_LEN, N_HEADS_SB, HEAD_DIM), 1.0),
        'cache_band_k': nrm(ks[4], (L, DEC_BATCH, lb, N_HEADS_BAND, HEAD_DIM), 1.0),
        'cache_band_v': nrm(ks[5], (L, DEC_BATCH, lb, N_HEADS_BAND, HEAD_DIM), 1.0),
        'c_prompt': nrm(ks[6], (BATCH, D_MODEL), 1.0),
        'c_sample': nrm(ks[7], (DEC_BATCH, D_MODEL), 1.0),
        'norm_mix': 1.0 + nrm(ks[8], (L, D_MODEL), 0.1),
        'norm_ffn': 1.0 + nrm(ks[9], (L, D_MODEL), 0.1),
        'w_ada': nrm(ks[10], (L, D_MODEL, 6 * D_MODEL), D_MODEL ** -0.5),
        'b_ada': nrm(ks[11], (L, 6 * D_MODEL), 0.02),
        'w_in': nrm(ks[12], (L, D_MODEL, D_IN), D_MODEL ** -0.5),
        'q_norm_band': 1.0 + nrm(ks[13], (L, N_HEADS_BAND, HEAD_DIM), 0.1),
        'k_norm_band': 1.0 + nrm(ks[14], (L, N_HEADS_BAND, HEAD_DIM), 0.1),
        'rel_bias_band': nrm(ks[15], (L, N_HEADS_BAND, 2 * REL_CLIP + 1), 0.5),
        'w_proj_sb': nrm(ks[16], (L, W_SB, D_MODEL), W_SB ** -0.5),
        'w_proj_band': nrm(ks[17], (L, W_BAND, D_MODEL), W_BAND ** -0.5),
        'w_out': nrm(ks[18], (L, D_MODEL, D_MODEL), D_MODEL ** -0.5),
        'w_router_group': nrm(ks[19], (L, D_MODEL, N_GROUPS), D_MODEL ** -0.5),
        'b_router_group': nrm(ks[20], (L, N_GROUPS), 0.01),
        'w_router_expert': nrm(ks[21], (L, D_MODEL, N_EXPERTS), D_MODEL ** -0.5),
        'b_router_expert': nrm(ks[22], (L, N_GROUPS, EXPERTS_PER_GROUP), 0.01),
        'w_gate': nrm(ks[23], (L, N_EXPERTS, D_MODEL, D_EXPERT), D_MODEL ** -0.5),
        'w_up': nrm(ks[24], (L, N_EXPERTS, D_MODEL, D_EXPERT), D_MODEL ** -0.5),
        'w_down': nrm(ks[25], (L, N_EXPERTS, D_EXPERT, D_MODEL), D_EXPERT ** -0.5),
    }


def reference(x_prompt, x_sample, cache_sb_k, cache_sb_v, cache_band_k, cache_band_v, c_prompt, c_sample, norm_mix, norm_ffn, w_ada, b_ada, w_in, q_norm_band, k_norm_band, rel_bias_band, w_proj_sb, w_proj_band, w_out, w_router_group, b_router_group, w_router_expert, b_router_expert, w_gate, w_up, w_down):
    n_band_prompt = min(BAND_PAST, x_prompt.shape[1])
    xp, xs = x_prompt, x_sample
    p_sb_k, p_sb_v, p_band_k, p_band_v = [], [], [], []
    s_sb_k, s_sb_v, s_band_k, s_band_v = [], [], [], []
    for l in range(DEPTH):
        table = rel_bias_band[l]
        layer_w = (norm_mix[l], norm_ffn[l], w_ada[l], b_ada[l], w_in[l], q_norm_band[l], k_norm_band[l],
                   w_proj_sb[l], w_proj_band[l], w_out[l], w_router_group[l], b_router_group[l],
                   w_router_expert[l], b_router_expert[l], w_gate[l], w_up[l], w_down[l])

        def prompt_attend(qa, ka, va, qb, kb, vb, table=table):
            return stick_breaking_prompt(qa, ka, va), band_prompt(qb, kb, vb, table)

        def sample_attend(qa, ka, va, qb, kb, vb, table=table, l=l):
            return (stick_breaking_step(qa, ka, va, cache_sb_k[l], cache_sb_v[l]),
                    band_step(qb, kb, vb, cache_band_k[l], cache_band_v[l], table))

        xp, ka, va, kb, vb = trunk_layer(xp, c_prompt, prompt_attend, *layer_w)
        p_sb_k.append(ka)
        p_sb_v.append(va)
        p_band_k.append(kb[:, -n_band_prompt:])
        p_band_v.append(vb[:, -n_band_prompt:])
        xs, ka, va, kb, vb = trunk_layer(xs, c_sample, sample_attend, *layer_w)
        s_sb_k.append(ka)
        s_sb_v.append(va)
        s_band_k.append(kb)
        s_band_v.append(vb)
    return (xp, xs, jnp.stack(p_sb_k, axis=0), jnp.stack(p_sb_v, axis=0), jnp.stack(p_band_k, axis=0), jnp.stack(p_band_v, axis=0), jnp.stack(s_sb_k, axis=0), jnp.stack(s_sb_v, axis=0), jnp.stack(s_band_k, axis=0), jnp.stack(s_band_v, axis=0))
```

```python
import functools

import jax
import jax.numpy as jnp
from jax import lax
from jax.experimental import pallas as pl
from jax.experimental.pallas import tpu as pltpu

F32 = jnp.float32
BF16 = jnp.bfloat16

EPS = 1e-6
HEAD_DIM = 128
N_HEADS = 8
W_HEADS = N_HEADS * HEAD_DIM
CHUNK = 64
BAND_LEFT_CHUNKS = 8
BAND_PAST = BAND_LEFT_CHUNKS * CHUNK
REL_CLIP = 128
N_GROUPS = 4
EXPERTS_PER_GROUP = 8
N_EXPERTS = N_GROUPS * EXPERTS_PER_GROUP
QK_SCALE = HEAD_DIM ** -0.5
NEG_BIG = -1e30

V7X_LANES = 128
V7X_VMEM_LIMIT = 56 * 1024 * 1024
ROW_GROUP = 64
MOE_TILE = 256


def _params(*sem):
    return pltpu.CompilerParams(dimension_semantics=sem, vmem_limit_bytes=V7X_VMEM_LIMIT)


def _sigmoid(x):
    return 1.0 / (1.0 + jnp.exp(-x))


def _dot(a, b):
    return jnp.dot(a, b, preferred_element_type=F32)


def _dot_nt(a, b):
    return lax.dot_general(a, b, (((1,), (1,)), ((), ())), preferred_element_type=F32)


def _ada_kernel(c_ref, w_ref, b_ref, o_ref):
    c = c_ref[...]
    a = (c * _sigmoid(c)).astype(BF16)
    o_ref[0] = _dot(a, w_ref[...].astype(BF16)) + b_ref[...]


def _ada(c_pad, w_ada, b_ada):
    r, d = c_pad.shape
    tn = min(1024, d)
    per = d // tn
    return pl.pallas_call(
        _ada_kernel,
        grid=(6 * per,),
        in_specs=[
            pl.BlockSpec((r, d), lambda j: (0, 0)),
            pl.BlockSpec((d, tn), lambda j: (0, j)),
            pl.BlockSpec((1, tn), lambda j: (0, j)),
        ],
        out_specs=pl.BlockSpec((1, r, tn), lambda j: (j // per, 0, j % per)),
        out_shape=jax.ShapeDtypeStruct((6, r, d), F32),
        compiler_params=_params("arbitrary"),
        name="ada",
    )(c_pad, w_ada, b_ada.reshape(1, 6 * d))


def _modulated_norm(x, g, scale, shift):
    ms = jnp.mean(x * x, axis=-1, keepdims=True)
    return (x * lax.rsqrt(ms + EPS) * g) * (1.0 + scale) + shift


def _head_norm(a, gain_ref):
    outs = []
    for hh in range(a.shape[1] // HEAD_DIM):
        blk = a[:, hh * HEAD_DIM:(hh + 1) * HEAD_DIM]
        ms = jnp.mean(blk * blk, axis=-1, keepdims=True)
        outs.append(blk * lax.rsqrt(ms + EPS) * gain_ref[:, hh * HEAD_DIM:(hh + 1) * HEAD_DIM])
    return jnp.concatenate(outs, axis=1)


def _proj_kernel(x_ref, g_ref, sc_ref, sh_ref, w_ref, qn_ref, kn_ref,
                 proj_ref, ka_ref, va_ref, kb_ref, vb_ref, h_scr, *, gb, per):
    j = pl.program_id(1)

    @pl.when(j == 0)
    def _():
        def body(s, carry):
            h = _modulated_norm(x_ref[s], g_ref[...], sc_ref[0, s], sh_ref[0, s])
            h_scr[pl.ds(pl.multiple_of(s * ROW_GROUP, ROW_GROUP), ROW_GROUP), :] = h.astype(BF16)
            return carry
        lax.fori_loop(0, gb, body, 0)

    acc = _dot(h_scr[...], w_ref[...])
    sec = j // per

    @pl.when(sec == 0)
    def _():
        proj_ref[...] = (acc * QK_SCALE).astype(BF16)

    @pl.when(sec == 1)
    def _():
        proj_ref[...] = acc.astype(BF16)
        ka_ref[...] = acc

    @pl.when(sec == 2)
    def _():
        proj_ref[...] = acc.astype(BF16)
        va_ref[...] = acc

    @pl.when(sec == 3)
    def _():
        proj_ref[...] = (_head_norm(acc, qn_ref) * QK_SCALE).astype(BF16)

    @pl.when(sec == 4)
    def _():
        n = _head_norm(acc, kn_ref)
        proj_ref[...] = n.astype(BF16)
        kb_ref[...] = n

    @pl.when(sec == 5)
    def _():
        proj_ref[...] = acc.astype(BF16)
        vb_ref[...] = acc

    @pl.when(sec >= 6)
    def _():
        proj_ref[...] = acc.astype(BF16)


MOD_SHIFT_M, MOD_SCALE_M, MOD_GATE_M, MOD_SHIFT_F, MOD_SCALE_F, MOD_GATE_F = range(6)


def _proj(x3, g, mods, w_in_bf, qn, kn, gb):
    ng, _, d = x3.shape
    d_in = w_in_bf.shape[1]
    m = ng * ROW_GROUP
    tm = gb * ROW_GROUP
    tn = 512
    per = W_HEADS // tn

    def sect(s):
        return lambda i, j: (i, jnp.clip(j - s * per, 0, per - 1))

    def mod_spec(which):
        return pl.BlockSpec((1, gb, 1, d), lambda i, j: (which, i, 0, 0))

    kv_shape = jax.ShapeDtypeStruct((m, W_HEADS), F32)
    return pl.pallas_call(
        functools.partial(_proj_kernel, gb=gb, per=per),
        grid=(ng // gb, d_in // tn),
        in_specs=[
            pl.BlockSpec((gb, ROW_GROUP, d), lambda i, j: (i, 0, 0)),
            pl.BlockSpec((1, d), lambda i, j: (0, 0)),
            mod_spec(MOD_SCALE_M), mod_spec(MOD_SHIFT_M),
            pl.BlockSpec((d, tn), lambda i, j: (0, j)),
            pl.BlockSpec((1, tn), lambda i, j: (0, jnp.clip(j - 3 * per, 0, per - 1))),
            pl.BlockSpec((1, tn), lambda i, j: (0, jnp.clip(j - 4 * per, 0, per - 1))),
        ],
        out_specs=[
            pl.BlockSpec((tm, tn), lambda i, j: (i, j)),
            pl.BlockSpec((tm, tn), sect(1)),
            pl.BlockSpec((tm, tn), sect(2)),
            pl.BlockSpec((tm, tn), sect(4)),
            pl.BlockSpec((tm, tn), sect(5)),
        ],
        out_shape=[jax.ShapeDtypeStruct((m, d_in), BF16), kv_shape, kv_shape, kv_shape, kv_shape],
        scratch_shapes=[pltpu.VMEM((tm, d), BF16)],
        compiler_params=_params("arbitrary", "arbitrary"),
        name="proj",
    )(x3, g, mods, mods, w_in_bf, qn, kn)


def _strict_lower(n):
    row = lax.broadcasted_iota(jnp.int32, (n, n), 0)
    col = lax.broadcasted_iota(jnp.int32, (n, n), 1)
    return jnp.where(row > col, 1.0, 0.0).astype(BF16)


def _sb_block(q, k, v, u, carry, diagonal):
    z = _dot_nt(q, k)
    sp = jnp.maximum(z, 0.0) + jnp.log(1.0 + jnp.exp(-jnp.abs(z)))
    if diagonal:
        row = lax.broadcasted_iota(jnp.int32, z.shape, 0)
        col = lax.broadcasted_iota(jnp.int32, z.shape, 1)
        valid = col < row
        sp = jnp.where(valid, sp, 0.0)
    hi = sp.astype(BF16)
    lo = (sp - hi.astype(F32)).astype(BF16)
    tail = _dot(hi, u) + _dot(lo, u)
    loga = z - sp - tail - carry
    if diagonal:
        loga = jnp.where(valid, loga, NEG_BIG)
    a = jnp.exp(loga)
    return _dot(a.astype(BF16), v), carry + tail[:, 0:1] + sp[:, 0:1]


def _sb_prompt_kernel(q_ref, k_ref, v_ref, o_ref, *, t):
    qi = pl.program_id(2)
    q = q_ref[0]
    u = _strict_lower(t)
    d0 = pl.multiple_of(qi * t, t)
    acc, carry = _sb_block(q, k_ref[0, pl.ds(d0, t), :], v_ref[0, pl.ds(d0, t), :], u,
                           jnp.zeros((t, 1), F32), True)

    def body(step, c):
        acc, carry = c
        r = pl.multiple_of((qi - 1 - step) * t, t)
        pv, carry = _sb_block(q, k_ref[0, pl.ds(r, t), :], v_ref[0, pl.ds(r, t), :], u, carry, False)
        return acc + pv, carry

    acc, carry = lax.fori_loop(0, qi, body, (acc, carry))
    o_ref[0] = acc.astype(BF16)


def _sb_prompt(proj3):
    b, s, _ = proj3.shape
    t = 256
    return pl.pallas_call(
        functools.partial(_sb_prompt_kernel, t=t),
        grid=(b, N_HEADS, s // t),
        in_specs=[
            pl.BlockSpec((1, t, HEAD_DIM), lambda bi, h, qi: (bi, qi, h)),
            pl.BlockSpec((1, s, HEAD_DIM), lambda bi, h, qi: (bi, 0, N_HEADS + h)),
            pl.BlockSpec((1, s, HEAD_DIM), lambda bi, h, qi: (bi, 0, 2 * N_HEADS + h)),
        ],
        out_specs=pl.BlockSpec((1, t, HEAD_DIM), lambda bi, h, qi: (bi, qi, h)),
        out_shape=jax.ShapeDtypeStruct((b, s, W_HEADS), BF16),
        compiler_params=_params("arbitrary", "arbitrary", "arbitrary"),
        name="sb_prompt",
    )(proj3, proj3, proj3)


def _sb_step_kernel(q_ref, kn_ref, vn_ref, ck_ref, cv_ref, o_ref, acc_scr, carry_scr, *, tkb, tk, t):
    j = pl.program_id(1)
    u = _strict_lower(tk)

    @pl.when(j == 0)
    def _():
        un = _strict_lower(t)
        for h in range(N_HEADS):
            hs = slice(h * HEAD_DIM, (h + 1) * HEAD_DIM)
            pv, carry = _sb_block(q_ref[0, :, hs], kn_ref[0, :, hs], vn_ref[0, :, hs], un,
                                  jnp.zeros((t, 1), F32), True)
            acc_scr[h] = pv
            carry_scr[h] = jnp.broadcast_to(carry, (t, V7X_LANES))

    for h in range(N_HEADS):
        hs = slice(h * HEAD_DIM, (h + 1) * HEAD_DIM)
        q = q_ref[0, :, hs]

        def body(step, c, hs=hs, q=q):
            acc, carry = c
            r = pl.multiple_of(tkb - (step + 1) * tk, tk)
            k = ck_ref[0, pl.ds(r, tk), hs].astype(BF16)
            v = cv_ref[0, pl.ds(r, tk), hs].astype(BF16)
            pv, carry = _sb_block(q, k, v, u, carry, False)
            return acc + pv, carry

        acc, carry = lax.fori_loop(0, tkb // tk, body, (acc_scr[h], carry_scr[h][:, 0:1]))
        acc_scr[h] = acc
        carry_scr[h] = jnp.broadcast_to(carry, (t, V7X_LANES))

    @pl.when(j == pl.num_programs(1) - 1)
    def _():
        for h in range(N_HEADS):
            o_ref[0, :, h * HEAD_DIM:(h + 1) * HEAD_DIM] = acc_scr[h].astype(BF16)


def _sb_step(proj3, cache_k, cache_v):
    b, t, _ = proj3.shape
    p = cache_k.shape[1]
    tkb = min(1024, p)
    tk = 256
    nj = p // tkb
    cache_spec = pl.BlockSpec((1, tkb, W_HEADS), lambda bi, j: (bi, nj - 1 - j, 0))
    return pl.pallas_call(
        functools.partial(_sb_step_kernel, tkb=tkb, tk=tk, t=t),
        grid=(b, nj),
        in_specs=[
            pl.BlockSpec((1, t, W_HEADS), lambda bi, j: (bi, 0, 0)),
            pl.BlockSpec((1, t, W_HEADS), lambda bi, j: (bi, 0, 1)),
            pl.BlockSpec((1, t, W_HEADS), lambda bi, j: (bi, 0, 2)),
            cache_spec, cache_spec,
        ],
        out_specs=pl.BlockSpec((1, t, W_HEADS), lambda bi, j: (bi, 0, 0)),
        out_shape=jax.ShapeDtypeStruct((b, t, W_HEADS), BF16),
        scratch_shapes=[pltpu.VMEM((N_HEADS, t, HEAD_DIM), F32), pltpu.VMEM((N_HEADS, t, V7X_LANES), F32)],
        compiler_params=_params("arbitrary", "arbitrary"),
        name="sb_step",
    )(proj3, proj3, proj3, cache_k, cache_v)


BAND_QB = 4 * CHUNK
BAND_KB = 3


def _band_bias(table):
    i = jnp.arange(BAND_QB)[:, None]
    j = jnp.arange(BAND_KB * BAND_QB)[None, :]
    rel = jnp.clip(i - j + BAND_PAST, -REL_CLIP, REL_CLIP) + REL_CLIP
    dchunk = j // CHUNK - i // CHUNK
    inside = (dchunk >= 0) & (dchunk <= BAND_LEFT_CHUNKS)
    return jnp.where(inside[None], table[:, rel].astype(F32), NEG_BIG)


def _softmax_pv(scores, values):
    m = scores[0].max(axis=1, keepdims=True)
    for s in scores[1:]:
        m = jnp.maximum(m, s.max(axis=1, keepdims=True))
    ps = [jnp.exp(s - m) for s in scores]
    den = ps[0].sum(axis=1, keepdims=True)
    for p in ps[1:]:
        den = den + p.sum(axis=1, keepdims=True)
    num = _dot(ps[0].astype(BF16), values[0])
    for p, v in zip(ps[1:], values[1:]):
        num = num + _dot(p.astype(BF16), v)
    return num / den


def _band_prompt_kernel(q_ref, k0_ref, k1_ref, k2_ref, v0_ref, v1_ref, v2_ref, bias_ref, o_ref):
    qi = pl.program_id(2)
    q = q_ref[0]
    scores = []
    for c, k_ref in enumerate((k0_ref, k1_ref, k2_ref)):
        s = _dot_nt(q, k_ref[0]) + bias_ref[0, :, c * BAND_QB:(c + 1) * BAND_QB]
        if c < BAND_KB - 1:
            s = jnp.where(qi + c >= BAND_KB - 1, s, NEG_BIG)
        scores.append(s)
    o_ref[0] = _softmax_pv(scores, (v0_ref[0], v1_ref[0], v2_ref[0])).astype(BF16)


def _band_prompt(proj3, bias):
    b, s, _ = proj3.shape
    qcol, kcol, vcol = 3 * N_HEADS, 4 * N_HEADS, 5 * N_HEADS

    def kv_spec(col, back):
        return pl.BlockSpec((1, BAND_QB, HEAD_DIM),
                            lambda h, bi, qi: (bi, jnp.maximum(qi - back, 0), col + h))

    return pl.pallas_call(
        _band_prompt_kernel,
        grid=(N_HEADS, b, s // BAND_QB),
        in_specs=[
            pl.BlockSpec((1, BAND_QB, HEAD_DIM), lambda h, bi, qi: (bi, qi, qcol + h)),
            kv_spec(kcol, 2), kv_spec(kcol, 1), kv_spec(kcol, 0),
            kv_spec(vcol, 2), kv_spec(vcol, 1), kv_spec(vcol, 0),
            pl.BlockSpec((1, BAND_QB, BAND_KB * BAND_QB), lambda h, bi, qi: (h, 0, 0)),
        ],
        out_specs=pl.BlockSpec((1, BAND_QB, HEAD_DIM), lambda h, bi, qi: (bi, qi, h)),
        out_shape=jax.ShapeDtypeStruct((b, s, W_HEADS), BF16),
        compiler_params=_params("arbitrary", "arbitrary", "arbitrary"),
        name="band_prompt",
    )(proj3, proj3, proj3, proj3, proj3, proj3, proj3, bias)


def _band_step_kernel(q_ref, kn_ref, vn_ref, ck_ref, cv_ref, bias_ref, o_ref, *, lb):
    for h in range(N_HEADS):
        hs = slice(h * HEAD_DIM, (h + 1) * HEAD_DIM)
        q = q_ref[0, :, hs]
        s_cache = _dot_nt(q, ck_ref[0, :, hs].astype(BF16)) + bias_ref[h, :, :lb]
        s_new = _dot_nt(q, kn_ref[0, :, hs]) + bias_ref[h, :, lb:]
        out = _softmax_pv((s_cache, s_new), (cv_ref[0, :, hs].astype(BF16), vn_ref[0, :, hs]))
        o_ref[0, :, hs] = out.astype(BF16)


def _band_step(proj3, cache_k, cache_v, bias):
    b, t, _ = proj3.shape
    lb = cache_k.shape[1]
    cache_spec = pl.BlockSpec((1, lb, W_HEADS), lambda bi: (bi, 0, 0))
    return pl.pallas_call(
        functools.partial(_band_step_kernel, lb=lb),
        grid=(b,),
        in_specs=[
            pl.BlockSpec((1, t, W_HEADS), lambda bi: (bi, 0, 3)),
            pl.BlockSpec((1, t, W_HEADS), lambda bi: (bi, 0, 4)),
            pl.BlockSpec((1, t, W_HEADS), lambda bi: (bi, 0, 5)),
            cache_spec, cache_spec,
            pl.BlockSpec((N_HEADS, t, lb + t), lambda bi: (0, 0, 0)),
        ],
        out_specs=pl.BlockSpec((1, t, W_HEADS), lambda bi: (bi, 0, 0)),
        out_shape=jax.ShapeDtypeStruct((b, t, W_HEADS), BF16),
        compiler_params=_params("arbitrary"),
        name="band_step",
    )(proj3, proj3, proj3, cache_k, cache_v, bias)


def _merge_kernel(osb_ref, obd_ref, wsb_ref, wbd_ref, ga_ref, gb_ref, o_ref):
    a = _dot(osb_ref[...], wsb_ref[...])
    b = _dot(obd_ref[...], wbd_ref[...])
    merged = _sigmoid(ga_ref[...].astype(F32)) * a + _sigmoid(gb_ref[...].astype(F32)) * b
    o_ref[...] = merged.astype(BF16)


def _merge(o_sb, o_band, w_sb_bf, w_band_bf, proj, tm):
    m = o_sb.shape[0]
    d = w_sb_bf.shape[1]
    tn = min(512, d)
    ga_col = 6 * W_HEADS // tn
    gb_col = ga_col + d // tn
    return pl.pallas_call(
        _merge_kernel,
        grid=(m // tm, d // tn),
        in_specs=[
            pl.BlockSpec((tm, W_HEADS), lambda i, j: (i, 0)),
            pl.BlockSpec((tm, W_HEADS), lambda i, j: (i, 0)),
            pl.BlockSpec((W_HEADS, tn), lambda i, j: (0, j)),
            pl.BlockSpec((W_HEADS, tn), lambda i, j: (0, j)),
            pl.BlockSpec((tm, tn), lambda i, j: (i, ga_col + j)),
            pl.BlockSpec((tm, tn), lambda i, j: (i, gb_col + j)),
        ],
        out_specs=pl.BlockSpec((tm, tn), lambda i, j: (i, j)),
        out_shape=jax.ShapeDtypeStruct((m, d), BF16),
        compiler_params=_params("arbitrary", "arbitrary"),
        name="merge",
    )(o_sb, o_band, w_sb_bf, w_band_bf, proj, proj)


def _outproj_kernel(m_ref, w_ref, x_ref, gate_ref, o_ref, *, gb):
    acc = _dot(m_ref[...], w_ref[...])
    acc = acc.reshape(gb, ROW_GROUP, acc.shape[1])
    o_ref[...] = x_ref[...] + gate_ref[0] * acc


def _outproj(merged, w_out_bf, x3, mods, gb):
    ng, _, d = x3.shape
    tm = gb * ROW_GROUP
    tn = min(512, d)
    return pl.pallas_call(
        functools.partial(_outproj_kernel, gb=gb),
        grid=(ng // gb, d // tn),
        in_specs=[
            pl.BlockSpec((tm, d), lambda i, j: (i, 0)),
            pl.BlockSpec((d, tn), lambda i, j: (0, j)),
            pl.BlockSpec((gb, ROW_GROUP, tn), lambda i, j: (i, 0, j)),
            pl.BlockSpec((1, gb, 1, tn), lambda i, j: (MOD_GATE_M, i, 0, j)),
        ],
        out_specs=pl.BlockSpec((gb, ROW_GROUP, tn), lambda i, j: (i, 0, j)),
        out_shape=jax.ShapeDtypeStruct(x3.shape, F32),
        compiler_params=_params("arbitrary", "arbitrary"),
        name="outproj",
    )(merged, w_out_bf, x3, mods)


ROUTE_E1, ROUTE_E2, ROUTE_W1, ROUTE_W2 = 0, 1, 2, 3


def _router_kernel(x_ref, g_ref, sc_ref, sh_ref, whi_ref, wlo_ref, b_ref, h_ref, route_ref, lo_scr, *, gb):
    def body(s, carry):
        h = _modulated_norm(x_ref[s], g_ref[...], sc_ref[0, s], sh_ref[0, s])
        hi = h.astype(BF16)
        rows = pl.ds(pl.multiple_of(s * ROW_GROUP, ROW_GROUP), ROW_GROUP)
        h_ref[rows, :] = hi
        lo_scr[rows, :] = (h - hi.astype(F32)).astype(BF16)
        return carry
    lax.fori_loop(0, gb, body, 0)

    hi = h_ref[...]
    logits = _dot(hi, whi_ref[...]) + _dot(hi, wlo_ref[...]) + _dot(lo_scr[...], whi_ref[...]) + b_ref[...]
    lane = lax.broadcasted_iota(jnp.int32, logits.shape, 1)
    ninf = -jnp.inf

    lane_f = lane.astype(F32)

    def first_max(vals):
        mx = vals.max(axis=1, keepdims=True)
        idx = jnp.where(vals == mx, lane_f, float(V7X_LANES)).min(axis=1, keepdims=True)
        return mx, idx

    gl = jnp.where(lane < N_GROUPS, logits, ninf)
    gmax, gidx = first_max(gl)
    g_weight = 1.0 / jnp.exp(gl - gmax).sum(axis=1, keepdims=True)
    lo_lane = N_GROUPS + EXPERTS_PER_GROUP * gidx
    el = jnp.where(lane_f >= lo_lane, jnp.where(lane_f < lo_lane + EXPERTS_PER_GROUP, logits, ninf), ninf)
    m1, i1 = first_max(el)
    m2, i2 = first_max(jnp.where(lane_f == i1, ninf, el))
    e21 = jnp.exp(m2 - m1)
    p1 = 1.0 / (1.0 + e21)
    p2 = e21 / (1.0 + e21)
    route = jnp.where(lane == ROUTE_E1, i1 - N_GROUPS,
                      jnp.where(lane == ROUTE_E2, i2 - N_GROUPS,
                                jnp.where(lane == ROUTE_W1, g_weight * p1,
                                          jnp.where(lane == ROUTE_W2, g_weight * p2, 0.0))))
    route_ref[...] = route


def _router(x3, g, mods, w_hi, w_lo, bias, gb):
    ng, _, d = x3.shape
    m = ng * ROW_GROUP
    tm = gb * ROW_GROUP

    def mod_spec(which):
        return pl.BlockSpec((1, gb, 1, d), lambda i: (which, i, 0, 0))

    w_spec = pl.BlockSpec((d, V7X_LANES), lambda i: (0, 0))
    return pl.pallas_call(
        functools.partial(_router_kernel, gb=gb),
        grid=(ng // gb,),
        in_specs=[
            pl.BlockSpec((gb, ROW_GROUP, d), lambda i: (i, 0, 0)),
            pl.BlockSpec((1, d), lambda i: (0, 0)),
            mod_spec(MOD_SCALE_F), mod_spec(MOD_SHIFT_F), w_spec, w_spec,
            pl.BlockSpec((1, V7X_LANES), lambda i: (0, 0)),
        ],
        out_specs=[
            pl.BlockSpec((tm, d), lambda i: (i, 0)),
            pl.BlockSpec((tm, V7X_LANES), lambda i: (i, 0)),
        ],
        out_shape=[jax.ShapeDtypeStruct((m, d), BF16), jax.ShapeDtypeStruct((m, V7X_LANES), F32)],
        scratch_shapes=[pltpu.VMEM((tm, d), BF16)],
        compiler_params=_params("arbitrary"),
        name="router",
    )(x3, g, mods, mods, w_hi, w_lo, bias)


def _moe_kernel(te_ref, nv_ref, x_ref, wg_ref, wu_ref, wd_ref, o_ref, wg_scr, wu_scr, wd_scr):
    t = pl.program_id(0)
    changed = te_ref[t] != te_ref[jnp.maximum(t - 1, 0)]

    @pl.when((t == 0) | changed)
    def _():
        wg_scr[...] = wg_ref[0].astype(BF16)
        wu_scr[...] = wu_ref[0].astype(BF16)
        wd_scr[...] = wd_ref[0].astype(BF16)

    @pl.when(t < nv_ref[0])
    def _():
        x = x_ref[...]
        g = _dot(x, wg_scr[...])
        u = _dot(x, wu_scr[...])
        hidden = (g * _sigmoid(g)) * u
        o_ref[...] = _dot(hidden.astype(BF16), wd_scr[...]).astype(BF16)

    @pl.when(t >= nv_ref[0])
    def _():
        o_ref[...] = jnp.zeros(o_ref.shape, BF16)


def _moe(tile_expert, n_valid, xs, w_gate, w_up, w_down):
    rows, d = xs.shape
    f = w_gate.shape[2]
    n_tiles = rows // MOE_TILE

    def x_map(t, te, nv):
        return (jnp.minimum(t, nv[0] - 1), 0)

    grid_spec = pltpu.PrefetchScalarGridSpec(
        num_scalar_prefetch=2,
        grid=(n_tiles,),
        in_specs=[
            pl.BlockSpec((MOE_TILE, d), x_map),
            pl.BlockSpec((1, d, f), lambda t, te, nv: (te[t], 0, 0)),
            pl.BlockSpec((1, d, f), lambda t, te, nv: (te[t], 0, 0)),
            pl.BlockSpec((1, f, d), lambda t, te, nv: (te[t], 0, 0)),
        ],
        out_specs=pl.BlockSpec((MOE_TILE, d), lambda t, te, nv: (t, 0)),
        scratch_shapes=[pltpu.VMEM((d, f), BF16), pltpu.VMEM((d, f), BF16), pltpu.VMEM((f, d), BF16)],
    )
    return pl.pallas_call(
        _moe_kernel,
        grid_spec=grid_spec,
        out_shape=jax.ShapeDtypeStruct((rows, d), BF16),
        compiler_params=_params("arbitrary"),
        name="moe",
    )(tile_expert, n_valid, xs, w_gate, w_up, w_down)


def _dispatch(route):
    n = route.shape[0]
    e = jnp.concatenate([route[:, ROUTE_E1], route[:, ROUTE_E2]]).astype(jnp.int32)
    tok = jnp.concatenate([jnp.arange(n, dtype=jnp.int32)] * 2)
    onehot = (e[:, None] == jnp.arange(N_EXPERTS, dtype=jnp.int32)[None, :]).astype(jnp.int32)
    before = jnp.cumsum(onehot, axis=0) - onehot
    rank = jnp.sum(before * onehot, axis=1)
    counts = jnp.sum(onehot, axis=0)
    padded = ((counts + MOE_TILE - 1) // MOE_TILE) * MOE_TILE
    ends = jnp.cumsum(padded)
    pos = (ends - padded)[e] + rank
    n_tiles = (2 * n + N_EXPERTS * (MOE_TILE - 1)) // MOE_TILE
    tile_start = jnp.arange(n_tiles, dtype=jnp.int32) * MOE_TILE
    n_valid = (ends[-1] // MOE_TILE).astype(jnp.int32)
    tile_expert = jnp.searchsorted(ends, tile_start, side="right").astype(jnp.int32)
    last_expert = tile_expert[jnp.maximum(n_valid - 1, 0)]
    tile_expert = jnp.where(tile_start < ends[-1], tile_expert, last_expert)
    src_tok = jnp.zeros((n_tiles * MOE_TILE,), jnp.int32).at[pos].set(tok)
    return pos[:n], pos[n:], src_tok, tile_expert, n_valid.reshape(1)


def _final_kernel(x_ref, y1_ref, y2_ref, route_ref, gate_ref, o_ref):
    r = route_ref[...]
    w1 = r[:, :, ROUTE_W1:ROUTE_W1 + 1]
    w2 = r[:, :, ROUTE_W2:ROUTE_W2 + 1]
    y = w1 * y1_ref[...].astype(F32) + w2 * y2_ref[...].astype(F32)
    o_ref[...] = x_ref[...] + gate_ref[0] * y


def _final(x3, y1, y2, route3, mods, gb):
    ng, _, d = x3.shape
    tn = min(1024, d)
    blk = pl.BlockSpec((gb, ROW_GROUP, tn), lambda i, j: (i, 0, j))
    return pl.pallas_call(
        _final_kernel,
        grid=(ng // gb, d // tn),
        in_specs=[
            blk, blk, blk,
            pl.BlockSpec((gb, ROW_GROUP, V7X_LANES), lambda i, j: (i, 0, 0)),
            pl.BlockSpec((1, gb, 1, tn), lambda i, j: (MOD_GATE_F, i, 0, j)),
        ],
        out_specs=blk,
        out_shape=jax.ShapeDtypeStruct(x3.shape, F32),
        compiler_params=_params("arbitrary", "arbitrary"),
        name="final",
    )(x3, y1, y2, route3, mods)


def _layer(xp, xs, csk, csv, cbk, cbv, c_prompt, c_sample, norm_mix, norm_ffn, w_ada, b_ada, w_in, q_norm,
           k_norm, rel_table, w_proj_sb, w_proj_band, w_out, w_rg, b_rg, w_re, b_re, w_gate, w_up, w_down):
    bp, sp_len, d = xp.shape
    bs, ts, _ = xs.shape
    d_in = w_in.shape[1]
    assert sp_len % BAND_QB == 0 and ts == ROW_GROUP and sp_len % ROW_GROUP == 0

    n_c = bp + bs
    c_pad = jnp.concatenate([c_prompt, c_sample, jnp.zeros((-n_c % 8, d), F32)], axis=0)
    mod = _ada(c_pad, w_ada, b_ada)
    gp = sp_len // ROW_GROUP
    mod_p = jnp.repeat(mod[:, :bp], gp, axis=1)[:, :, None, :]
    mod_s = mod[:, bp:n_c][:, :, None, :]

    w_in_bf = w_in.astype(BF16)
    w_sb_bf = w_proj_sb.astype(BF16)
    w_band_bf = w_proj_band.astype(BF16)
    w_out_bf = w_out.astype(BF16)
    qn = q_norm.reshape(1, W_HEADS)
    kn = k_norm.reshape(1, W_HEADS)
    g_mix = norm_mix.reshape(1, d)
    g_ffn = norm_ffn.reshape(1, d)
    bias = _band_bias(rel_table)
    lb = cbk.shape[1]
    rel_step = jnp.arange(ts)[:, None] - jnp.arange(lb + ts)[None, :] + lb
    bias_step = rel_table[:, jnp.clip(rel_step, -REL_CLIP, REL_CLIP) + REL_CLIP].astype(F32)

    w_router = jnp.zeros((d, V7X_LANES), F32).at[:, :N_GROUPS].set(w_rg)
    w_router = w_router.at[:, N_GROUPS:N_GROUPS + N_EXPERTS].set(w_re)
    w_router_hi = w_router.astype(BF16)
    w_router_lo = (w_router - w_router_hi.astype(F32)).astype(BF16)
    b_router = jnp.zeros((1, V7X_LANES), F32).at[0, :N_GROUPS].set(b_rg)
    b_router = b_router.at[0, N_GROUPS:N_GROUPS + N_EXPERTS].set(b_re.reshape(-1))

    xp3 = xp.reshape(bp * gp, ROW_GROUP, d)
    xs3 = xs.reshape(bs, ROW_GROUP, d)
    gb_p = min(8, bp * gp)
    gb_s = min(8, bs)

    def mixer(x3, mods, gb, attend):
        proj, ka, va, kb, vb = _proj(x3, g_mix, mods, w_in_bf, qn, kn, gb)
        o_sb, o_band = attend(proj)
        merged = _merge(o_sb, o_band, w_sb_bf, w_band_bf, proj, gb * ROW_GROUP)
        x1 = _outproj(merged, w_out_bf, x3, mods, gb)
        h2, route = _router(x1, g_ffn, mods, w_router_hi, w_router_lo, b_router, gb)
        return x1, h2, route, ka, va, kb, vb

    def attend_prompt(proj):
        p3 = proj.reshape(bp, sp_len, d_in)
        return (_sb_prompt(p3).reshape(bp * sp_len, W_HEADS),
                _band_prompt(p3, bias).reshape(bp * sp_len, W_HEADS))

    def attend_sample(proj):
        p3 = proj.reshape(bs, ts, d_in)
        o_sb = _sb_step(p3, csk.reshape(bs, -1, W_HEADS), csv.reshape(bs, -1, W_HEADS))
        o_band = _band_step(p3, cbk.reshape(bs, lb, W_HEADS), cbv.reshape(bs, lb, W_HEADS), bias_step)
        return o_sb.reshape(bs * ts, W_HEADS), o_band.reshape(bs * ts, W_HEADS)

    x1p, h2p, route_p, kap, vap, kbp, vbp = mixer(xp3, mod_p, gb_p, attend_prompt)
    x1s, h2s, route_s, kas, vas, kbs, vbs = mixer(xs3, mod_s, gb_s, attend_sample)

    n_p = bp * sp_len
    h2 = jnp.concatenate([h2p, h2s], axis=0)
    route = jnp.concatenate([route_p, route_s], axis=0)
    pos1, pos2, src_tok, tile_expert, n_valid = _dispatch(route)
    y_sorted = _moe(tile_expert, n_valid, jnp.take(h2, src_tok, axis=0), w_gate, w_up, w_down)
    y1 = jnp.take(y_sorted, pos1, axis=0)
    y2 = jnp.take(y_sorted, pos2, axis=0)

    def finish(x1, y1, y2, route, mods, gb):
        ng = x1.shape[0]
        return _final(x1, y1.reshape(ng, ROW_GROUP, d), y2.reshape(ng, ROW_GROUP, d),
                      route.reshape(ng, ROW_GROUP, V7X_LANES), mods, gb)

    yp = finish(x1p, y1[:n_p], y2[:n_p], route_p, mod_p, gb_p).reshape(bp, sp_len, d)
    ys = finish(x1s, y1[n_p:], y2[n_p:], route_s, mod_s, gb_s).reshape(bs, ts, d)

    def heads(a, b):
        return a.reshape(b, -1, N_HEADS, HEAD_DIM)

    n_band = min(BAND_PAST, sp_len)
    return (yp, ys, heads(kap, bp), heads(vap, bp), heads(kbp, bp)[:, -n_band:], heads(vbp, bp)[:, -n_band:],
            heads(kas, bs), heads(vas, bs), heads(kbs, bs), heads(vbs, bs))


def kernel(x_prompt, x_sample, cache_sb_k, cache_sb_v, cache_band_k, cache_band_v, c_prompt, c_sample, norm_mix, norm_ffn, w_ada, b_ada, w_in, q_norm_band, k_norm_band, rel_bias_band, w_proj_sb, w_proj_band, w_out, w_router_group, b_router_group, w_router_expert, b_router_expert, w_gate, w_up, w_down):
    depth = w_in.shape[0]
    xp, xs = x_prompt, x_sample
    outs = [[] for _ in range(8)]
    for l in range(depth):
        res = _layer(xp, xs, cache_sb_k[l], cache_sb_v[l], cache_band_k[l], cache_band_v[l], c_prompt, c_sample,
                     norm_mix[l], norm_ffn[l], w_ada[l], b_ada[l], w_in[l], q_norm_band[l], k_norm_band[l],
                     rel_bias_band[l], w_proj_sb[l], w_proj_band[l], w_out[l], w_router_group[l],
                     b_router_group[l], w_router_expert[l], b_router_expert[l], w_gate[l], w_up[l], w_down[l])
        xp, xs = res[0], res[1]
        for acc, r in zip(outs, res[2:]):
            acc.append(r)
    return (xp, xs) + tuple(jnp.stack(o, axis=0) for o in outs)
```

```python
import functools

import jax
import jax.numpy as jnp
from jax import lax
from jax.experimental import pallas as pl
from jax.experimental.pallas import tpu as pltpu

F32 = jnp.float32
BF16 = jnp.bfloat16

EPS = 1e-6
HEAD_DIM = 128
N_HEADS = 8
W_HEADS = N_HEADS * HEAD_DIM
CHUNK = 64
BAND_LEFT_CHUNKS = 8
BAND_PAST = BAND_LEFT_CHUNKS * CHUNK
REL_CLIP = 128
N_GROUPS = 4
EXPERTS_PER_GROUP = 8
N_EXPERTS = N_GROUPS * EXPERTS_PER_GROUP
QK_SCALE = HEAD_DIM ** -0.5
NEG_BIG = -1e30

V7X_LANES = 128
V7X_VMEM_LIMIT = 56 * 1024 * 1024
ROW_GROUP = 64
MOE_TILE = 256


def _params(*sem):
    return pltpu.CompilerParams(dimension_semantics=sem, vmem_limit_bytes=V7X_VMEM_LIMIT)


def _sigmoid(x):
    return 1.0 / (1.0 + jnp.exp(-x))


def _dot(a, b):
    return jnp.dot(a, b, preferred_element_type=F32)


def _dot_nt(a, b):
    return lax.dot_general(a, b, (((1,), (1,)), ((), ())), preferred_element_type=F32)


def _ada_kernel(c_ref, w_ref, b_ref, o_ref):
    c = c_ref[...]
    a = (c * _sigmoid(c)).astype(BF16)
    o_ref[0] = _dot(a, w_ref[...].astype(BF16)) + b_ref[...]


def _ada(c_pad, w_ada, b_ada):
    r, d = c_pad.shape
    tn = min(1024, d)
    per = d // tn
    return pl.pallas_call(
        _ada_kernel,
        grid=(6 * per,),
        in_specs=[
            pl.BlockSpec((r, d), lambda j: (0, 0)),
            pl.BlockSpec((d, tn), lambda j: (0, j)),
            pl.BlockSpec((1, tn), lambda j: (0, j)),
        ],
        out_specs=pl.BlockSpec((1, r, tn), lambda j: (j // per, 0, j % per)),
        out_shape=jax.ShapeDtypeStruct((6, r, d), F32),
        compiler_params=_params("arbitrary"),
        name="ada",
    )(c_pad, w_ada, b_ada.reshape(1, 6 * d))


def _modulated_norm(x, g, scale, shift):
    ms = jnp.mean(x * x, axis=-1, keepdims=True)
    return (x * lax.rsqrt(ms + EPS) * g) * (1.0 + scale) + shift


def _head_norm(a, gain_ref):
    outs = []
    for hh in range(a.shape[1] // HEAD_DIM):
        blk = a[:, hh * HEAD_DIM:(hh + 1) * HEAD_DIM]
        ms = jnp.mean(blk * blk, axis=-1, keepdims=True)
        outs.append(blk * lax.rsqrt(ms + EPS) * gain_ref[:, hh * HEAD_DIM:(hh + 1) * HEAD_DIM])
    return jnp.concatenate(outs, axis=1)


def _proj_kernel(x_ref, g_ref, sc_ref, sh_ref, w_ref, qn_ref, kn_ref,
                 proj_ref, ka_ref, va_ref, kb_ref, vb_ref, h_scr, *, gb, per):
    j = pl.program_id(1)

    @pl.when(j == 0)
    def _():
        def body(s, carry):
            h = _modulated_norm(x_ref[s], g_ref[...], sc_ref[0, s], sh_ref[0, s])
            h_scr[pl.ds(pl.multiple_of(s * ROW_GROUP, ROW_GROUP), ROW_GROUP), :] = h.astype(BF16)
            return carry
        lax.fori_loop(0, gb, body, 0)

    acc = _dot(h_scr[...], w_ref[...])
    sec = j // per

    @pl.when(sec == 0)
    def _():
        proj_ref[...] = (acc * QK_SCALE).astype(BF16)

    @pl.when(sec == 1)
    def _():
        proj_ref[...] = acc.astype(BF16)
        ka_ref[...] = acc

    @pl.when(sec == 2)
    def _():
        proj_ref[...] = acc.astype(BF16)
        va_ref[...] = acc

    @pl.when(sec == 3)
    def _():
        proj_ref[...] = (_head_norm(acc, qn_ref) * QK_SCALE).astype(BF16)

    @pl.when(sec == 4)
    def _():
        n = _head_norm(acc, kn_ref)
        proj_ref[...] = n.astype(BF16)
        kb_ref[...] = n

    @pl.when(sec == 5)
    def _():
        proj_ref[...] = acc.astype(BF16)
        vb_ref[...] = acc

    @pl.when(sec >= 6)
    def _():
        proj_ref[...] = acc.astype(BF16)


MOD_SHIFT_M, MOD_SCALE_M, MOD_GATE_M, MOD_SHIFT_F, MOD_SCALE_F, MOD_GATE_F = range(6)


def _proj(x3, g, mods, w_in_bf, qn, kn, gb):
    ng, _, d = x3.shape
    d_in = w_in_bf.shape[1]
    m = ng * ROW_GROUP
    tm = gb * ROW_GROUP
    tn = 512
    per = W_HEADS // tn

    def sect(s):
        return lambda i, j: (i, jnp.clip(j - s * per, 0, per - 1))

    def mod_spec(which):
        return pl.BlockSpec((1, gb, 1, d), lambda i, j: (which, i, 0, 0))

    kv_shape = jax.ShapeDtypeStruct((m, W_HEADS), F32)
    return pl.pallas_call(
        functools.partial(_proj_kernel, gb=gb, per=per),
        grid=(ng // gb, d_in // tn),
        in_specs=[
            pl.BlockSpec((gb, ROW_GROUP, d), lambda i, j: (i, 0, 0)),
            pl.BlockSpec((1, d), lambda i, j: (0, 0)),
            mod_spec(MOD_SCALE_M), mod_spec(MOD_SHIFT_M),
            pl.BlockSpec((d, tn), lambda i, j: (0, j)),
            pl.BlockSpec((1, tn), lambda i, j: (0, jnp.clip(j - 3 * per, 0, per - 1))),
            pl.BlockSpec((1, tn), lambda i, j: (0, jnp.clip(j - 4 * per, 0, per - 1))),
        ],
        out_specs=[
            pl.BlockSpec((tm, tn), lambda i, j: (i, j)),
            pl.BlockSpec((tm, tn), sect(1)),
            pl.BlockSpec((tm, tn), sect(2)),
            pl.BlockSpec((tm, tn), sect(4)),
            pl.BlockSpec((tm, tn), sect(5)),
        ],
        out_shape=[jax.ShapeDtypeStruct((m, d_in), BF16), kv_shape, kv_shape, kv_shape, kv_shape],
        scratch_shapes=[pltpu.VMEM((tm, d), BF16)],
        compiler_params=_params("arbitrary", "arbitrary"),
        name="proj",
    )(x3, g, mods, mods, w_in_bf, qn, kn)


def _strict_lower(n):
    row = lax.broadcasted_iota(jnp.int32, (n, n), 0)
    col = lax.broadcasted_iota(jnp.int32, (n, n), 1)
    return jnp.where(row > col, 1.0, 0.0).astype(BF16)


def _sb_heads(load_qkv, n_heads, u, carries, diagonal):
    qkv = [load_qkv(h) for h in range(n_heads)]
    zs = [_dot_nt(q, k) for q, k, _ in qkv]
    if diagonal:
        row = lax.broadcasted_iota(jnp.int32, zs[0].shape, 0)
        col = lax.broadcasted_iota(jnp.int32, zs[0].shape, 1)
        valid = col < row
    sps, tails = [], []
    for z in zs:
        sp = jnp.maximum(z, 0.0) + jnp.log(1.0 + jnp.exp(-jnp.abs(z)))
        if diagonal:
            sp = jnp.where(valid, sp, 0.0)
        hi = sp.astype(BF16)
        lo = (sp - hi.astype(F32)).astype(BF16)
        sps.append(sp)
        tails.append(_dot(hi, u) + _dot(lo, u))
    pvs, new = [], []
    for h, (z, sp, tail) in enumerate(zip(zs, sps, tails)):
        tq = z.shape[0]
        carry = jnp.zeros((tq, 1), F32) if carries is None else carries[:, h * HEAD_DIM:h * HEAD_DIM + 1]
        loga = z - sp - tail - carry
        if diagonal:
            loga = jnp.where(valid, loga, NEG_BIG)
        pvs.append(_dot(jnp.exp(loga).astype(BF16), qkv[h][2]))
        new.append(jnp.broadcast_to(carry + tail[:, 0:1] + sp[:, 0:1], (tq, HEAD_DIM)))
    return jnp.concatenate(pvs, axis=1), jnp.concatenate(new, axis=1)


SB_T = 256
SB_HEADS_PER_STEP = 4


def _head_slice(h):
    return slice(h * HEAD_DIM, (h + 1) * HEAD_DIM)


def _sb_prompt_kernel(q_ref, k_ref, v_ref, u_ref, o_ref, acc_scr, carry_scr):
    qi = pl.program_id(2)
    u = u_ref[...]

    def block(r, carries, diagonal):
        def load(h):
            hs = _head_slice(h)
            return q_ref[0, :, hs], k_ref[0, pl.ds(r, SB_T), hs], v_ref[0, pl.ds(r, SB_T), hs]
        return _sb_heads(load, SB_HEADS_PER_STEP, u, carries, diagonal)

    acc_scr[...], carry_scr[...] = block(pl.multiple_of(qi * SB_T, SB_T), None, True)

    def body(step, c):
        pv, carries = block(pl.multiple_of((qi - 1 - step) * SB_T, SB_T), carry_scr[...], False)
        acc_scr[...] += pv
        carry_scr[...] = carries
        return c

    lax.fori_loop(0, qi, body, 0)
    o_ref[0] = acc_scr[...].astype(BF16)


def _sb_prompt(proj3, u):
    b, s, _ = proj3.shape
    hp = SB_HEADS_PER_STEP
    w = hp * HEAD_DIM
    kcol, vcol = W_HEADS // w, 2 * W_HEADS // w
    return pl.pallas_call(
        _sb_prompt_kernel,
        grid=(b, N_HEADS // hp, s // SB_T),
        in_specs=[
            pl.BlockSpec((1, SB_T, w), lambda bi, h, qi: (bi, qi, h)),
            pl.BlockSpec((1, s, w), lambda bi, h, qi: (bi, 0, kcol + h)),
            pl.BlockSpec((1, s, w), lambda bi, h, qi: (bi, 0, vcol + h)),
            pl.BlockSpec((SB_T, SB_T), lambda bi, h, qi: (0, 0)),
        ],
        out_specs=pl.BlockSpec((1, SB_T, w), lambda bi, h, qi: (bi, qi, h)),
        out_shape=jax.ShapeDtypeStruct((b, s, W_HEADS), BF16),
        scratch_shapes=[pltpu.VMEM((SB_T, w), F32), pltpu.VMEM((SB_T, w), F32)],
        compiler_params=_params("arbitrary", "arbitrary", "arbitrary"),
        name="sb_prompt",
    )(proj3, proj3, proj3, u)


def _sb_step_kernel(q_ref, kn_ref, vn_ref, ck_ref, cv_ref, u_ref, o_ref, acc_scr, carry_scr, *, tkb, t):
    j = pl.program_id(1)
    u = u_ref[...]

    @pl.when(j == 0)
    def _():
        def load(h):
            hs = _head_slice(h)
            return q_ref[0, :, hs], kn_ref[0, :, hs], vn_ref[0, :, hs]
        acc_scr[...], carry_scr[...] = _sb_heads(load, N_HEADS, _strict_lower(t), None, True)

    def body(step, c):
        r = pl.multiple_of((tkb - (step + 1) * SB_T) * N_HEADS, SB_T * N_HEADS)

        def load(h):
            rows = pl.ds(r + h, SB_T, stride=N_HEADS)
            return q_ref[0, :, _head_slice(h)], ck_ref[0, rows, :].astype(BF16), cv_ref[0, rows, :].astype(BF16)

        pv, carries = _sb_heads(load, N_HEADS, u, carry_scr[...], False)
        acc_scr[...] += pv
        carry_scr[...] = carries
        return c

    lax.fori_loop(0, tkb // SB_T, body, 0)

    @pl.when(j == pl.num_programs(1) - 1)
    def _():
        o_ref[0] = acc_scr[...].astype(BF16)


def _sb_step(proj3, cache_k, cache_v, u):
    b, t, _ = proj3.shape
    p = cache_k.shape[1] // N_HEADS
    tkb = min(1024, p)
    nj = p // tkb
    cache_spec = pl.BlockSpec((1, tkb * N_HEADS, HEAD_DIM), lambda bi, j: (bi, nj - 1 - j, 0))
    return pl.pallas_call(
        functools.partial(_sb_step_kernel, tkb=tkb, t=t),
        grid=(b, nj),
        in_specs=[
            pl.BlockSpec((1, t, W_HEADS), lambda bi, j: (bi, 0, 0)),
            pl.BlockSpec((1, t, W_HEADS), lambda bi, j: (bi, 0, 1)),
            pl.BlockSpec((1, t, W_HEADS), lambda bi, j: (bi, 0, 2)),
            cache_spec, cache_spec,
            pl.BlockSpec((SB_T, SB_T), lambda bi, j: (0, 0)),
        ],
        out_specs=pl.BlockSpec((1, t, W_HEADS), lambda bi, j: (bi, 0, 0)),
        out_shape=jax.ShapeDtypeStruct((b, t, W_HEADS), BF16),
        scratch_shapes=[pltpu.VMEM((t, W_HEADS), F32), pltpu.VMEM((t, W_HEADS), F32)],
        compiler_params=_params("arbitrary", "arbitrary"),
        name="sb_step",
    )(proj3, proj3, proj3, cache_k, cache_v, u)


BAND_QB = 4 * CHUNK
BAND_KB = 3


BIAS_NEAR = REL_CLIP // CHUNK + 1


def _bias_blocks_kernel(table_ref, o_ref):
    h = pl.program_id(0)
    i = lax.broadcasted_iota(jnp.int32, (CHUNK, CHUNK), 0)
    j = lax.broadcasted_iota(jnp.int32, (CHUNK, CHUNK), 1)
    for d in range(BIAS_NEAR):
        idx = jnp.clip(i - j + d * CHUNK, -REL_CLIP, REL_CLIP) + REL_CLIP

        def body(r, acc, idx=idx):
            return jnp.where(idx == r, table_ref[h, r], acc)

        o_ref[0, d] = lax.fori_loop(0, 2 * REL_CLIP + 1, body, jnp.zeros((CHUNK, CHUNK), F32))


def _bias_blocks(table):
    nh = table.shape[0]
    return pl.pallas_call(
        _bias_blocks_kernel,
        grid=(nh,),
        in_specs=[pl.BlockSpec(memory_space=pltpu.SMEM)],
        out_specs=pl.BlockSpec((1, BIAS_NEAR, CHUNK, CHUNK), lambda h: (h, 0, 0, 0)),
        out_shape=jax.ShapeDtypeStruct((nh, BIAS_NEAR, CHUNK, CHUNK), F32),
        compiler_params=_params("arbitrary"),
        name="bias_blocks",
    )(table.astype(F32))


def _chunk_bias(blocks, far, dist):
    if dist < 0 or dist > BAND_LEFT_CHUNKS:
        return jnp.full(far.shape, NEG_BIG, F32)
    return blocks[:, dist] if dist < BIAS_NEAR else far


def _band_bias(blocks, table):
    far = jnp.broadcast_to(table[:, -1].astype(F32)[:, None, None], (table.shape[0], CHUNK, CHUNK))
    qc, kc = BAND_QB // CHUNK, BAND_KB * BAND_QB // CHUNK
    rows = [jnp.concatenate([_chunk_bias(blocks, far, r - c + BAND_LEFT_CHUNKS) for c in range(kc)], axis=2)
            for r in range(qc)]
    return jnp.concatenate(rows, axis=1)


def _band_step_bias(blocks, table, lb):
    far = jnp.broadcast_to(table[:, -1].astype(F32)[:, None, None], (table.shape[0], CHUNK, CHUNK))
    nc = lb // CHUNK
    return jnp.concatenate([blocks[:, nc - c] if nc - c < BIAS_NEAR else far for c in range(nc + 1)], axis=2)


def _softmax_pv(scores, values):
    m = scores[0].max(axis=1, keepdims=True)
    for s in scores[1:]:
        m = jnp.maximum(m, s.max(axis=1, keepdims=True))
    ps = [jnp.exp(s - m) for s in scores]
    den = ps[0].sum(axis=1, keepdims=True)
    for p in ps[1:]:
        den = den + p.sum(axis=1, keepdims=True)
    num = _dot(ps[0].astype(BF16), values[0])
    for p, v in zip(ps[1:], values[1:]):
        num = num + _dot(p.astype(BF16), v)
    return num / den


def _band_prompt_kernel(q_ref, k0_ref, k1_ref, k2_ref, v0_ref, v1_ref, v2_ref, bias_ref, o_ref):
    qi = pl.program_id(2)
    q = q_ref[0]
    scores = []
    for c, k_ref in enumerate((k0_ref, k1_ref, k2_ref)):
        s = _dot_nt(q, k_ref[0]) + bias_ref[0, :, c * BAND_QB:(c + 1) * BAND_QB]
        if c < BAND_KB - 1:
            s = jnp.where(qi + c >= BAND_KB - 1, s, NEG_BIG)
        scores.append(s)
    o_ref[0] = _softmax_pv(scores, (v0_ref[0], v1_ref[0], v2_ref[0])).astype(BF16)


def _band_prompt(proj3, bias):
    b, s, _ = proj3.shape
    qcol, kcol, vcol = 3 * N_HEADS, 4 * N_HEADS, 5 * N_HEADS

    def kv_spec(col, back):
        return pl.BlockSpec((1, BAND_QB, HEAD_DIM),
                            lambda h, bi, qi: (bi, jnp.maximum(qi - back, 0), col + h))

    return pl.pallas_call(
        _band_prompt_kernel,
        grid=(N_HEADS, b, s // BAND_QB),
        in_specs=[
            pl.BlockSpec((1, BAND_QB, HEAD_DIM), lambda h, bi, qi: (bi, qi, qcol + h)),
            kv_spec(kcol, 2), kv_spec(kcol, 1), kv_spec(kcol, 0),
            kv_spec(vcol, 2), kv_spec(vcol, 1), kv_spec(vcol, 0),
            pl.BlockSpec((1, BAND_QB, BAND_KB * BAND_QB), lambda h, bi, qi: (h, 0, 0)),
        ],
        out_specs=pl.BlockSpec((1, BAND_QB, HEAD_DIM), lambda h, bi, qi: (bi, qi, h)),
        out_shape=jax.ShapeDtypeStruct((b, s, W_HEADS), BF16),
        compiler_params=_params("arbitrary", "arbitrary", "arbitrary"),
        name="band_prompt",
    )(proj3, proj3, proj3, proj3, proj3, proj3, proj3, bias)


def _band_step_kernel(q_ref, kn_ref, vn_ref, ck_ref, cv_ref, bias_ref, o_ref, *, lb):
    for h in range(N_HEADS):
        hs = slice(h * HEAD_DIM, (h + 1) * HEAD_DIM)
        q = q_ref[0, :, hs]
        rows = pl.ds(h, lb, stride=N_HEADS)
        s_cache = _dot_nt(q, ck_ref[0, rows, :].astype(BF16)) + bias_ref[h, :, :lb]
        s_new = _dot_nt(q, kn_ref[0, :, hs]) + bias_ref[h, :, lb:]
        out = _softmax_pv((s_cache, s_new), (cv_ref[0, rows, :].astype(BF16), vn_ref[0, :, hs]))
        o_ref[0, :, hs] = out.astype(BF16)


def _band_step(proj3, cache_k, cache_v, bias):
    b, t, _ = proj3.shape
    lb = cache_k.shape[1] // N_HEADS
    cache_spec = pl.BlockSpec((1, lb * N_HEADS, HEAD_DIM), lambda bi: (bi, 0, 0))
    return pl.pallas_call(
        functools.partial(_band_step_kernel, lb=lb),
        grid=(b,),
        in_specs=[
            pl.BlockSpec((1, t, W_HEADS), lambda bi: (bi, 0, 3)),
            pl.BlockSpec((1, t, W_HEADS), lambda bi: (bi, 0, 4)),
            pl.BlockSpec((1, t, W_HEADS), lambda bi: (bi, 0, 5)),
            cache_spec, cache_spec,
            pl.BlockSpec((N_HEADS, t, lb + t), lambda bi: (0, 0, 0)),
        ],
        out_specs=pl.BlockSpec((1, t, W_HEADS), lambda bi: (bi, 0, 0)),
        out_shape=jax.ShapeDtypeStruct((b, t, W_HEADS), BF16),
        compiler_params=_params("arbitrary"),
        name="band_step",
    )(proj3, proj3, proj3, cache_k, cache_v, bias)


def _merge_kernel(osb_ref, obd_ref, wsb_ref, wbd_ref, ga_ref, gb_ref, o_ref):
    a = _dot(osb_ref[...], wsb_ref[...])
    b = _dot(obd_ref[...], wbd_ref[...])
    merged = _sigmoid(ga_ref[...].astype(F32)) * a + _sigmoid(gb_ref[...].astype(F32)) * b
    o_ref[...] = merged.astype(BF16)


def _merge(o_sb, o_band, w_sb_bf, w_band_bf, proj, tm):
    m = o_sb.shape[0]
    d = w_sb_bf.shape[1]
    tn = min(512, d)
    ga_col = 6 * W_HEADS // tn
    gb_col = ga_col + d // tn
    return pl.pallas_call(
        _merge_kernel,
        grid=(m // tm, d // tn),
        in_specs=[
            pl.BlockSpec((tm, W_HEADS), lambda i, j: (i, 0)),
            pl.BlockSpec((tm, W_HEADS), lambda i, j: (i, 0)),
            pl.BlockSpec((W_HEADS, tn), lambda i, j: (0, j)),
            pl.BlockSpec((W_HEADS, tn), lambda i, j: (0, j)),
            pl.BlockSpec((tm, tn), lambda i, j: (i, ga_col + j)),
            pl.BlockSpec((tm, tn), lambda i, j: (i, gb_col + j)),
        ],
        out_specs=pl.BlockSpec((tm, tn), lambda i, j: (i, j)),
        out_shape=jax.ShapeDtypeStruct((m, d), BF16),
        compiler_params=_params("arbitrary", "arbitrary"),
        name="merge",
    )(o_sb, o_band, w_sb_bf, w_band_bf, proj, proj)


def _outproj_kernel(m_ref, w_ref, x_ref, gate_ref, o_ref, *, gb):
    acc = _dot(m_ref[...], w_ref[...])
    acc = acc.reshape(gb, ROW_GROUP, acc.shape[1])
    o_ref[...] = x_ref[...] + gate_ref[0] * acc


def _outproj(merged, w_out_bf, x3, mods, gb):
    ng, _, d = x3.shape
    tm = gb * ROW_GROUP
    tn = min(512, d)
    return pl.pallas_call(
        functools.partial(_outproj_kernel, gb=gb),
        grid=(ng // gb, d // tn),
        in_specs=[
            pl.BlockSpec((tm, d), lambda i, j: (i, 0)),
            pl.BlockSpec((d, tn), lambda i, j: (0, j)),
            pl.BlockSpec((gb, ROW_GROUP, tn), lambda i, j: (i, 0, j)),
            pl.BlockSpec((1, gb, 1, tn), lambda i, j: (MOD_GATE_M, i, 0, j)),
        ],
        out_specs=pl.BlockSpec((gb, ROW_GROUP, tn), lambda i, j: (i, 0, j)),
        out_shape=jax.ShapeDtypeStruct(x3.shape, F32),
        compiler_params=_params("arbitrary", "arbitrary"),
        name="outproj",
    )(merged, w_out_bf, x3, mods)


ROUTE_E1, ROUTE_E2, ROUTE_W1, ROUTE_W2 = 0, 1, 2, 3


def _router_kernel(x_ref, g_ref, sc_ref, sh_ref, whi_ref, wlo_ref, b_ref, h_ref, route_ref, lo_scr, *, gb):
    def body(s, carry):
        h = _modulated_norm(x_ref[s], g_ref[...], sc_ref[0, s], sh_ref[0, s])
        hi = h.astype(BF16)
        rows = pl.ds(pl.multiple_of(s * ROW_GROUP, ROW_GROUP), ROW_GROUP)
        h_ref[rows, :] = hi
        lo_scr[rows, :] = (h - hi.astype(F32)).astype(BF16)
        return carry
    lax.fori_loop(0, gb, body, 0)

    hi = h_ref[...]
    logits = _dot(hi, whi_ref[...]) + _dot(hi, wlo_ref[...]) + _dot(lo_scr[...], whi_ref[...]) + b_ref[...]
    lane = lax.broadcasted_iota(jnp.int32, logits.shape, 1)
    ninf = -jnp.inf

    lane_f = lane.astype(F32)

    def first_max(vals):
        mx = vals.max(axis=1, keepdims=True)
        idx = jnp.where(vals == mx, lane_f, float(V7X_LANES)).min(axis=1, keepdims=True)
        return mx, idx

    gl = jnp.where(lane < N_GROUPS, logits, ninf)
    gmax, gidx = first_max(gl)
    g_weight = 1.0 / jnp.exp(gl - gmax).sum(axis=1, keepdims=True)
    lo_lane = N_GROUPS + EXPERTS_PER_GROUP * gidx
    el = jnp.where(lane_f >= lo_lane, jnp.where(lane_f < lo_lane + EXPERTS_PER_GROUP, logits, ninf), ninf)
    m1, i1 = first_max(el)
    m2, i2 = first_max(jnp.where(lane_f == i1, ninf, el))
    e21 = jnp.exp(m2 - m1)
    p1 = 1.0 / (1.0 + e21)
    p2 = e21 / (1.0 + e21)
    route = jnp.where(lane == ROUTE_E1, i1 - N_GROUPS,
                      jnp.where(lane == ROUTE_E2, i2 - N_GROUPS,
                                jnp.where(lane == ROUTE_W1, g_weight * p1,
                                          jnp.where(lane == ROUTE_W2, g_weight * p2, 0.0))))
    route_ref[...] = route


def _router_into_kernel(x_ref, g_ref, sc_ref, sh_ref, whi_ref, wlo_ref, b_ref, h_prev, route_prev,
                        h_ref, route_ref, lo_scr, *, gb):
    del h_prev, route_prev
    _router_kernel(x_ref, g_ref, sc_ref, sh_ref, whi_ref, wlo_ref, b_ref, h_ref, route_ref, lo_scr, gb=gb)


def _router(x3, g, mods, w_hi, w_lo, bias, gb, first_row, h_rows, n_tokens, into=None):
    ng, _, d = x3.shape
    tm = gb * ROW_GROUP
    assert first_row % tm == 0
    row_block = first_row // tm

    def mod_spec(which):
        return pl.BlockSpec((1, gb, 1, d), lambda i: (which, i, 0, 0))

    w_spec = pl.BlockSpec((d, V7X_LANES), lambda i: (0, 0))
    in_specs = [
        pl.BlockSpec((gb, ROW_GROUP, d), lambda i: (i, 0, 0)),
        pl.BlockSpec((1, d), lambda i: (0, 0)),
        mod_spec(MOD_SCALE_F), mod_spec(MOD_SHIFT_F), w_spec, w_spec,
        pl.BlockSpec((1, V7X_LANES), lambda i: (0, 0)),
    ]
    args = (x3, g, mods, mods, w_hi, w_lo, bias)
    body, aliases = _router_kernel, {}
    if into is not None:
        body = _router_into_kernel
        aliases = {len(args): 0, len(args) + 1: 1}
        in_specs += [pl.BlockSpec(memory_space=pl.ANY), pl.BlockSpec(memory_space=pl.ANY)]
        args += tuple(into)
    return pl.pallas_call(
        functools.partial(body, gb=gb),
        grid=(ng // gb,),
        in_specs=in_specs,
        out_specs=[
            pl.BlockSpec((tm, d), lambda i: (row_block + i, 0)),
            pl.BlockSpec((tm, V7X_LANES), lambda i: (row_block + i, 0)),
        ],
        out_shape=[jax.ShapeDtypeStruct((h_rows, d), BF16), jax.ShapeDtypeStruct((n_tokens, V7X_LANES), F32)],
        input_output_aliases=aliases,
        scratch_shapes=[pltpu.VMEM((tm, d), BF16)],
        compiler_params=_params("arbitrary"),
        name="router",
    )(*args)


def _moe_kernel(te_ref, nv_ref, x_ref, wg_ref, wu_ref, wd_ref, o_ref, wg_scr, wu_scr, wd_scr):
    t = pl.program_id(0)
    changed = te_ref[t] != te_ref[jnp.maximum(t - 1, 0)]

    @pl.when((t == 0) | changed)
    def _():
        wg_scr[...] = wg_ref[0].astype(BF16)
        wu_scr[...] = wu_ref[0].astype(BF16)
        wd_scr[...] = wd_ref[0].astype(BF16)

    @pl.when(t < nv_ref[0])
    def _():
        x = x_ref[...]
        g = _dot(x, wg_scr[...])
        u = _dot(x, wu_scr[...])
        hidden = (g * _sigmoid(g)) * u
        o_ref[...] = _dot(hidden.astype(BF16), wd_scr[...]).astype(BF16)

    @pl.when(t >= nv_ref[0])
    def _():
        o_ref[...] = jnp.zeros(o_ref.shape, BF16)


def _moe(tile_expert, n_valid, xs, w_gate, w_up, w_down):
    rows, d = xs.shape
    f = w_gate.shape[2]
    n_tiles = rows // MOE_TILE

    def x_map(t, te, nv):
        return (jnp.minimum(t, nv[0] - 1), 0)

    grid_spec = pltpu.PrefetchScalarGridSpec(
        num_scalar_prefetch=2,
        grid=(n_tiles,),
        in_specs=[
            pl.BlockSpec((MOE_TILE, d), x_map),
            pl.BlockSpec((1, d, f), lambda t, te, nv: (te[t], 0, 0)),
            pl.BlockSpec((1, d, f), lambda t, te, nv: (te[t], 0, 0)),
            pl.BlockSpec((1, f, d), lambda t, te, nv: (te[t], 0, 0)),
        ],
        out_specs=pl.BlockSpec((MOE_TILE, d), lambda t, te, nv: (t, 0)),
        scratch_shapes=[pltpu.VMEM((d, f), BF16), pltpu.VMEM((d, f), BF16), pltpu.VMEM((f, d), BF16)],
    )
    return pl.pallas_call(
        _moe_kernel,
        grid_spec=grid_spec,
        out_shape=jax.ShapeDtypeStruct((rows, d), BF16),
        compiler_params=_params("arbitrary"),
        name="moe",
    )(tile_expert, n_valid, xs, w_gate, w_up, w_down)


def _dispatch(route):
    n = route.shape[0]
    e = jnp.concatenate([route[:, ROUTE_E1], route[:, ROUTE_E2]]).astype(jnp.int32)
    tok = jnp.concatenate([jnp.arange(n, dtype=jnp.int32)] * 2)
    onehot = (e[:, None] == jnp.arange(N_EXPERTS, dtype=jnp.int32)[None, :]).astype(jnp.int32)
    before = jnp.cumsum(onehot, axis=0) - onehot
    rank = jnp.sum(before * onehot, axis=1)
    counts = jnp.sum(onehot, axis=0)
    padded = ((counts + MOE_TILE - 1) // MOE_TILE) * MOE_TILE
    ends = jnp.cumsum(padded)
    pos = (ends - padded)[e] + rank
    n_tiles = -(-(2 * n + N_EXPERTS * (MOE_TILE - 1)) // (4 * MOE_TILE)) * 4
    tile_start = jnp.arange(n_tiles, dtype=jnp.int32) * MOE_TILE
    n_valid = (ends[-1] // MOE_TILE).astype(jnp.int32)
    tile_expert = jnp.sum((ends[None, :] <= tile_start[:, None]).astype(jnp.int32), axis=1)
    last_expert = tile_expert[jnp.maximum(n_valid - 1, 0)]
    tile_expert = jnp.where(tile_start < ends[-1], tile_expert, last_expert)
    src_tok = jnp.zeros((n_tiles * MOE_TILE,), jnp.int32).at[pos].set(tok)
    return pos[:n], pos[n:], src_tok, tile_expert, n_valid.reshape(1)


def _final_kernel(x_ref, y1_ref, y2_ref, route_ref, gate_ref, o_ref):
    r = route_ref[...]
    w1 = r[:, :, ROUTE_W1:ROUTE_W1 + 1]
    w2 = r[:, :, ROUTE_W2:ROUTE_W2 + 1]
    y = w1 * y1_ref[...].astype(F32) + w2 * y2_ref[...].astype(F32)
    o_ref[...] = x_ref[...] + gate_ref[0] * y


def _final(x3, y1, y2, route3, first_group, mods, gb):
    ng, _, d = x3.shape
    tn = min(1024, d)
    assert first_group % gb == 0
    g0 = first_group // gb
    blk = pl.BlockSpec((gb, ROW_GROUP, tn), lambda i, j: (i, 0, j))
    return pl.pallas_call(
        _final_kernel,
        grid=(ng // gb, d // tn),
        in_specs=[
            blk, blk, blk,
            pl.BlockSpec((gb, ROW_GROUP, V7X_LANES), lambda i, j: (g0 + i, 0, 0)),
            pl.BlockSpec((1, gb, 1, tn), lambda i, j: (MOD_GATE_F, i, 0, j)),
        ],
        out_specs=blk,
        out_shape=jax.ShapeDtypeStruct(x3.shape, F32),
        compiler_params=_params("arbitrary", "arbitrary"),
        name="final",
    )(x3, y1, y2, route3, mods)


def _layer(xp, xs, csk, csv, cbk, cbv, c_prompt, c_sample, norm_mix, norm_ffn, w_ada, b_ada, w_in, q_norm,
           k_norm, rel_table, w_proj_sb, w_proj_band, w_out, w_rg, b_rg, w_re, b_re, w_gate, w_up, w_down):
    bp, sp_len, d = xp.shape
    bs, ts, _ = xs.shape
    d_in = w_in.shape[1]
    assert sp_len % BAND_QB == 0 and ts == ROW_GROUP and sp_len % ROW_GROUP == 0

    n_c = bp + bs
    c_pad = jnp.concatenate([c_prompt, c_sample, jnp.zeros((-n_c % 8, d), F32)], axis=0)
    mod = _ada(c_pad, w_ada, b_ada)
    gp = sp_len // ROW_GROUP
    mod_p = jnp.repeat(mod[:, :bp], gp, axis=1)[:, :, None, :]
    mod_s = mod[:, bp:n_c][:, :, None, :]

    w_in_bf = w_in.astype(BF16)
    w_sb_bf = w_proj_sb.astype(BF16)
    w_band_bf = w_proj_band.astype(BF16)
    w_out_bf = w_out.astype(BF16)
    qn = q_norm.reshape(1, W_HEADS)
    kn = k_norm.reshape(1, W_HEADS)
    g_mix = norm_mix.reshape(1, d)
    g_ffn = norm_ffn.reshape(1, d)
    lb = cbk.shape[1]
    assert lb % CHUNK == 0 and ts == CHUNK
    bias_blocks = _bias_blocks(rel_table)
    bias = _band_bias(bias_blocks, rel_table)
    bias_step = _band_step_bias(bias_blocks, rel_table, lb)
    u = _strict_lower(SB_T)

    w_router = jnp.zeros((d, V7X_LANES), F32).at[:, :N_GROUPS].set(w_rg)
    w_router = w_router.at[:, N_GROUPS:N_GROUPS + N_EXPERTS].set(w_re)
    w_router_hi = w_router.astype(BF16)
    w_router_lo = (w_router - w_router_hi.astype(F32)).astype(BF16)
    b_router = jnp.zeros((1, V7X_LANES), F32).at[0, :N_GROUPS].set(b_rg)
    b_router = b_router.at[0, N_GROUPS:N_GROUPS + N_EXPERTS].set(b_re.reshape(-1))

    xp3 = xp.reshape(bp * gp, ROW_GROUP, d)
    xs3 = xs.reshape(bs, ROW_GROUP, d)
    gb_p = min(8, bp * gp)
    gb_s = min(8, bs)

    def mixer(x3, mods, gb, attend):
        proj, ka, va, kb, vb = _proj(x3, g_mix, mods, w_in_bf, qn, kn, gb)
        o_sb, o_band = attend(proj)
        merged = _merge(o_sb, o_band, w_sb_bf, w_band_bf, proj, gb * ROW_GROUP)
        x1 = _outproj(merged, w_out_bf, x3, mods, gb)
        return x1, ka, va, kb, vb

    def attend_prompt(proj):
        p3 = proj.reshape(bp, sp_len, d_in)
        return (_sb_prompt(p3, u).reshape(bp * sp_len, W_HEADS),
                _band_prompt(p3, bias).reshape(bp * sp_len, W_HEADS))

    def attend_sample(proj):
        p3 = proj.reshape(bs, ts, d_in)
        def rows(cache):
            return cache.reshape(bs, -1, HEAD_DIM)

        o_sb = _sb_step(p3, rows(csk), rows(csv), u)
        o_band = _band_step(p3, rows(cbk), rows(cbv), bias_step)
        return o_sb.reshape(bs * ts, W_HEADS), o_band.reshape(bs * ts, W_HEADS)

    x1p, kap, vap, kbp, vbp = mixer(xp3, mod_p, gb_p, attend_prompt)
    x1s, kas, vas, kbs, vbs = mixer(xs3, mod_s, gb_s, attend_sample)

    n_p, n_tok = bp * sp_len, bp * sp_len + bs * ts
    h_rows = 2 * n_tok
    router_w = (w_router_hi, w_router_lo, b_router)
    bufs = _router(x1p, g_ffn, mod_p, *router_w, gb_p, 0, h_rows, n_tok)
    h2, route = _router(x1s, g_ffn, mod_s, *router_w, gb_s, n_p, h_rows, n_tok, into=bufs)
    pos1, pos2, src_tok, tile_expert, n_valid = _dispatch(route)

    def take_rows(a, idx):
        return a.at[idx].get(mode="promise_in_bounds")

    y_sorted = _moe(tile_expert, n_valid, take_rows(h2, src_tok), w_gate, w_up, w_down)
    route3 = route.reshape(n_tok // ROW_GROUP, ROW_GROUP, V7X_LANES)

    def finish(x1, pos1, pos2, first_group, mods, gb):
        ng = x1.shape[0]
        return _final(x1, take_rows(y_sorted, pos1).reshape(ng, ROW_GROUP, d),
                      take_rows(y_sorted, pos2).reshape(ng, ROW_GROUP, d), route3, first_group, mods, gb)

    yp = finish(x1p, pos1[:n_p], pos2[:n_p], 0, mod_p, gb_p).reshape(bp, sp_len, d)
    ys = finish(x1s, pos1[n_p:], pos2[n_p:], n_p // ROW_GROUP, mod_s, gb_s).reshape(bs, ts, d)

    def heads(a, b):
        return a.reshape(b, -1, N_HEADS, HEAD_DIM)

    n_band = min(BAND_PAST, sp_len)
    return (yp, ys, heads(kap, bp), heads(vap, bp), heads(kbp, bp)[:, -n_band:], heads(vbp, bp)[:, -n_band:],
            heads(kas, bs), heads(vas, bs), heads(kbs, bs), heads(vbs, bs))


def kernel(x_prompt, x_sample, cache_sb_k, cache_sb_v, cache_band_k, cache_band_v, c_prompt, c_sample, norm_mix, norm_ffn, w_ada, b_ada, w_in, q_norm_band, k_norm_band, rel_bias_band, w_proj_sb, w_proj_band, w_out, w_router_group, b_router_group, w_router_expert, b_router_expert, w_gate, w_up, w_down):
    depth = w_in.shape[0]
    xp, xs = x_prompt, x_sample
    outs = [[] for _ in range(8)]
    for l in range(depth):
        res = _layer(xp, xs, cache_sb_k[l], cache_sb_v[l], cache_band_k[l], cache_band_v[l], c_prompt, c_sample,
                     norm_mix[l], norm_ffn[l], w_ada[l], b_ada[l], w_in[l], q_norm_band[l], k_norm_band[l],
                     rel_bias_band[l], w_proj_sb[l], w_proj_band[l], w_out[l], w_router_group[l],
                     b_router_group[l], w_router_expert[l], b_router_expert[l], w_gate[l], w_up[l], w_down[l])
        xp, xs = res[0], res[1]
        for acc, r in zip(outs, res[2:]):
            acc.append(r)
    return (xp, xs) + tuple(jnp.stack(o, axis=0) for o in outs)
```

```python
import functools

import jax
import jax.numpy as jnp
from jax import lax
from jax.experimental import pallas as pl
from jax.experimental.pallas import tpu as pltpu

F32 = jnp.float32
BF16 = jnp.bfloat16

EPS = 1e-6
HEAD_DIM = 128
N_HEADS = 8
W_HEADS = N_HEADS * HEAD_DIM
CHUNK = 64
BAND_LEFT_CHUNKS = 8
BAND_PAST = BAND_LEFT_CHUNKS * CHUNK
REL_CLIP = 128
N_GROUPS = 4
EXPERTS_PER_GROUP = 8
N_EXPERTS = N_GROUPS * EXPERTS_PER_GROUP
QK_SCALE = HEAD_DIM ** -0.5
NEG_BIG = -1e30

V7X_LANES = 128
V7X_VMEM_LIMIT = 56 * 1024 * 1024
ROW_GROUP = 64
MOE_TILE = 256
MATMUL_GROUPS = 16
MATMUL_TN = 1024
ROWWISE_GROUPS = 8


def _params(*sem):
    return pltpu.CompilerParams(dimension_semantics=sem, vmem_limit_bytes=V7X_VMEM_LIMIT)


def _sigmoid(x):
    return 1.0 / (1.0 + jnp.exp(-x))


def _dot(a, b):
    return jnp.dot(a, b, preferred_element_type=F32)


def _dot_nt(a, b):
    return lax.dot_general(a, b, (((1,), (1,)), ((), ())), preferred_element_type=F32)


def _ada_kernel(c_ref, w_ref, b_ref, o_ref):
    c = c_ref[...]
    a = (c * _sigmoid(c)).astype(BF16)
    o_ref[0] = _dot(a, w_ref[...].astype(BF16)) + b_ref[...]


def _ada(c_pad, w_ada, b_ada):
    r, d = c_pad.shape
    tn = min(1024, d)
    per = d // tn
    return pl.pallas_call(
        _ada_kernel,
        grid=(6 * per,),
        in_specs=[
            pl.BlockSpec((r, d), lambda j: (0, 0)),
            pl.BlockSpec((d, tn), lambda j: (0, j)),
            pl.BlockSpec((1, tn), lambda j: (0, j)),
        ],
        out_specs=pl.BlockSpec((1, r, tn), lambda j: (j // per, 0, j % per)),
        out_shape=jax.ShapeDtypeStruct((6, r, d), F32),
        compiler_params=_params("arbitrary"),
        name="ada",
    )(c_pad, w_ada, b_ada.reshape(1, 6 * d))


def _modulated_norm(x, g, scale, shift):
    ms = jnp.mean(x * x, axis=-1, keepdims=True)
    return (x * lax.rsqrt(ms + EPS) * g) * (1.0 + scale) + shift


def _head_norm(a, gain_ref):
    outs = []
    for hh in range(a.shape[1] // HEAD_DIM):
        blk = a[:, hh * HEAD_DIM:(hh + 1) * HEAD_DIM]
        ms = jnp.mean(blk * blk, axis=-1, keepdims=True)
        outs.append(blk * lax.rsqrt(ms + EPS) * gain_ref[:, hh * HEAD_DIM:(hh + 1) * HEAD_DIM])
    return jnp.concatenate(outs, axis=1)


def _proj_kernel(x_ref, g_ref, sc_ref, sh_ref, w_ref, qn_ref, kn_ref,
                 proj_ref, ka_ref, va_ref, kb_ref, vb_ref, h_scr, *, gb, per):
    j = pl.program_id(1)

    @pl.when(j == 0)
    def _():
        def body(s, carry):
            h = _modulated_norm(x_ref[s], g_ref[...], sc_ref[0, s], sh_ref[0, s])
            h_scr[pl.ds(pl.multiple_of(s * ROW_GROUP, ROW_GROUP), ROW_GROUP), :] = h.astype(BF16)
            return carry
        lax.fori_loop(0, gb, body, 0)

    acc = _dot(h_scr[...], w_ref[...])
    sec = j // per

    @pl.when(sec == 0)
    def _():
        proj_ref[...] = (acc * QK_SCALE).astype(BF16)

    @pl.when(sec == 1)
    def _():
        proj_ref[...] = acc.astype(BF16)
        ka_ref[...] = acc

    @pl.when(sec == 2)
    def _():
        proj_ref[...] = acc.astype(BF16)
        va_ref[...] = acc

    @pl.when(sec == 3)
    def _():
        proj_ref[...] = (_head_norm(acc, qn_ref) * QK_SCALE).astype(BF16)

    @pl.when(sec == 4)
    def _():
        n = _head_norm(acc, kn_ref)
        proj_ref[...] = n.astype(BF16)
        kb_ref[...] = n

    @pl.when(sec == 5)
    def _():
        proj_ref[...] = acc.astype(BF16)
        vb_ref[...] = acc

    @pl.when(sec >= 6)
    def _():
        proj_ref[...] = acc.astype(BF16)


MOD_SHIFT_M, MOD_SCALE_M, MOD_GATE_M, MOD_SHIFT_F, MOD_SCALE_F, MOD_GATE_F = range(6)


def _proj(x3, g, mods, w_in_bf, qn, kn, gb):
    ng, _, d = x3.shape
    d_in = w_in_bf.shape[1]
    m = ng * ROW_GROUP
    tm = gb * ROW_GROUP
    tn = 512
    per = W_HEADS // tn

    def sect(s):
        return lambda i, j: (i, jnp.clip(j - s * per, 0, per - 1))

    def mod_spec(which):
        return pl.BlockSpec((1, gb, 1, d), lambda i, j: (which, i, 0, 0))

    kv_shape = jax.ShapeDtypeStruct((m, W_HEADS), F32)
    return pl.pallas_call(
        functools.partial(_proj_kernel, gb=gb, per=per),
        grid=(ng // gb, d_in // tn),
        in_specs=[
            pl.BlockSpec((gb, ROW_GROUP, d), lambda i, j: (i, 0, 0)),
            pl.BlockSpec((1, d), lambda i, j: (0, 0)),
            mod_spec(MOD_SCALE_M), mod_spec(MOD_SHIFT_M),
            pl.BlockSpec((d, tn), lambda i, j: (0, j)),
            pl.BlockSpec((1, tn), lambda i, j: (0, jnp.clip(j - 3 * per, 0, per - 1))),
            pl.BlockSpec((1, tn), lambda i, j: (0, jnp.clip(j - 4 * per, 0, per - 1))),
        ],
        out_specs=[
            pl.BlockSpec((tm, tn), lambda i, j: (i, j)),
            pl.BlockSpec((tm, tn), sect(1)),
            pl.BlockSpec((tm, tn), sect(2)),
            pl.BlockSpec((tm, tn), sect(4)),
            pl.BlockSpec((tm, tn), sect(5)),
        ],
        out_shape=[jax.ShapeDtypeStruct((m, d_in), BF16), kv_shape, kv_shape, kv_shape, kv_shape],
        scratch_shapes=[pltpu.VMEM((tm, d), BF16)],
        compiler_params=_params("arbitrary", "arbitrary"),
        name="proj",
    )(x3, g, mods, mods, w_in_bf, qn, kn)


def _strict_lower_twice(n):
    row = lax.broadcasted_iota(jnp.int32, (2 * n, n), 0)
    col = lax.broadcasted_iota(jnp.int32, (2 * n, n), 1)
    row = jnp.where(row >= n, row - n, row)
    return jnp.where(row > col, 1.0, 0.0).astype(BF16)


def _sb_heads(load_qkv, n_heads, u, carries, diagonal):
    qkv = [load_qkv(h) for h in range(n_heads)]
    zs = [_dot_nt(q, k) for q, k, _ in qkv]
    if diagonal:
        row = lax.broadcasted_iota(jnp.int32, zs[0].shape, 0)
        col = lax.broadcasted_iota(jnp.int32, zs[0].shape, 1)
        valid = col < row
    sps, tails = [], []
    for z in zs:
        sp = jnp.maximum(z, 0.0) + jnp.log(1.0 + jnp.exp(-jnp.abs(z)))
        if diagonal:
            sp = jnp.where(valid, sp, 0.0)
        hi = sp.astype(BF16)
        lo = (sp - hi.astype(F32)).astype(BF16)
        sps.append(sp)
        tails.append(_dot(jnp.concatenate([hi, lo], axis=1), u))
    pvs, new = [], []
    for h, (z, sp, tail) in enumerate(zip(zs, sps, tails)):
        tq = z.shape[0]
        carry = jnp.zeros((tq, 1), F32) if carries is None else carries[:, h * HEAD_DIM:h * HEAD_DIM + 1]
        loga = z - sp - tail - carry
        if diagonal:
            loga = jnp.where(valid, loga, NEG_BIG)
        pvs.append(_dot(jnp.exp(loga).astype(BF16), qkv[h][2]))
        new.append(jnp.broadcast_to(carry + tail[:, 0:1] + sp[:, 0:1], (tq, HEAD_DIM)))
    return jnp.concatenate(pvs, axis=1), jnp.concatenate(new, axis=1)


SB_T = 256
SB_HEADS_PER_STEP = 8


def _head_slice(h):
    return slice(h * HEAD_DIM, (h + 1) * HEAD_DIM)


def _sb_prompt_kernel(q_ref, k_ref, v_ref, u_ref, o_ref, acc_scr, carry_scr):
    qi = pl.program_id(2)
    u = u_ref[...]

    def block(r, carries, diagonal):
        def load(h):
            hs = _head_slice(h)
            return q_ref[0, :, hs], k_ref[0, pl.ds(r, SB_T), hs], v_ref[0, pl.ds(r, SB_T), hs]
        return _sb_heads(load, SB_HEADS_PER_STEP, u, carries, diagonal)

    acc_scr[...], carry_scr[...] = block(pl.multiple_of(qi * SB_T, SB_T), None, True)

    def body(step, c):
        pv, carries = block(pl.multiple_of((qi - 1 - step) * SB_T, SB_T), carry_scr[...], False)
        acc_scr[...] += pv
        carry_scr[...] = carries
        return c

    lax.fori_loop(0, qi, body, 0)
    o_ref[0] = acc_scr[...].astype(BF16)


def _sb_prompt(proj3, u):
    b, s, _ = proj3.shape
    hp = SB_HEADS_PER_STEP
    w = hp * HEAD_DIM
    kcol, vcol = W_HEADS // w, 2 * W_HEADS // w
    return pl.pallas_call(
        _sb_prompt_kernel,
        grid=(b, N_HEADS // hp, s // SB_T),
        in_specs=[
            pl.BlockSpec((1, SB_T, w), lambda bi, h, qi: (bi, qi, h)),
            pl.BlockSpec((1, s, w), lambda bi, h, qi: (bi, 0, kcol + h)),
            pl.BlockSpec((1, s, w), lambda bi, h, qi: (bi, 0, vcol + h)),
            pl.BlockSpec((2 * SB_T, SB_T), lambda bi, h, qi: (0, 0)),
        ],
        out_specs=pl.BlockSpec((1, SB_T, w), lambda bi, h, qi: (bi, qi, h)),
        out_shape=jax.ShapeDtypeStruct((b, s, W_HEADS), BF16),
        scratch_shapes=[pltpu.VMEM((SB_T, w), F32), pltpu.VMEM((SB_T, w), F32)],
        compiler_params=_params("arbitrary", "arbitrary", "arbitrary"),
        name="sb_prompt",
    )(proj3, proj3, proj3, u)


def _sb_step_kernel(q_ref, kn_ref, vn_ref, ck_ref, cv_ref, u_ref, o_ref, acc_scr, carry_scr, *, tkb, t):
    j = pl.program_id(1)
    u = u_ref[...]

    @pl.when(j == 0)
    def _():
        def load(h):
            hs = _head_slice(h)
            return q_ref[0, :, hs], kn_ref[0, :, hs], vn_ref[0, :, hs]
        acc_scr[...], carry_scr[...] = _sb_heads(load, N_HEADS, _strict_lower_twice(t), None, True)

    def body(step, c):
        r = pl.multiple_of((tkb - (step + 1) * SB_T) * N_HEADS, SB_T * N_HEADS)

        def load(h):
            rows = pl.ds(r + h, SB_T, stride=N_HEADS)
            return q_ref[0, :, _head_slice(h)], ck_ref[0, rows, :].astype(BF16), cv_ref[0, rows, :].astype(BF16)

        pv, carries = _sb_heads(load, N_HEADS, u, carry_scr[...], False)
        acc_scr[...] += pv
        carry_scr[...] = carries
        return c

    lax.fori_loop(0, tkb // SB_T, body, 0)

    @pl.when(j == pl.num_programs(1) - 1)
    def _():
        o_ref[0] = acc_scr[...].astype(BF16)


def _sb_step(proj3, cache_k, cache_v, u):
    b, t, _ = proj3.shape
    p = cache_k.shape[1] // N_HEADS
    tkb = min(1024, p)
    nj = p // tkb
    cache_spec = pl.BlockSpec((1, tkb * N_HEADS, HEAD_DIM), lambda bi, j: (bi, nj - 1 - j, 0))
    return pl.pallas_call(
        functools.partial(_sb_step_kernel, tkb=tkb, t=t),
        grid=(b, nj),
        in_specs=[
            pl.BlockSpec((1, t, W_HEADS), lambda bi, j: (bi, 0, 0)),
            pl.BlockSpec((1, t, W_HEADS), lambda bi, j: (bi, 0, 1)),
            pl.BlockSpec((1, t, W_HEADS), lambda bi, j: (bi, 0, 2)),
            cache_spec, cache_spec,
            pl.BlockSpec((2 * SB_T, SB_T), lambda bi, j: (0, 0)),
        ],
        out_specs=pl.BlockSpec((1, t, W_HEADS), lambda bi, j: (bi, 0, 0)),
        out_shape=jax.ShapeDtypeStruct((b, t, W_HEADS), BF16),
        scratch_shapes=[pltpu.VMEM((t, W_HEADS), F32), pltpu.VMEM((t, W_HEADS), F32)],
        compiler_params=_params("arbitrary", "arbitrary"),
        name="sb_step",
    )(proj3, proj3, proj3, cache_k, cache_v, u)


BAND_QB = 4 * CHUNK
BAND_KB = 3


BIAS_NEAR = REL_CLIP // CHUNK + 1


def _bias_blocks_kernel(table_ref, o_ref):
    h = pl.program_id(0)
    i = lax.broadcasted_iota(jnp.int32, (CHUNK, CHUNK), 0)
    j = lax.broadcasted_iota(jnp.int32, (CHUNK, CHUNK), 1)
    for d in range(BIAS_NEAR):
        idx = jnp.clip(i - j + d * CHUNK, -REL_CLIP, REL_CLIP) + REL_CLIP

        def body(r, acc, idx=idx):
            return jnp.where(idx == r, table_ref[h, r], acc)

        lo = max(d * CHUNK - (CHUNK - 1), -REL_CLIP) + REL_CLIP
        hi = min(d * CHUNK + (CHUNK - 1), REL_CLIP) + REL_CLIP
        o_ref[0, d] = lax.fori_loop(lo, hi + 1, body, jnp.zeros((CHUNK, CHUNK), F32))


def _bias_blocks(table):
    nh = table.shape[0]
    return pl.pallas_call(
        _bias_blocks_kernel,
        grid=(nh,),
        in_specs=[pl.BlockSpec(memory_space=pltpu.SMEM)],
        out_specs=pl.BlockSpec((1, BIAS_NEAR, CHUNK, CHUNK), lambda h: (h, 0, 0, 0)),
        out_shape=jax.ShapeDtypeStruct((nh, BIAS_NEAR, CHUNK, CHUNK), F32),
        compiler_params=_params("arbitrary"),
        name="bias_blocks",
    )(table.astype(F32))


def _chunk_bias(blocks, far, dist):
    if dist < 0 or dist > BAND_LEFT_CHUNKS:
        return jnp.full(far.shape, NEG_BIG, F32)
    return blocks[:, dist] if dist < BIAS_NEAR else far


def _band_bias(blocks, table):
    far = jnp.broadcast_to(table[:, -1].astype(F32)[:, None, None], (table.shape[0], CHUNK, CHUNK))
    qc, kc = BAND_QB // CHUNK, BAND_KB * BAND_QB // CHUNK
    rows = [jnp.concatenate([_chunk_bias(blocks, far, r - c + BAND_LEFT_CHUNKS) for c in range(kc)], axis=2)
            for r in range(qc)]
    return jnp.concatenate(rows, axis=1)


def _band_step_bias(blocks, table, lb):
    far = jnp.broadcast_to(table[:, -1].astype(F32)[:, None, None], (table.shape[0], CHUNK, CHUNK))
    nc = lb // CHUNK
    return jnp.concatenate([blocks[:, nc - c] if nc - c < BIAS_NEAR else far for c in range(nc + 1)], axis=2)


def _softmax_pv(scores, values):
    m = scores[0].max(axis=1, keepdims=True)
    for s in scores[1:]:
        m = jnp.maximum(m, s.max(axis=1, keepdims=True))
    ps = [jnp.exp(s - m) for s in scores]
    den = ps[0].sum(axis=1, keepdims=True)
    for p in ps[1:]:
        den = den + p.sum(axis=1, keepdims=True)
    num = _dot(ps[0].astype(BF16), values[0])
    for p, v in zip(ps[1:], values[1:]):
        num = num + _dot(p.astype(BF16), v)
    return num / den


def _band_prompt_kernel(q_ref, k0_ref, k1_ref, k2_ref, v0_ref, v1_ref, v2_ref, bias_ref, o_ref):
    qi = pl.program_id(2)
    q = q_ref[0]
    scores = []
    for c, k_ref in enumerate((k0_ref, k1_ref, k2_ref)):
        s = _dot_nt(q, k_ref[0]) + bias_ref[0, :, c * BAND_QB:(c + 1) * BAND_QB]
        if c < BAND_KB - 1:
            s = jnp.where(qi + c >= BAND_KB - 1, s, NEG_BIG)
        scores.append(s)
    o_ref[0] = _softmax_pv(scores, (v0_ref[0], v1_ref[0], v2_ref[0])).astype(BF16)


def _band_prompt(proj3, bias):
    b, s, _ = proj3.shape
    qcol, kcol, vcol = 3 * N_HEADS, 4 * N_HEADS, 5 * N_HEADS

    def kv_spec(col, back):
        return pl.BlockSpec((1, BAND_QB, HEAD_DIM),
                            lambda h, bi, qi: (bi, jnp.maximum(qi - back, 0), col + h))

    return pl.pallas_call(
        _band_prompt_kernel,
        grid=(N_HEADS, b, s // BAND_QB),
        in_specs=[
            pl.BlockSpec((1, BAND_QB, HEAD_DIM), lambda h, bi, qi: (bi, qi, qcol + h)),
            kv_spec(kcol, 2), kv_spec(kcol, 1), kv_spec(kcol, 0),
            kv_spec(vcol, 2), kv_spec(vcol, 1), kv_spec(vcol, 0),
            pl.BlockSpec((1, BAND_QB, BAND_KB * BAND_QB), lambda h, bi, qi: (h, 0, 0)),
        ],
        out_specs=pl.BlockSpec((1, BAND_QB, HEAD_DIM), lambda h, bi, qi: (bi, qi, h)),
        out_shape=jax.ShapeDtypeStruct((b, s, W_HEADS), BF16),
        compiler_params=_params("arbitrary", "arbitrary", "arbitrary"),
        name="band_prompt",
    )(proj3, proj3, proj3, proj3, proj3, proj3, proj3, bias)


def _band_step_kernel(q_ref, kn_ref, vn_ref, ck_ref, cv_ref, bias_ref, o_ref, *, lb):
    for h in range(N_HEADS):
        hs = slice(h * HEAD_DIM, (h + 1) * HEAD_DIM)
        q = q_ref[0, :, hs]
        rows = pl.ds(h, lb, stride=N_HEADS)
        s_cache = _dot_nt(q, ck_ref[0, rows, :].astype(BF16)) + bias_ref[h, :, :lb]
        s_new = _dot_nt(q, kn_ref[0, :, hs]) + bias_ref[h, :, lb:]
        out = _softmax_pv((s_cache, s_new), (cv_ref[0, rows, :].astype(BF16), vn_ref[0, :, hs]))
        o_ref[0, :, hs] = out.astype(BF16)


def _band_step(proj3, cache_k, cache_v, bias):
    b, t, _ = proj3.shape
    lb = cache_k.shape[1] // N_HEADS
    cache_spec = pl.BlockSpec((1, lb * N_HEADS, HEAD_DIM), lambda bi: (bi, 0, 0))
    return pl.pallas_call(
        functools.partial(_band_step_kernel, lb=lb),
        grid=(b,),
        in_specs=[
            pl.BlockSpec((1, t, W_HEADS), lambda bi: (bi, 0, 3)),
            pl.BlockSpec((1, t, W_HEADS), lambda bi: (bi, 0, 4)),
            pl.BlockSpec((1, t, W_HEADS), lambda bi: (bi, 0, 5)),
            cache_spec, cache_spec,
            pl.BlockSpec((N_HEADS, t, lb + t), lambda bi: (0, 0, 0)),
        ],
        out_specs=pl.BlockSpec((1, t, W_HEADS), lambda bi: (bi, 0, 0)),
        out_shape=jax.ShapeDtypeStruct((b, t, W_HEADS), BF16),
        compiler_params=_params("arbitrary"),
        name="band_step",
    )(proj3, proj3, proj3, cache_k, cache_v, bias)


def _merge_kernel(osb_ref, obd_ref, wsb_ref, wbd_ref, ga_ref, gb_ref, o_ref):
    a = _dot(osb_ref[...], wsb_ref[...])
    b = _dot(obd_ref[...], wbd_ref[...])
    merged = _sigmoid(ga_ref[...].astype(F32)) * a + _sigmoid(gb_ref[...].astype(F32)) * b
    o_ref[...] = merged.astype(BF16)


def _merge(o_sb, o_band, w_sb_bf, w_band_bf, proj, tm):
    m = o_sb.shape[0]
    d = w_sb_bf.shape[1]
    tn = min(MATMUL_TN, d)
    ga_col = 6 * W_HEADS // tn
    gb_col = ga_col + d // tn
    return pl.pallas_call(
        _merge_kernel,
        grid=(m // tm, d // tn),
        in_specs=[
            pl.BlockSpec((tm, W_HEADS), lambda i, j: (i, 0)),
            pl.BlockSpec((tm, W_HEADS), lambda i, j: (i, 0)),
            pl.BlockSpec((W_HEADS, tn), lambda i, j: (0, j)),
            pl.BlockSpec((W_HEADS, tn), lambda i, j: (0, j)),
            pl.BlockSpec((tm, tn), lambda i, j: (i, ga_col + j)),
            pl.BlockSpec((tm, tn), lambda i, j: (i, gb_col + j)),
        ],
        out_specs=pl.BlockSpec((tm, tn), lambda i, j: (i, j)),
        out_shape=jax.ShapeDtypeStruct((m, d), BF16),
        compiler_params=_params("arbitrary", "arbitrary"),
        name="merge",
    )(o_sb, o_band, w_sb_bf, w_band_bf, proj, proj)


def _outproj_kernel(m_ref, w_ref, x_ref, gate_ref, o_ref, *, gb):
    acc = _dot(m_ref[...], w_ref[...])
    acc = acc.reshape(gb, ROW_GROUP, acc.shape[1])
    o_ref[...] = x_ref[...] + gate_ref[0] * acc


def _outproj(merged, w_out_bf, x3, mods, gb):
    ng, _, d = x3.shape
    tm = gb * ROW_GROUP
    tn = min(MATMUL_TN, d)
    return pl.pallas_call(
        functools.partial(_outproj_kernel, gb=gb),
        grid=(ng // gb, d // tn),
        in_specs=[
            pl.BlockSpec((tm, d), lambda i, j: (i, 0)),
            pl.BlockSpec((d, tn), lambda i, j: (0, j)),
            pl.BlockSpec((gb, ROW_GROUP, tn), lambda i, j: (i, 0, j)),
            pl.BlockSpec((1, gb, 1, tn), lambda i, j: (MOD_GATE_M, i, 0, j)),
        ],
        out_specs=pl.BlockSpec((gb, ROW_GROUP, tn), lambda i, j: (i, 0, j)),
        out_shape=jax.ShapeDtypeStruct(x3.shape, F32),
        compiler_params=_params("arbitrary", "arbitrary"),
        name="outproj",
    )(merged, w_out_bf, x3, mods)


ROUTE_E1, ROUTE_E2, ROUTE_W1, ROUTE_W2 = 0, 1, 2, 3


def _router_kernel(x_ref, g_ref, sc_ref, sh_ref, whi_ref, wlo_ref, b_ref, h_ref, route_ref, lo_scr, *, gb):
    def body(s, carry):
        h = _modulated_norm(x_ref[s], g_ref[...], sc_ref[0, s], sh_ref[0, s])
        hi = h.astype(BF16)
        rows = pl.ds(pl.multiple_of(s * ROW_GROUP, ROW_GROUP), ROW_GROUP)
        h_ref[rows, :] = hi
        lo_scr[rows, :] = (h - hi.astype(F32)).astype(BF16)
        return carry
    lax.fori_loop(0, gb, body, 0)

    hi = h_ref[...]
    logits = _dot(hi, whi_ref[...]) + _dot(hi, wlo_ref[...]) + _dot(lo_scr[...], whi_ref[...]) + b_ref[...]
    lane = lax.broadcasted_iota(jnp.int32, logits.shape, 1)
    ninf = -jnp.inf

    lane_f = lane.astype(F32)

    def first_max(vals):
        mx = vals.max(axis=1, keepdims=True)
        idx = jnp.where(vals == mx, lane_f, float(V7X_LANES)).min(axis=1, keepdims=True)
        return mx, idx

    gl = jnp.where(lane < N_GROUPS, logits, ninf)
    gmax, gidx = first_max(gl)
    g_weight = 1.0 / jnp.exp(gl - gmax).sum(axis=1, keepdims=True)
    lo_lane = N_GROUPS + EXPERTS_PER_GROUP * gidx
    el = jnp.where(lane_f >= lo_lane, jnp.where(lane_f < lo_lane + EXPERTS_PER_GROUP, logits, ninf), ninf)
    m1, i1 = first_max(el)
    m2, i2 = first_max(jnp.where(lane_f == i1, ninf, el))
    e21 = jnp.exp(m2 - m1)
    p1 = 1.0 / (1.0 + e21)
    p2 = e21 / (1.0 + e21)
    route = jnp.where(lane == ROUTE_E1, i1 - N_GROUPS,
                      jnp.where(lane == ROUTE_E2, i2 - N_GROUPS,
                                jnp.where(lane == ROUTE_W1, g_weight * p1,
                                          jnp.where(lane == ROUTE_W2, g_weight * p2, 0.0))))
    route_ref[...] = route


def _router(x3, g, mods, w_hi, w_lo, bias, gb, h_rows):
    ng, _, d = x3.shape
    m = ng * ROW_GROUP
    tm = gb * ROW_GROUP
    assert h_rows >= m

    def mod_spec(which):
        return pl.BlockSpec((1, gb, 1, d), lambda i: (which, i, 0, 0))

    w_spec = pl.BlockSpec((d, V7X_LANES), lambda i: (0, 0))
    return pl.pallas_call(
        functools.partial(_router_kernel, gb=gb),
        grid=(ng // gb,),
        in_specs=[
            pl.BlockSpec((gb, ROW_GROUP, d), lambda i: (i, 0, 0)),
            pl.BlockSpec((1, d), lambda i: (0, 0)),
            mod_spec(MOD_SCALE_F), mod_spec(MOD_SHIFT_F), w_spec, w_spec,
            pl.BlockSpec((1, V7X_LANES), lambda i: (0, 0)),
        ],
        out_specs=[
            pl.BlockSpec((tm, d), lambda i: (i, 0)),
            pl.BlockSpec((tm, V7X_LANES), lambda i: (i, 0)),
        ],
        out_shape=[jax.ShapeDtypeStruct((h_rows, d), BF16), jax.ShapeDtypeStruct((m, V7X_LANES), F32)],
        scratch_shapes=[pltpu.VMEM((tm, d), BF16)],
        compiler_params=_params("arbitrary"),
        name="router",
    )(x3, g, mods, mods, w_hi, w_lo, bias)


def _moe_kernel(te_ref, nv_ref, x_ref, wg_ref, wu_ref, wd_ref, o_ref, wg_scr, wu_scr, wd_scr):
    t = pl.program_id(0)
    changed = te_ref[t] != te_ref[jnp.maximum(t - 1, 0)]

    @pl.when((t == 0) | changed)
    def _():
        wg_scr[...] = wg_ref[0].astype(BF16)
        wu_scr[...] = wu_ref[0].astype(BF16)
        wd_scr[...] = wd_ref[0].astype(BF16)

    @pl.when(t < nv_ref[0])
    def _():
        x = x_ref[...]
        g = _dot(x, wg_scr[...])
        u = _dot(x, wu_scr[...])
        hidden = (g * _sigmoid(g)) * u
        o_ref[...] = _dot(hidden.astype(BF16), wd_scr[...]).astype(BF16)

    @pl.when(t >= nv_ref[0])
    def _():
        o_ref[...] = jnp.zeros(o_ref.shape, BF16)


def _moe(tile_expert, n_valid, xs, w_gate, w_up, w_down):
    rows, d = xs.shape
    f = w_gate.shape[2]
    n_tiles = rows // MOE_TILE

    def x_map(t, te, nv):
        return (jnp.minimum(t, nv[0] - 1), 0)

    grid_spec = pltpu.PrefetchScalarGridSpec(
        num_scalar_prefetch=2,
        grid=(n_tiles,),
        in_specs=[
            pl.BlockSpec((MOE_TILE, d), x_map),
            pl.BlockSpec((1, d, f), lambda t, te, nv: (te[t], 0, 0)),
            pl.BlockSpec((1, d, f), lambda t, te, nv: (te[t], 0, 0)),
            pl.BlockSpec((1, f, d), lambda t, te, nv: (te[t], 0, 0)),
        ],
        out_specs=pl.BlockSpec((MOE_TILE, d), lambda t, te, nv: (t, 0)),
        scratch_shapes=[pltpu.VMEM((d, f), BF16), pltpu.VMEM((d, f), BF16), pltpu.VMEM((f, d), BF16)],
    )
    return pl.pallas_call(
        _moe_kernel,
        grid_spec=grid_spec,
        out_shape=jax.ShapeDtypeStruct((rows, d), BF16),
        compiler_params=_params("arbitrary"),
        name="moe",
    )(tile_expert, n_valid, xs, w_gate, w_up, w_down)


def _dispatch(route):
    n = route.shape[0]
    e = jnp.concatenate([route[:, ROUTE_E1], route[:, ROUTE_E2]]).astype(jnp.int32)
    tok = jnp.concatenate([jnp.arange(n, dtype=jnp.int32)] * 2)
    onehot = (e[:, None] == jnp.arange(N_EXPERTS, dtype=jnp.int32)[None, :]).astype(jnp.int32)
    before = jnp.cumsum(onehot, axis=0) - onehot
    rank = jnp.sum(before * onehot, axis=1)
    counts = jnp.sum(onehot, axis=0)
    padded = ((counts + MOE_TILE - 1) // MOE_TILE) * MOE_TILE
    ends = jnp.cumsum(padded)
    pos = (ends - padded)[e] + rank
    n_tiles = -(-(2 * n + N_EXPERTS * (MOE_TILE - 1)) // (4 * MOE_TILE)) * 4
    tile_start = jnp.arange(n_tiles, dtype=jnp.int32) * MOE_TILE
    n_valid = (ends[-1] // MOE_TILE).astype(jnp.int32)
    tile_expert = jnp.sum((ends[None, :] <= tile_start[:, None]).astype(jnp.int32), axis=1)
    last_expert = tile_expert[jnp.maximum(n_valid - 1, 0)]
    tile_expert = jnp.where(tile_start < ends[-1], tile_expert, last_expert)
    src_tok = jnp.zeros((n_tiles * MOE_TILE,), jnp.int32).at[pos].set(tok)
    return pos[:n], pos[n:], src_tok, tile_expert, n_valid.reshape(1)


def _final_kernel(x_ref, y1_ref, y2_ref, route_ref, gate_ref, o_ref):
    r = route_ref[...]
    w1 = r[:, :, ROUTE_W1:ROUTE_W1 + 1]
    w2 = r[:, :, ROUTE_W2:ROUTE_W2 + 1]
    y = w1 * y1_ref[...].astype(F32) + w2 * y2_ref[...].astype(F32)
    o_ref[...] = x_ref[...] + gate_ref[0] * y


def _final(x3, y1, y2, route3, mods, gb):
    ng, _, d = x3.shape
    tn = min(1024, d)
    blk = pl.BlockSpec((gb, ROW_GROUP, tn), lambda i, j: (i, 0, j))
    return pl.pallas_call(
        _final_kernel,
        grid=(ng // gb, d // tn),
        in_specs=[
            blk, blk, blk,
            pl.BlockSpec((gb, ROW_GROUP, V7X_LANES), lambda i, j: (i, 0, 0)),
            pl.BlockSpec((1, gb, 1, tn), lambda i, j: (MOD_GATE_F, i, 0, j)),
        ],
        out_specs=blk,
        out_shape=jax.ShapeDtypeStruct(x3.shape, F32),
        compiler_params=_params("arbitrary", "arbitrary"),
        name="final",
    )(x3, y1, y2, route3, mods)


def _layer(xp, xs, csk, csv, cbk, cbv, c_prompt, c_sample, norm_mix, norm_ffn, w_ada, b_ada, w_in, q_norm,
           k_norm, rel_table, w_proj_sb, w_proj_band, w_out, w_rg, b_rg, w_re, b_re, w_gate, w_up, w_down):
    bp, sp_len, d = xp.shape
    bs, ts, _ = xs.shape
    d_in = w_in.shape[1]
    assert sp_len % BAND_QB == 0 and ts == ROW_GROUP and sp_len % ROW_GROUP == 0

    n_c = bp + bs
    c_pad = jnp.concatenate([c_prompt, c_sample, jnp.zeros((-n_c % 8, d), F32)], axis=0)
    mod = _ada(c_pad, w_ada, b_ada)
    gp = sp_len // ROW_GROUP
    mod_p = jnp.repeat(mod[:, :bp], gp, axis=1)[:, :, None, :]
    mod_s = mod[:, bp:n_c][:, :, None, :]

    w_in_bf = w_in.astype(BF16)
    w_sb_bf = w_proj_sb.astype(BF16)
    w_band_bf = w_proj_band.astype(BF16)
    w_out_bf = w_out.astype(BF16)
    qn = q_norm.reshape(1, W_HEADS)
    kn = k_norm.reshape(1, W_HEADS)
    g_mix = norm_mix.reshape(1, d)
    g_ffn = norm_ffn.reshape(1, d)
    lb = cbk.shape[1]
    assert lb % CHUNK == 0 and ts == CHUNK
    bias_blocks = _bias_blocks(rel_table)
    bias = _band_bias(bias_blocks, rel_table)
    bias_step = _band_step_bias(bias_blocks, rel_table, lb)
    u = _strict_lower_twice(SB_T)

    w_router = jnp.zeros((d, V7X_LANES), F32).at[:, :N_GROUPS].set(w_rg)
    w_router = w_router.at[:, N_GROUPS:N_GROUPS + N_EXPERTS].set(w_re)
    w_router_hi = w_router.astype(BF16)
    w_router_lo = (w_router - w_router_hi.astype(F32)).astype(BF16)
    b_router = jnp.zeros((1, V7X_LANES), F32).at[0, :N_GROUPS].set(b_rg)
    b_router = b_router.at[0, N_GROUPS:N_GROUPS + N_EXPERTS].set(b_re.reshape(-1))

    xp3 = xp.reshape(bp * gp, ROW_GROUP, d)
    xs3 = xs.reshape(bs, ROW_GROUP, d)
    gb_p = min(ROWWISE_GROUPS, bp * gp)
    gb_s = min(ROWWISE_GROUPS, bs)

    def mixer(x3, mods, gb, attend):
        proj, ka, va, kb, vb = _proj(x3, g_mix, mods, w_in_bf, qn, kn, gb)
        o_sb, o_band = attend(proj)
        merged = _merge(o_sb, o_band, w_sb_bf, w_band_bf, proj, gb * ROW_GROUP)
        x1 = _outproj(merged, w_out_bf, x3, mods, gb)
        return x1, ka, va, kb, vb

    def attend_prompt(proj):
        p3 = proj.reshape(bp, sp_len, d_in)
        return (_sb_prompt(p3, u).reshape(bp * sp_len, W_HEADS),
                _band_prompt(p3, bias).reshape(bp * sp_len, W_HEADS))

    def attend_sample(proj):
        p3 = proj.reshape(bs, ts, d_in)
        def rows(cache):
            return cache.reshape(bs, -1, HEAD_DIM)

        o_sb = _sb_step(p3, rows(csk), rows(csv), u)
        o_band = _band_step(p3, rows(cbk), rows(cbv), bias_step)
        return o_sb.reshape(bs * ts, W_HEADS), o_band.reshape(bs * ts, W_HEADS)

    x1p, kap, vap, kbp, vbp = mixer(xp3, mod_p, min(MATMUL_GROUPS, bp * gp), attend_prompt)
    x1s, kas, vas, kbs, vbs = mixer(xs3, mod_s, min(MATMUL_GROUPS, bs), attend_sample)

    h_rows = 2 * (bp * sp_len + bs * ts)

    def take_rows(a, idx):
        return a.at[idx].get(mode="promise_in_bounds")

    def ffn(x1, mods, gb):
        ng = x1.shape[0]
        h2, route = _router(x1, g_ffn, mods, w_router_hi, w_router_lo, b_router, gb, h_rows)
        pos1, pos2, src_tok, tile_expert, n_valid = _dispatch(route)
        y_sorted = _moe(tile_expert, n_valid, take_rows(h2, src_tok), w_gate, w_up, w_down)
        return _final(x1, take_rows(y_sorted, pos1).reshape(ng, ROW_GROUP, d),
                      take_rows(y_sorted, pos2).reshape(ng, ROW_GROUP, d),
                      route.reshape(ng, ROW_GROUP, V7X_LANES), mods, gb)

    yp = ffn(x1p, mod_p, gb_p).reshape(bp, sp_len, d)
    ys = ffn(x1s, mod_s, gb_s).reshape(bs, ts, d)

    def heads(a, b):
        return a.reshape(b, -1, N_HEADS, HEAD_DIM)

    n_band = min(BAND_PAST, sp_len)
    return (yp, ys, heads(kap, bp), heads(vap, bp), heads(kbp, bp)[:, -n_band:], heads(vbp, bp)[:, -n_band:],
            heads(kas, bs), heads(vas, bs), heads(kbs, bs), heads(vbs, bs))


def kernel(x_prompt, x_sample, cache_sb_k, cache_sb_v, cache_band_k, cache_band_v, c_prompt, c_sample, norm_mix, norm_ffn, w_ada, b_ada, w_in, q_norm_band, k_norm_band, rel_bias_band, w_proj_sb, w_proj_band, w_out, w_router_group, b_router_group, w_router_expert, b_router_expert, w_gate, w_up, w_down):
    depth = w_in.shape[0]
    xp, xs = x_prompt, x_sample
    outs = [[] for _ in range(8)]
    for l in range(depth):
        res = _layer(xp, xs, cache_sb_k[l], cache_sb_v[l], cache_band_k[l], cache_band_v[l], c_prompt, c_sample,
                     norm_mix[l], norm_ffn[l], w_ada[l], b_ada[l], w_in[l], q_norm_band[l], k_norm_band[l],
                     rel_bias_band[l], w_proj_sb[l], w_proj_band[l], w_out[l], w_router_group[l],
                     b_router_group[l], w_router_expert[l], b_router_expert[l], w_gate[l], w_up[l], w_down[l])
        xp, xs = res[0], res[1]
        for acc, r in zip(outs, res[2:]):
            acc.append(r)
    return (xp, xs) + tuple(jnp.stack(o, axis=0) for o in outs)
```

```python
import functools

import jax
import jax.numpy as jnp
from jax import lax
from jax.experimental import pallas as pl
from jax.experimental.pallas import tpu as pltpu

F32 = jnp.float32
BF16 = jnp.bfloat16

EPS = 1e-6
HEAD_DIM = 128
N_HEADS = 8
W_HEADS = N_HEADS * HEAD_DIM
CHUNK = 64
BAND_LEFT_CHUNKS = 8
BAND_PAST = BAND_LEFT_CHUNKS * CHUNK
REL_CLIP = 128
N_GROUPS = 4
EXPERTS_PER_GROUP = 8
N_EXPERTS = N_GROUPS * EXPERTS_PER_GROUP
QK_SCALE = HEAD_DIM ** -0.5
NEG_BIG = -1e30

V7X_LANES = 128
V7X_VMEM_LIMIT = 56 * 1024 * 1024
ROW_GROUP = 64
MOE_TILE = 256
MATMUL_GROUPS = 16
MATMUL_TN = 1024
ROWWISE_GROUPS = 8


def _params(*sem):
    return pltpu.CompilerParams(dimension_semantics=sem, vmem_limit_bytes=V7X_VMEM_LIMIT)


def _sigmoid(x):
    return 1.0 / (1.0 + jnp.exp(-x))


def _dot(a, b):
    return jnp.dot(a, b, preferred_element_type=F32)


def _dot_nt(a, b):
    return lax.dot_general(a, b, (((1,), (1,)), ((), ())), preferred_element_type=F32)


def _ada_kernel(c_ref, w_ref, b_ref, o_ref):
    c = c_ref[...]
    a = (c * _sigmoid(c)).astype(BF16)
    o_ref[0] = _dot(a, w_ref[...].astype(BF16)) + b_ref[...]


def _ada(c_pad, w_ada, b_ada):
    r, d = c_pad.shape
    tn = min(1024, d)
    per = d // tn
    return pl.pallas_call(
        _ada_kernel,
        grid=(6 * per,),
        in_specs=[
            pl.BlockSpec((r, d), lambda j: (0, 0)),
            pl.BlockSpec((d, tn), lambda j: (0, j)),
            pl.BlockSpec((1, tn), lambda j: (0, j)),
        ],
        out_specs=pl.BlockSpec((1, r, tn), lambda j: (j // per, 0, j % per)),
        out_shape=jax.ShapeDtypeStruct((6, r, d), F32),
        compiler_params=_params("arbitrary"),
        name="ada",
    )(c_pad, w_ada, b_ada.reshape(1, 6 * d))


def _modulated_norm(x, g, scale, shift):
    ms = jnp.mean(x * x, axis=-1, keepdims=True)
    return (x * lax.rsqrt(ms + EPS) * g) * (1.0 + scale) + shift


def _head_norm(a, gain_ref):
    outs = []
    for hh in range(a.shape[1] // HEAD_DIM):
        blk = a[:, hh * HEAD_DIM:(hh + 1) * HEAD_DIM]
        ms = jnp.mean(blk * blk, axis=-1, keepdims=True)
        outs.append(blk * lax.rsqrt(ms + EPS) * gain_ref[:, hh * HEAD_DIM:(hh + 1) * HEAD_DIM])
    return jnp.concatenate(outs, axis=1)


def _proj_kernel(x_ref, g_ref, sc_ref, sh_ref, w_ref, qn_ref, kn_ref,
                 proj_ref, ka_ref, va_ref, kb_ref, vb_ref, h_scr, *, gb, per):
    j = pl.program_id(1)

    @pl.when(j == 0)
    def _():
        def body(s, carry):
            h = _modulated_norm(x_ref[s], g_ref[...], sc_ref[0, s], sh_ref[0, s])
            h_scr[pl.ds(pl.multiple_of(s * ROW_GROUP, ROW_GROUP), ROW_GROUP), :] = h.astype(BF16)
            return carry
        lax.fori_loop(0, gb, body, 0)

    acc = _dot(h_scr[...], w_ref[...])
    sec = j // per

    @pl.when(sec == 0)
    def _():
        proj_ref[...] = (acc * QK_SCALE).astype(BF16)

    @pl.when(sec == 1)
    def _():
        proj_ref[...] = acc.astype(BF16)
        ka_ref[...] = acc

    @pl.when(sec == 2)
    def _():
        proj_ref[...] = acc.astype(BF16)
        va_ref[...] = acc

    @pl.when(sec == 3)
    def _():
        proj_ref[...] = (_head_norm(acc, qn_ref) * QK_SCALE).astype(BF16)

    @pl.when(sec == 4)
    def _():
        n = _head_norm(acc, kn_ref)
        proj_ref[...] = n.astype(BF16)
        kb_ref[...] = n

    @pl.when(sec == 5)
    def _():
        proj_ref[...] = acc.astype(BF16)
        vb_ref[...] = acc

    @pl.when(sec >= 6)
    def _():
        proj_ref[...] = acc.astype(BF16)


MOD_SHIFT_M, MOD_SCALE_M, MOD_GATE_M, MOD_SHIFT_F, MOD_SCALE_F, MOD_GATE_F = range(6)


def _proj(x3, g, mods, w_in_bf, qn, kn, gb):
    ng, _, d = x3.shape
    d_in = w_in_bf.shape[1]
    m = ng * ROW_GROUP
    tm = gb * ROW_GROUP
    tn = 512
    per = W_HEADS // tn

    def sect(s):
        return lambda i, j: (i, jnp.clip(j - s * per, 0, per - 1))

    def mod_spec(which):
        return pl.BlockSpec((1, gb, 1, d), lambda i, j: (which, i, 0, 0))

    kv_shape = jax.ShapeDtypeStruct((m, W_HEADS), F32)
    return pl.pallas_call(
        functools.partial(_proj_kernel, gb=gb, per=per),
        grid=(ng // gb, d_in // tn),
        in_specs=[
            pl.BlockSpec((gb, ROW_GROUP, d), lambda i, j: (i, 0, 0)),
            pl.BlockSpec((1, d), lambda i, j: (0, 0)),
            mod_spec(MOD_SCALE_M), mod_spec(MOD_SHIFT_M),
            pl.BlockSpec((d, tn), lambda i, j: (0, j)),
            pl.BlockSpec((1, tn), lambda i, j: (0, jnp.clip(j - 3 * per, 0, per - 1))),
            pl.BlockSpec((1, tn), lambda i, j: (0, jnp.clip(j - 4 * per, 0, per - 1))),
        ],
        out_specs=[
            pl.BlockSpec((tm, tn), lambda i, j: (i, j)),
            pl.BlockSpec((tm, tn), sect(1)),
            pl.BlockSpec((tm, tn), sect(2)),
            pl.BlockSpec((tm, tn), sect(4)),
            pl.BlockSpec((tm, tn), sect(5)),
        ],
        out_shape=[jax.ShapeDtypeStruct((m, d_in), BF16), kv_shape, kv_shape, kv_shape, kv_shape],
        scratch_shapes=[pltpu.VMEM((tm, d), BF16)],
        compiler_params=_params("arbitrary", "arbitrary"),
        name="proj",
    )(x3, g, mods, mods, w_in_bf, qn, kn)


def _strict_lower_twice(n):
    row = lax.broadcasted_iota(jnp.int32, (2 * n, n), 0)
    col = lax.broadcasted_iota(jnp.int32, (2 * n, n), 1)
    row = jnp.where(row >= n, row - n, row)
    return jnp.where(row > col, 1.0, 0.0).astype(BF16)


def _sb_heads(loads, n_heads, u, carries, diagonal):
    qkv = [[load(h) for h in range(n_heads)] for load in loads]
    zs = [[_dot_nt(q, k) for q, k, _ in blk] for blk in qkv]
    if diagonal:
        row = lax.broadcasted_iota(jnp.int32, zs[0][0].shape, 0)
        col = lax.broadcasted_iota(jnp.int32, zs[0][0].shape, 1)
        valid = col < row
    sps, tails = [], []
    for blk in zs:
        sps.append([])
        tails.append([])
        for z in blk:
            sp = jnp.maximum(z, 0.0) + jnp.log(1.0 + jnp.exp(-jnp.abs(z)))
            if diagonal:
                sp = jnp.where(valid, sp, 0.0)
            hi = sp.astype(BF16)
            lo = (sp - hi.astype(F32)).astype(BF16)
            sps[-1].append(sp)
            tails[-1].append(_dot(jnp.concatenate([hi, lo], axis=1), u))
    pvs, new = [], []
    for h in range(n_heads):
        tq = zs[0][h].shape[0]
        carry = jnp.zeros((tq, 1), F32) if carries is None else carries[:, h * HEAD_DIM:h * HEAD_DIM + 1]
        pv = None
        for b in range(len(loads)):
            z, sp, tail = zs[b][h], sps[b][h], tails[b][h]
            loga = z - sp - tail - carry
            if diagonal:
                loga = jnp.where(valid, loga, NEG_BIG)
            term = _dot(jnp.exp(loga).astype(BF16), qkv[b][h][2])
            pv = term if pv is None else pv + term
            carry = carry + tail[:, 0:1] + sp[:, 0:1]
        pvs.append(pv)
        new.append(jnp.broadcast_to(carry, (tq, HEAD_DIM)))
    return jnp.concatenate(pvs, axis=1), jnp.concatenate(new, axis=1)


SB_T = 256
SB_HEADS_PER_STEP = 8
SB_STEP_BLOCKS = 1


def _head_slice(h):
    return slice(h * HEAD_DIM, (h + 1) * HEAD_DIM)


def _sb_prompt_kernel(q_ref, k_ref, v_ref, u_ref, o_ref, acc_scr, carry_scr):
    qi = pl.program_id(2)
    u = u_ref[...]

    def block(r, carries, diagonal):
        def load(h):
            hs = _head_slice(h)
            return q_ref[0, :, hs], k_ref[0, pl.ds(r, SB_T), hs], v_ref[0, pl.ds(r, SB_T), hs]
        return _sb_heads([load], SB_HEADS_PER_STEP, u, carries, diagonal)

    acc_scr[...], carry_scr[...] = block(pl.multiple_of(qi * SB_T, SB_T), None, True)

    def body(step, c):
        pv, carries = block(pl.multiple_of((qi - 1 - step) * SB_T, SB_T), carry_scr[...], False)
        acc_scr[...] += pv
        carry_scr[...] = carries
        return c

    lax.fori_loop(0, qi, body, 0)
    o_ref[0] = acc_scr[...].astype(BF16)


def _sb_prompt(proj3, u):
    b, s, _ = proj3.shape
    hp = SB_HEADS_PER_STEP
    w = hp * HEAD_DIM
    kcol, vcol = W_HEADS // w, 2 * W_HEADS // w
    return pl.pallas_call(
        _sb_prompt_kernel,
        grid=(b, N_HEADS // hp, s // SB_T),
        in_specs=[
            pl.BlockSpec((1, SB_T, w), lambda bi, h, qi: (bi, qi, h)),
            pl.BlockSpec((1, s, w), lambda bi, h, qi: (bi, 0, kcol + h)),
            pl.BlockSpec((1, s, w), lambda bi, h, qi: (bi, 0, vcol + h)),
            pl.BlockSpec((2 * SB_T, SB_T), lambda bi, h, qi: (0, 0)),
        ],
        out_specs=pl.BlockSpec((1, SB_T, w), lambda bi, h, qi: (bi, qi, h)),
        out_shape=jax.ShapeDtypeStruct((b, s, W_HEADS), BF16),
        scratch_shapes=[pltpu.VMEM((SB_T, w), F32), pltpu.VMEM((SB_T, w), F32)],
        compiler_params=_params("arbitrary", "arbitrary", "arbitrary"),
        name="sb_prompt",
    )(proj3, proj3, proj3, u)


def _sb_step_kernel(q_ref, kn_ref, vn_ref, ck_ref, cv_ref, u_ref, o_ref, acc_scr, carry_scr, *, tkb, t):
    j = pl.program_id(1)
    u = u_ref[...]

    @pl.when(j == 0)
    def _():
        def load(h):
            hs = _head_slice(h)
            return q_ref[0, :, hs], kn_ref[0, :, hs], vn_ref[0, :, hs]
        acc_scr[...], carry_scr[...] = _sb_heads([load], N_HEADS, _strict_lower_twice(t), None, True)

    def body(step, c):
        def loader(blk):
            r = pl.multiple_of((tkb - (step * SB_STEP_BLOCKS + blk + 1) * SB_T) * N_HEADS, SB_T * N_HEADS)

            def load(h):
                rows = pl.ds(r + h, SB_T, stride=N_HEADS)
                return (q_ref[0, :, _head_slice(h)], ck_ref[0, rows, :].astype(BF16),
                        cv_ref[0, rows, :].astype(BF16))
            return load

        pv, carries = _sb_heads([loader(blk) for blk in range(SB_STEP_BLOCKS)], N_HEADS, u, carry_scr[...], False)
        acc_scr[...] += pv
        carry_scr[...] = carries
        return c

    lax.fori_loop(0, tkb // (SB_T * SB_STEP_BLOCKS), body, 0)

    @pl.when(j == pl.num_programs(1) - 1)
    def _():
        o_ref[0] = acc_scr[...].astype(BF16)


def _sb_step(proj3, cache_k, cache_v, u):
    b, t, _ = proj3.shape
    p = cache_k.shape[1] // N_HEADS
    tkb = min(1024, p)
    nj = p // tkb
    cache_spec = pl.BlockSpec((1, tkb * N_HEADS, HEAD_DIM), lambda bi, j: (bi, nj - 1 - j, 0))
    return pl.pallas_call(
        functools.partial(_sb_step_kernel, tkb=tkb, t=t),
        grid=(b, nj),
        in_specs=[
            pl.BlockSpec((1, t, W_HEADS), lambda bi, j: (bi, 0, 0)),
            pl.BlockSpec((1, t, W_HEADS), lambda bi, j: (bi, 0, 1)),
            pl.BlockSpec((1, t, W_HEADS), lambda bi, j: (bi, 0, 2)),
            cache_spec, cache_spec,
            pl.BlockSpec((2 * SB_T, SB_T), lambda bi, j: (0, 0)),
        ],
        out_specs=pl.BlockSpec((1, t, W_HEADS), lambda bi, j: (bi, 0, 0)),
        out_shape=jax.ShapeDtypeStruct((b, t, W_HEADS), BF16),
        scratch_shapes=[pltpu.VMEM((t, W_HEADS), F32), pltpu.VMEM((t, W_HEADS), F32)],
        compiler_params=_params("arbitrary", "arbitrary"),
        name="sb_step",
    )(proj3, proj3, proj3, cache_k, cache_v, u)


BAND_QB = 4 * CHUNK
BAND_KB = 3


BIAS_NEAR = REL_CLIP // CHUNK + 1


def _bias_blocks_kernel(table_ref, o_ref):
    h = pl.program_id(0)
    i = lax.broadcasted_iota(jnp.int32, (CHUNK, CHUNK), 0)
    j = lax.broadcasted_iota(jnp.int32, (CHUNK, CHUNK), 1)
    for d in range(BIAS_NEAR):
        idx = jnp.clip(i - j + d * CHUNK, -REL_CLIP, REL_CLIP) + REL_CLIP

        def body(r, acc, idx=idx):
            return jnp.where(idx == r, table_ref[h, r], acc)

        lo = max(d * CHUNK - (CHUNK - 1), -REL_CLIP) + REL_CLIP
        hi = min(d * CHUNK + (CHUNK - 1), REL_CLIP) + REL_CLIP
        o_ref[0, d] = lax.fori_loop(lo, hi + 1, body, jnp.zeros((CHUNK, CHUNK), F32))


def _bias_blocks(table):
    nh = table.shape[0]
    return pl.pallas_call(
        _bias_blocks_kernel,
        grid=(nh,),
        in_specs=[pl.BlockSpec(memory_space=pltpu.SMEM)],
        out_specs=pl.BlockSpec((1, BIAS_NEAR, CHUNK, CHUNK), lambda h: (h, 0, 0, 0)),
        out_shape=jax.ShapeDtypeStruct((nh, BIAS_NEAR, CHUNK, CHUNK), F32),
        compiler_params=_params("arbitrary"),
        name="bias_blocks",
    )(table.astype(F32))


def _chunk_bias(blocks, far, dist):
    if dist < 0 or dist > BAND_LEFT_CHUNKS:
        return jnp.full(far.shape, NEG_BIG, F32)
    return blocks[:, dist] if dist < BIAS_NEAR else far


def _band_bias(blocks, table):
    far = jnp.broadcast_to(table[:, -1].astype(F32)[:, None, None], (table.shape[0], CHUNK, CHUNK))
    qc, kc = BAND_QB // CHUNK, BAND_KB * BAND_QB // CHUNK
    rows = [jnp.concatenate([_chunk_bias(blocks, far, r - c + BAND_LEFT_CHUNKS) for c in range(kc)], axis=2)
            for r in range(qc)]
    return jnp.concatenate(rows, axis=1)


def _band_step_bias(blocks, table, lb):
    far = jnp.broadcast_to(table[:, -1].astype(F32)[:, None, None], (table.shape[0], CHUNK, CHUNK))
    nc = lb // CHUNK
    return jnp.concatenate([blocks[:, nc - c] if nc - c < BIAS_NEAR else far for c in range(nc + 1)], axis=2)


def _softmax_pv_heads(scores, values):
    ps, dens = [], []
    for sc in scores:
        m = sc[0].max(axis=1, keepdims=True)
        for s in sc[1:]:
            m = jnp.maximum(m, s.max(axis=1, keepdims=True))
        p = [jnp.exp(s - m) for s in sc]
        den = p[0].sum(axis=1, keepdims=True)
        for pc in p[1:]:
            den = den + pc.sum(axis=1, keepdims=True)
        ps.append(p)
        dens.append(den)
    outs = []
    for p, den, vals in zip(ps, dens, values):
        num = _dot(p[0].astype(BF16), vals[0])
        for pc, v in zip(p[1:], vals[1:]):
            num = num + _dot(pc.astype(BF16), v)
        outs.append(num / den)
    return jnp.concatenate(outs, axis=1)


def _band_prompt_kernel(q_ref, k0_ref, k1_ref, k2_ref, v0_ref, v1_ref, v2_ref, bias_ref, o_ref):
    qi = pl.program_id(1)
    scores, values = [], []
    for h in range(N_HEADS):
        hs = _head_slice(h)
        q = q_ref[0, :, hs]
        sc = []
        for c, k_ref in enumerate((k0_ref, k1_ref, k2_ref)):
            s = _dot_nt(q, k_ref[0, :, hs]) + bias_ref[h, :, c * BAND_QB:(c + 1) * BAND_QB]
            if c < BAND_KB - 1:
                s = jnp.where(qi + c >= BAND_KB - 1, s, NEG_BIG)
            sc.append(s)
        scores.append(sc)
        values.append([v_ref[0, :, hs] for v_ref in (v0_ref, v1_ref, v2_ref)])
    o_ref[0] = _softmax_pv_heads(scores, values).astype(BF16)


def _band_prompt(proj3, bias):
    b, s, _ = proj3.shape
    qcol, kcol, vcol = 3, 4, 5

    def kv_spec(col, back):
        return pl.BlockSpec((1, BAND_QB, W_HEADS), lambda bi, qi: (bi, jnp.maximum(qi - back, 0), col))

    return pl.pallas_call(
        _band_prompt_kernel,
        grid=(b, s // BAND_QB),
        in_specs=[
            pl.BlockSpec((1, BAND_QB, W_HEADS), lambda bi, qi: (bi, qi, qcol)),
            kv_spec(kcol, 2), kv_spec(kcol, 1), kv_spec(kcol, 0),
            kv_spec(vcol, 2), kv_spec(vcol, 1), kv_spec(vcol, 0),
            pl.BlockSpec((N_HEADS, BAND_QB, BAND_KB * BAND_QB), lambda bi, qi: (0, 0, 0)),
        ],
        out_specs=pl.BlockSpec((1, BAND_QB, W_HEADS), lambda bi, qi: (bi, qi, 0)),
        out_shape=jax.ShapeDtypeStruct((b, s, W_HEADS), BF16),
        compiler_params=_params("arbitrary", "arbitrary"),
        name="band_prompt",
    )(proj3, proj3, proj3, proj3, proj3, proj3, proj3, bias)


def _band_step_kernel(q_ref, kn_ref, vn_ref, ck_ref, cv_ref, bias_ref, o_ref, *, lb):
    scores, values = [], []
    for h in range(N_HEADS):
        hs = _head_slice(h)
        q = q_ref[0, :, hs]
        rows = pl.ds(h, lb, stride=N_HEADS)
        scores.append([_dot_nt(q, ck_ref[0, rows, :].astype(BF16)) + bias_ref[h, :, :lb],
                       _dot_nt(q, kn_ref[0, :, hs]) + bias_ref[h, :, lb:]])
        values.append([cv_ref[0, rows, :].astype(BF16), vn_ref[0, :, hs]])
    o_ref[0] = _softmax_pv_heads(scores, values).astype(BF16)


def _band_step(proj3, cache_k, cache_v, bias):
    b, t, _ = proj3.shape
    lb = cache_k.shape[1] // N_HEADS
    cache_spec = pl.BlockSpec((1, lb * N_HEADS, HEAD_DIM), lambda bi: (bi, 0, 0))
    return pl.pallas_call(
        functools.partial(_band_step_kernel, lb=lb),
        grid=(b,),
        in_specs=[
            pl.BlockSpec((1, t, W_HEADS), lambda bi: (bi, 0, 3)),
            pl.BlockSpec((1, t, W_HEADS), lambda bi: (bi, 0, 4)),
            pl.BlockSpec((1, t, W_HEADS), lambda bi: (bi, 0, 5)),
            cache_spec, cache_spec,
            pl.BlockSpec((N_HEADS, t, lb + t), lambda bi: (0, 0, 0)),
        ],
        out_specs=pl.BlockSpec((1, t, W_HEADS), lambda bi: (bi, 0, 0)),
        out_shape=jax.ShapeDtypeStruct((b, t, W_HEADS), BF16),
        compiler_params=_params("arbitrary"),
        name="band_step",
    )(proj3, proj3, proj3, cache_k, cache_v, bias)


def _merge_kernel(osb_ref, obd_ref, wsb_ref, wbd_ref, ga_ref, gb_ref, o_ref):
    a = _dot(osb_ref[...], wsb_ref[...])
    b = _dot(obd_ref[...], wbd_ref[...])
    merged = _sigmoid(ga_ref[...].astype(F32)) * a + _sigmoid(gb_ref[...].astype(F32)) * b
    o_ref[...] = merged.astype(BF16)


def _merge(o_sb, o_band, w_sb_bf, w_band_bf, proj, tm):
    m = o_sb.shape[0]
    d = w_sb_bf.shape[1]
    tn = min(MATMUL_TN, d)
    ga_col = 6 * W_HEADS // tn
    gb_col = ga_col + d // tn
    return pl.pallas_call(
        _merge_kernel,
        grid=(m // tm, d // tn),
        in_specs=[
            pl.BlockSpec((tm, W_HEADS), lambda i, j: (i, 0)),
            pl.BlockSpec((tm, W_HEADS), lambda i, j: (i, 0)),
            pl.BlockSpec((W_HEADS, tn), lambda i, j: (0, j)),
            pl.BlockSpec((W_HEADS, tn), lambda i, j: (0, j)),
            pl.BlockSpec((tm, tn), lambda i, j: (i, ga_col + j)),
            pl.BlockSpec((tm, tn), lambda i, j: (i, gb_col + j)),
        ],
        out_specs=pl.BlockSpec((tm, tn), lambda i, j: (i, j)),
        out_shape=jax.ShapeDtypeStruct((m, d), BF16),
        compiler_params=_params("arbitrary", "arbitrary"),
        name="merge",
    )(o_sb, o_band, w_sb_bf, w_band_bf, proj, proj)


def _outproj_kernel(m_ref, w_ref, x_ref, gate_ref, o_ref, *, gb):
    acc = _dot(m_ref[...], w_ref[...])
    acc = acc.reshape(gb, ROW_GROUP, acc.shape[1])
    o_ref[...] = x_ref[...] + gate_ref[0] * acc


def _outproj(merged, w_out_bf, x3, mods, gb):
    ng, _, d = x3.shape
    tm = gb * ROW_GROUP
    tn = min(MATMUL_TN, d)
    return pl.pallas_call(
        functools.partial(_outproj_kernel, gb=gb),
        grid=(ng // gb, d // tn),
        in_specs=[
            pl.BlockSpec((tm, d), lambda i, j: (i, 0)),
            pl.BlockSpec((d, tn), lambda i, j: (0, j)),
            pl.BlockSpec((gb, ROW_GROUP, tn), lambda i, j: (i, 0, j)),
            pl.BlockSpec((1, gb, 1, tn), lambda i, j: (MOD_GATE_M, i, 0, j)),
        ],
        out_specs=pl.BlockSpec((gb, ROW_GROUP, tn), lambda i, j: (i, 0, j)),
        out_shape=jax.ShapeDtypeStruct(x3.shape, F32),
        compiler_params=_params("arbitrary", "arbitrary"),
        name="outproj",
    )(merged, w_out_bf, x3, mods)


ROUTE_E1, ROUTE_E2, ROUTE_W1, ROUTE_W2 = 0, 1, 2, 3


def _router_kernel(x_ref, g_ref, sc_ref, sh_ref, whi_ref, wlo_ref, b_ref, h_ref, route_ref, lo_scr, *, gb):
    def body(s, carry):
        h = _modulated_norm(x_ref[s], g_ref[...], sc_ref[0, s], sh_ref[0, s])
        hi = h.astype(BF16)
        rows = pl.ds(pl.multiple_of(s * ROW_GROUP, ROW_GROUP), ROW_GROUP)
        h_ref[rows, :] = hi
        lo_scr[rows, :] = (h - hi.astype(F32)).astype(BF16)
        return carry
    lax.fori_loop(0, gb, body, 0)

    hi = h_ref[...]
    logits = _dot(hi, whi_ref[...]) + _dot(hi, wlo_ref[...]) + _dot(lo_scr[...], whi_ref[...]) + b_ref[...]
    lane = lax.broadcasted_iota(jnp.int32, logits.shape, 1)
    ninf = -jnp.inf

    lane_f = lane.astype(F32)

    def first_max(vals):
        mx = vals.max(axis=1, keepdims=True)
        idx = jnp.where(vals == mx, lane_f, float(V7X_LANES)).min(axis=1, keepdims=True)
        return mx, idx

    gl = jnp.where(lane < N_GROUPS, logits, ninf)
    gmax, gidx = first_max(gl)
    g_weight = 1.0 / jnp.exp(gl - gmax).sum(axis=1, keepdims=True)
    lo_lane = N_GROUPS + EXPERTS_PER_GROUP * gidx
    el = jnp.where(lane_f >= lo_lane, jnp.where(lane_f < lo_lane + EXPERTS_PER_GROUP, logits, ninf), ninf)
    m1, i1 = first_max(el)
    m2, i2 = first_max(jnp.where(lane_f == i1, ninf, el))
    e21 = jnp.exp(m2 - m1)
    p1 = 1.0 / (1.0 + e21)
    p2 = e21 / (1.0 + e21)
    route = jnp.where(lane == ROUTE_E1, i1 - N_GROUPS,
                      jnp.where(lane == ROUTE_E2, i2 - N_GROUPS,
                                jnp.where(lane == ROUTE_W1, g_weight * p1,
                                          jnp.where(lane == ROUTE_W2, g_weight * p2, 0.0))))
    route_ref[...] = route


def _router(x3, g, mods, w_hi, w_lo, bias, gb, h_rows):
    ng, _, d = x3.shape
    m = ng * ROW_GROUP
    tm = gb * ROW_GROUP
    assert h_rows >= m

    def mod_spec(which):
        return pl.BlockSpec((1, gb, 1, d), lambda i: (which, i, 0, 0))

    w_spec = pl.BlockSpec((d, V7X_LANES), lambda i: (0, 0))
    return pl.pallas_call(
        functools.partial(_router_kernel, gb=gb),
        grid=(ng // gb,),
        in_specs=[
            pl.BlockSpec((gb, ROW_GROUP, d), lambda i: (i, 0, 0)),
            pl.BlockSpec((1, d), lambda i: (0, 0)),
            mod_spec(MOD_SCALE_F), mod_spec(MOD_SHIFT_F), w_spec, w_spec,
            pl.BlockSpec((1, V7X_LANES), lambda i: (0, 0)),
        ],
        out_specs=[
            pl.BlockSpec((tm, d), lambda i: (i, 0)),
            pl.BlockSpec((tm, V7X_LANES), lambda i: (i, 0)),
        ],
        out_shape=[jax.ShapeDtypeStruct((h_rows, d), BF16), jax.ShapeDtypeStruct((m, V7X_LANES), F32)],
        scratch_shapes=[pltpu.VMEM((tm, d), BF16)],
        compiler_params=_params("arbitrary"),
        name="router",
    )(x3, g, mods, mods, w_hi, w_lo, bias)


def _moe_kernel(te_ref, nv_ref, x_ref, wg_ref, wu_ref, wd_ref, o_ref, wg_scr, wu_scr, wd_scr):
    t = pl.program_id(0)
    changed = te_ref[t] != te_ref[jnp.maximum(t - 1, 0)]

    @pl.when((t == 0) | changed)
    def _():
        wg_scr[...] = wg_ref[0].astype(BF16)
        wu_scr[...] = wu_ref[0].astype(BF16)
        wd_scr[...] = wd_ref[0].astype(BF16)

    @pl.when(t < nv_ref[0])
    def _():
        x = x_ref[...]
        g = _dot(x, wg_scr[...])
        u = _dot(x, wu_scr[...])
        hidden = (g * _sigmoid(g)) * u
        o_ref[...] = _dot(hidden.astype(BF16), wd_scr[...]).astype(BF16)

    @pl.when(t >= nv_ref[0])
    def _():
        o_ref[...] = jnp.zeros(o_ref.shape, BF16)


def _moe(tile_expert, n_valid, xs, w_gate, w_up, w_down):
    rows, d = xs.shape
    f = w_gate.shape[2]
    n_tiles = rows // MOE_TILE

    def x_map(t, te, nv):
        return (jnp.minimum(t, nv[0] - 1), 0)

    grid_spec = pltpu.PrefetchScalarGridSpec(
        num_scalar_prefetch=2,
        grid=(n_tiles,),
        in_specs=[
            pl.BlockSpec((MOE_TILE, d), x_map),
            pl.BlockSpec((1, d, f), lambda t, te, nv: (te[t], 0, 0)),
            pl.BlockSpec((1, d, f), lambda t, te, nv: (te[t], 0, 0)),
            pl.BlockSpec((1, f, d), lambda t, te, nv: (te[t], 0, 0)),
        ],
        out_specs=pl.BlockSpec((MOE_TILE, d), lambda t, te, nv: (t, 0)),
        scratch_shapes=[pltpu.VMEM((d, f), BF16), pltpu.VMEM((d, f), BF16), pltpu.VMEM((f, d), BF16)],
    )
    return pl.pallas_call(
        _moe_kernel,
        grid_spec=grid_spec,
        out_shape=jax.ShapeDtypeStruct((rows, d), BF16),
        compiler_params=_params("arbitrary"),
        name="moe",
    )(tile_expert, n_valid, xs, w_gate, w_up, w_down)


def _dispatch(route):
    n = route.shape[0]
    e = jnp.concatenate([route[:, ROUTE_E1], route[:, ROUTE_E2]]).astype(jnp.int32)
    tok = jnp.concatenate([jnp.arange(n, dtype=jnp.int32)] * 2)
    onehot = (e[:, None] == jnp.arange(N_EXPERTS, dtype=jnp.int32)[None, :]).astype(jnp.int32)
    before = jnp.cumsum(onehot, axis=0) - onehot
    rank = jnp.sum(before * onehot, axis=1)
    counts = jnp.sum(onehot, axis=0)
    padded = ((counts + MOE_TILE - 1) // MOE_TILE) * MOE_TILE
    ends = jnp.cumsum(padded)
    pos = (ends - padded)[e] + rank
    n_tiles = -(-(2 * n + N_EXPERTS * (MOE_TILE - 1)) // (4 * MOE_TILE)) * 4
    tile_start = jnp.arange(n_tiles, dtype=jnp.int32) * MOE_TILE
    n_valid = (ends[-1] // MOE_TILE).astype(jnp.int32)
    tile_expert = jnp.sum((ends[None, :] <= tile_start[:, None]).astype(jnp.int32), axis=1)
    last_expert = tile_expert[jnp.maximum(n_valid - 1, 0)]
    tile_expert = jnp.where(tile_start < ends[-1], tile_expert, last_expert)
    src_tok = jnp.zeros((n_tiles * MOE_TILE,), jnp.int32).at[pos].set(tok)
    return pos[:n], pos[n:], src_tok, tile_expert, n_valid.reshape(1)


def _final_kernel(x_ref, y1_ref, y2_ref, route_ref, gate_ref, o_ref):
    r = route_ref[...]
    w1 = r[:, :, ROUTE_W1:ROUTE_W1 + 1]
    w2 = r[:, :, ROUTE_W2:ROUTE_W2 + 1]
    y = w1 * y1_ref[...].astype(F32) + w2 * y2_ref[...].astype(F32)
    o_ref[...] = x_ref[...] + gate_ref[0] * y


def _final(x3, y1, y2, route3, mods, gb):
    ng, _, d = x3.shape
    tn = min(1024, d)
    blk = pl.BlockSpec((gb, ROW_GROUP, tn), lambda i, j: (i, 0, j))
    return pl.pallas_call(
        _final_kernel,
        grid=(ng // gb, d // tn),
        in_specs=[
            blk, blk, blk,
            pl.BlockSpec((gb, ROW_GROUP, V7X_LANES), lambda i, j: (i, 0, 0)),
            pl.BlockSpec((1, gb, 1, tn), lambda i, j: (MOD_GATE_F, i, 0, j)),
        ],
        out_specs=blk,
        out_shape=jax.ShapeDtypeStruct(x3.shape, F32),
        compiler_params=_params("arbitrary", "arbitrary"),
        name="final",
    )(x3, y1, y2, route3, mods)


def _layer(xp, xs, csk, csv, cbk, cbv, c_prompt, c_sample, norm_mix, norm_ffn, w_ada, b_ada, w_in, q_norm,
           k_norm, rel_table, w_proj_sb, w_proj_band, w_out, w_rg, b_rg, w_re, b_re, w_gate, w_up, w_down):
    bp, sp_len, d = xp.shape
    bs, ts, _ = xs.shape
    d_in = w_in.shape[1]
    assert sp_len % BAND_QB == 0 and ts == ROW_GROUP and sp_len % ROW_GROUP == 0

    n_c = bp + bs
    c_pad = jnp.concatenate([c_prompt, c_sample, jnp.zeros((-n_c % 8, d), F32)], axis=0)
    mod = _ada(c_pad, w_ada, b_ada)
    gp = sp_len // ROW_GROUP
    mod_p = jnp.repeat(mod[:, :bp], gp, axis=1)[:, :, None, :]
    mod_s = mod[:, bp:n_c][:, :, None, :]

    w_in_bf = w_in.astype(BF16)
    w_sb_bf = w_proj_sb.astype(BF16)
    w_band_bf = w_proj_band.astype(BF16)
    w_out_bf = w_out.astype(BF16)
    qn = q_norm.reshape(1, W_HEADS)
    kn = k_norm.reshape(1, W_HEADS)
    g_mix = norm_mix.reshape(1, d)
    g_ffn = norm_ffn.reshape(1, d)
    lb = cbk.shape[1]
    assert lb % CHUNK == 0 and ts == CHUNK
    bias_blocks = _bias_blocks(rel_table)
    bias = _band_bias(bias_blocks, rel_table)
    bias_step = _band_step_bias(bias_blocks, rel_table, lb)
    u = _strict_lower_twice(SB_T)

    w_router = jnp.zeros((d, V7X_LANES), F32).at[:, :N_GROUPS].set(w_rg)
    w_router = w_router.at[:, N_GROUPS:N_GROUPS + N_EXPERTS].set(w_re)
    w_router_hi = w_router.astype(BF16)
    w_router_lo = (w_router - w_router_hi.astype(F32)).astype(BF16)
    b_router = jnp.zeros((1, V7X_LANES), F32).at[0, :N_GROUPS].set(b_rg)
    b_router = b_router.at[0, N_GROUPS:N_GROUPS + N_EXPERTS].set(b_re.reshape(-1))

    xp3 = xp.reshape(bp * gp, ROW_GROUP, d)
    xs3 = xs.reshape(bs, ROW_GROUP, d)
    gb_p = min(ROWWISE_GROUPS, bp * gp)
    gb_s = min(ROWWISE_GROUPS, bs)

    def mixer(x3, mods, gb, attend):
        proj, ka, va, kb, vb = _proj(x3, g_mix, mods, w_in_bf, qn, kn, gb)
        o_sb, o_band = attend(proj)
        merged = _merge(o_sb, o_band, w_sb_bf, w_band_bf, proj, gb * ROW_GROUP)
        x1 = _outproj(merged, w_out_bf, x3, mods, gb)
        return x1, ka, va, kb, vb

    def attend_prompt(proj):
        p3 = proj.reshape(bp, sp_len, d_in)
        return (_sb_prompt(p3, u).reshape(bp * sp_len, W_HEADS),
                _band_prompt(p3, bias).reshape(bp * sp_len, W_HEADS))

    def attend_sample(proj):
        p3 = proj.reshape(bs, ts, d_in)
        def rows(cache):
            return cache.reshape(bs, -1, HEAD_DIM)

        o_sb = _sb_step(p3, rows(csk), rows(csv), u)
        o_band = _band_step(p3, rows(cbk), rows(cbv), bias_step)
        return o_sb.reshape(bs * ts, W_HEADS), o_band.reshape(bs * ts, W_HEADS)

    x1p, kap, vap, kbp, vbp = mixer(xp3, mod_p, min(MATMUL_GROUPS, bp * gp), attend_prompt)
    x1s, kas, vas, kbs, vbs = mixer(xs3, mod_s, min(MATMUL_GROUPS, bs), attend_sample)

    h_rows = 2 * (bp * sp_len + bs * ts)

    def take_rows(a, idx):
        return a.at[idx].get(mode="promise_in_bounds")

    def ffn(x1, mods, gb):
        ng = x1.shape[0]
        h2, route = _router(x1, g_ffn, mods, w_router_hi, w_router_lo, b_router, gb, h_rows)
        pos1, pos2, src_tok, tile_expert, n_valid = _dispatch(route)
        y_sorted = _moe(tile_expert, n_valid, take_rows(h2, src_tok), w_gate, w_up, w_down)
        return _final(x1, take_rows(y_sorted, pos1).reshape(ng, ROW_GROUP, d),
                      take_rows(y_sorted, pos2).reshape(ng, ROW_GROUP, d),
                      route.reshape(ng, ROW_GROUP, V7X_LANES), mods, gb)

    yp = ffn(x1p, mod_p, gb_p).reshape(bp, sp_len, d)
    ys = ffn(x1s, mod_s, gb_s).reshape(bs, ts, d)

    def heads(a, b):
        return a.reshape(b, -1, N_HEADS, HEAD_DIM)

    n_band = min(BAND_PAST, sp_len)
    return (yp, ys, heads(kap, bp), heads(vap, bp), heads(kbp, bp)[:, -n_band:], heads(vbp, bp)[:, -n_band:],
            heads(kas, bs), heads(vas, bs), heads(kbs, bs), heads(vbs, bs))


def kernel(x_prompt, x_sample, cache_sb_k, cache_sb_v, cache_band_k, cache_band_v, c_prompt, c_sample, norm_mix, norm_ffn, w_ada, b_ada, w_in, q_norm_band, k_norm_band, rel_bias_band, w_proj_sb, w_proj_band, w_out, w_router_group, b_router_group, w_router_expert, b_router_expert, w_gate, w_up, w_down):
    depth = w_in.shape[0]
    xp, xs = x_prompt, x_sample
    outs = [[] for _ in range(8)]
    for l in range(depth):
        res = _layer(xp, xs, cache_sb_k[l], cache_sb_v[l], cache_band_k[l], cache_band_v[l], c_prompt, c_sample,
                     norm_mix[l], norm_ffn[l], w_ada[l], b_ada[l], w_in[l], q_norm_band[l], k_norm_band[l],
                     rel_bias_band[l], w_proj_sb[l], w_proj_band[l], w_out[l], w_router_group[l],
                     b_router_group[l], w_router_expert[l], b_router_expert[l], w_gate[l], w_up[l], w_down[l])
        xp, xs = res[0], res[1]
        for acc, r in zip(outs, res[2:]):
            acc.append(r)
    return (xp, xs) + tuple(jnp.stack(o, axis=0) for o in outs)
```

```python
import functools

import jax
import jax.numpy as jnp
from jax import lax
from jax.experimental import pallas as pl
from jax.experimental.pallas import tpu as pltpu

F32 = jnp.float32
BF16 = jnp.bfloat16

EPS = 1e-6
HEAD_DIM = 128
N_HEADS = 8
W_HEADS = N_HEADS * HEAD_DIM
CHUNK = 64
BAND_LEFT_CHUNKS = 8
BAND_PAST = BAND_LEFT_CHUNKS * CHUNK
REL_CLIP = 128
N_GROUPS = 4
EXPERTS_PER_GROUP = 8
N_EXPERTS = N_GROUPS * EXPERTS_PER_GROUP
QK_SCALE = HEAD_DIM ** -0.5
NEG_BIG = -1e30

V7X_LANES = 128
V7X_VMEM_LIMIT = 56 * 1024 * 1024
ROW_GROUP = 64
MOE_TILE = 256
MATMUL_GROUPS = 16
MATMUL_TN = 1024
ROWWISE_GROUPS = 8


def _params(*sem):
    return pltpu.CompilerParams(dimension_semantics=sem, vmem_limit_bytes=V7X_VMEM_LIMIT)


def _sigmoid(x):
    return 1.0 / (1.0 + jnp.exp(-x))


def _dot(a, b):
    return jnp.dot(a, b, preferred_element_type=F32)


def _dot_nt(a, b):
    return lax.dot_general(a, b, (((1,), (1,)), ((), ())), preferred_element_type=F32)


def _ada_kernel(c_ref, w_ref, b_ref, o_ref):
    c = c_ref[...]
    a = (c * _sigmoid(c)).astype(BF16)
    o_ref[0] = _dot(a, w_ref[...].astype(BF16)) + b_ref[...]


def _ada(c_pad, w_ada, b_ada):
    r, d = c_pad.shape
    tn = min(1024, d)
    per = d // tn
    return pl.pallas_call(
        _ada_kernel,
        grid=(6 * per,),
        in_specs=[
            pl.BlockSpec((r, d), lambda j: (0, 0)),
            pl.BlockSpec((d, tn), lambda j: (0, j)),
            pl.BlockSpec((1, tn), lambda j: (0, j)),
        ],
        out_specs=pl.BlockSpec((1, r, tn), lambda j: (j // per, 0, j % per)),
        out_shape=jax.ShapeDtypeStruct((6, r, d), F32),
        compiler_params=_params("arbitrary"),
        name="ada",
    )(c_pad, w_ada, b_ada.reshape(1, 6 * d))


def _modulated_norm(x, g, scale, shift):
    ms = jnp.mean(x * x, axis=-1, keepdims=True)
    return (x * lax.rsqrt(ms + EPS) * g) * (1.0 + scale) + shift


def _head_norm(a, gain_ref):
    outs = []
    for hh in range(a.shape[1] // HEAD_DIM):
        blk = a[:, hh * HEAD_DIM:(hh + 1) * HEAD_DIM]
        ms = jnp.mean(blk * blk, axis=-1, keepdims=True)
        outs.append(blk * lax.rsqrt(ms + EPS) * gain_ref[:, hh * HEAD_DIM:(hh + 1) * HEAD_DIM])
    return jnp.concatenate(outs, axis=1)


def _proj_kernel(x_ref, g_ref, sc_ref, sh_ref, w_ref, qn_ref, kn_ref,
                 proj_ref, ka_ref, va_ref, kb_ref, vb_ref, h_scr, *, gb, per):
    j = pl.program_id(1)

    @pl.when(j == 0)
    def _():
        def body(s, carry):
            h = _modulated_norm(x_ref[s], g_ref[...], sc_ref[0, s], sh_ref[0, s])
            h_scr[pl.ds(pl.multiple_of(s * ROW_GROUP, ROW_GROUP), ROW_GROUP), :] = h.astype(BF16)
            return carry
        lax.fori_loop(0, gb, body, 0)

    acc = _dot(h_scr[...], w_ref[...])
    sec = j // per

    @pl.when(sec == 0)
    def _():
        proj_ref[...] = (acc * QK_SCALE).astype(BF16)

    @pl.when(sec == 1)
    def _():
        proj_ref[...] = acc.astype(BF16)
        ka_ref[...] = acc

    @pl.when(sec == 2)
    def _():
        proj_ref[...] = acc.astype(BF16)
        va_ref[...] = acc

    @pl.when(sec == 3)
    def _():
        proj_ref[...] = (_head_norm(acc, qn_ref) * QK_SCALE).astype(BF16)

    @pl.when(sec == 4)
    def _():
        n = _head_norm(acc, kn_ref)
        proj_ref[...] = n.astype(BF16)
        kb_ref[...] = n

    @pl.when(sec == 5)
    def _():
        proj_ref[...] = acc.astype(BF16)
        vb_ref[...] = acc

    @pl.when(sec >= 6)
    def _():
        proj_ref[...] = acc.astype(BF16)


MOD_SHIFT_M, MOD_SCALE_M, MOD_GATE_M, MOD_SHIFT_F, MOD_SCALE_F, MOD_GATE_F = range(6)


def _proj(x3, g, mods, w_in_bf, qn, kn, gb):
    ng, _, d = x3.shape
    d_in = w_in_bf.shape[1]
    m = ng * ROW_GROUP
    tm = gb * ROW_GROUP
    tn = 512
    per = W_HEADS // tn

    def sect(s):
        return lambda i, j: (i, jnp.clip(j - s * per, 0, per - 1))

    def mod_spec(which):
        return pl.BlockSpec((1, gb, 1, d), lambda i, j: (which, i, 0, 0))

    kv_shape = jax.ShapeDtypeStruct((m, W_HEADS), F32)
    return pl.pallas_call(
        functools.partial(_proj_kernel, gb=gb, per=per),
        grid=(ng // gb, d_in // tn),
        in_specs=[
            pl.BlockSpec((gb, ROW_GROUP, d), lambda i, j: (i, 0, 0)),
            pl.BlockSpec((1, d), lambda i, j: (0, 0)),
            mod_spec(MOD_SCALE_M), mod_spec(MOD_SHIFT_M),
            pl.BlockSpec((d, tn), lambda i, j: (0, j)),
            pl.BlockSpec((1, tn), lambda i, j: (0, jnp.clip(j - 3 * per, 0, per - 1))),
            pl.BlockSpec((1, tn), lambda i, j: (0, jnp.clip(j - 4 * per, 0, per - 1))),
        ],
        out_specs=[
            pl.BlockSpec((tm, tn), lambda i, j: (i, j)),
            pl.BlockSpec((tm, tn), sect(1)),
            pl.BlockSpec((tm, tn), sect(2)),
            pl.BlockSpec((tm, tn), sect(4)),
            pl.BlockSpec((tm, tn), sect(5)),
        ],
        out_shape=[jax.ShapeDtypeStruct((m, d_in), BF16), kv_shape, kv_shape, kv_shape, kv_shape],
        scratch_shapes=[pltpu.VMEM((tm, d), BF16)],
        compiler_params=_params("arbitrary", "arbitrary"),
        name="proj",
    )(x3, g, mods, mods, w_in_bf, qn, kn)


def _strict_lower_twice(n):
    row = lax.broadcasted_iota(jnp.int32, (2 * n, n), 0)
    col = lax.broadcasted_iota(jnp.int32, (2 * n, n), 1)
    row = jnp.where(row >= n, row - n, row)
    return jnp.where(row > col, 1.0, 0.0).astype(BF16)


def _sb_heads(loads, n_heads, u, carries, diagonal):
    qkv = [[load(h) for h in range(n_heads)] for load in loads]
    zs = [[_dot_nt(q, k) for q, k, _ in blk] for blk in qkv]
    if diagonal:
        row = lax.broadcasted_iota(jnp.int32, zs[0][0].shape, 0)
        col = lax.broadcasted_iota(jnp.int32, zs[0][0].shape, 1)
        valid = col < row
    sps, tails = [], []
    for blk in zs:
        sps.append([])
        tails.append([])
        for z in blk:
            sp = jnp.maximum(z, 0.0) + jnp.log(1.0 + jnp.exp(-jnp.abs(z)))
            if diagonal:
                sp = jnp.where(valid, sp, 0.0)
            hi = sp.astype(BF16)
            lo = (sp - hi.astype(F32)).astype(BF16)
            sps[-1].append(sp)
            tails[-1].append(_dot(jnp.concatenate([hi, lo], axis=1), u))
    pvs, new = [], []
    for h in range(n_heads):
        tq = zs[0][h].shape[0]
        carry = jnp.zeros((tq, 1), F32) if carries is None else carries[:, h * HEAD_DIM:h * HEAD_DIM + 1]
        pv = None
        for b in range(len(loads)):
            z, sp, tail = zs[b][h], sps[b][h], tails[b][h]
            loga = z - sp - tail - carry
            if diagonal:
                loga = jnp.where(valid, loga, NEG_BIG)
            term = _dot(jnp.exp(loga).astype(BF16), qkv[b][h][2])
            pv = term if pv is None else pv + term
            carry = carry + tail[:, 0:1] + sp[:, 0:1]
        pvs.append(pv)
        new.append(jnp.broadcast_to(carry, (tq, HEAD_DIM)))
    return jnp.concatenate(pvs, axis=1), jnp.concatenate(new, axis=1)


SB_T = 256
SB_HEADS_PER_STEP = 8
SB_STEP_BLOCKS = 1


def _head_slice(h):
    return slice(h * HEAD_DIM, (h + 1) * HEAD_DIM)


def _sb_prompt_kernel(q_ref, k_ref, v_ref, u_ref, o_ref, acc_scr, carry_scr):
    qi = pl.program_id(2)
    u = u_ref[...]

    def block(r, carries, diagonal):
        def load(h):
            hs = _head_slice(h)
            return q_ref[0, :, hs], k_ref[0, pl.ds(r, SB_T), hs], v_ref[0, pl.ds(r, SB_T), hs]
        return _sb_heads([load], SB_HEADS_PER_STEP, u, carries, diagonal)

    acc_scr[...], carry_scr[...] = block(pl.multiple_of(qi * SB_T, SB_T), None, True)

    def body(step, c):
        pv, carries = block(pl.multiple_of((qi - 1 - step) * SB_T, SB_T), carry_scr[...], False)
        acc_scr[...] += pv
        carry_scr[...] = carries
        return c

    lax.fori_loop(0, qi, body, 0)
    o_ref[0] = acc_scr[...].astype(BF16)


def _sb_prompt(proj3, u):
    b, s, _ = proj3.shape
    hp = SB_HEADS_PER_STEP
    w = hp * HEAD_DIM
    kcol, vcol = W_HEADS // w, 2 * W_HEADS // w
    return pl.pallas_call(
        _sb_prompt_kernel,
        grid=(b, N_HEADS // hp, s // SB_T),
        in_specs=[
            pl.BlockSpec((1, SB_T, w), lambda bi, h, qi: (bi, qi, h)),
            pl.BlockSpec((1, s, w), lambda bi, h, qi: (bi, 0, kcol + h)),
            pl.BlockSpec((1, s, w), lambda bi, h, qi: (bi, 0, vcol + h)),
            pl.BlockSpec((2 * SB_T, SB_T), lambda bi, h, qi: (0, 0)),
        ],
        out_specs=pl.BlockSpec((1, SB_T, w), lambda bi, h, qi: (bi, qi, h)),
        out_shape=jax.ShapeDtypeStruct((b, s, W_HEADS), BF16),
        scratch_shapes=[pltpu.VMEM((SB_T, w), F32), pltpu.VMEM((SB_T, w), F32)],
        compiler_params=_params("arbitrary", "arbitrary", "arbitrary"),
        name="sb_prompt",
    )(proj3, proj3, proj3, u)


def _sb_step_kernel(q_ref, kn_ref, vn_ref, ck_ref, cv_ref, u_ref, o_ref, acc_scr, carry_scr, *, tkb, t):
    j = pl.program_id(1)
    u = u_ref[...]

    @pl.when(j == 0)
    def _():
        def load(h):
            hs = _head_slice(h)
            return q_ref[0, :, hs], kn_ref[0, :, hs], vn_ref[0, :, hs]
        acc_scr[...], carry_scr[...] = _sb_heads([load], N_HEADS, _strict_lower_twice(t), None, True)

    def body(step, c):
        def loader(blk):
            r = pl.multiple_of((tkb - (step * SB_STEP_BLOCKS + blk + 1) * SB_T) * N_HEADS, SB_T * N_HEADS)

            def load(h):
                rows = pl.ds(r + h, SB_T, stride=N_HEADS)
                return (q_ref[0, :, _head_slice(h)], ck_ref[0, rows, :].astype(BF16),
                        cv_ref[0, rows, :].astype(BF16))
            return load

        pv, carries = _sb_heads([loader(blk) for blk in range(SB_STEP_BLOCKS)], N_HEADS, u, carry_scr[...], False)
        acc_scr[...] += pv
        carry_scr[...] = carries
        return c

    lax.fori_loop(0, tkb // (SB_T * SB_STEP_BLOCKS), body, 0)

    @pl.when(j == pl.num_programs(1) - 1)
    def _():
        o_ref[0] = acc_scr[...].astype(BF16)


def _sb_step(proj3, cache_k, cache_v, u):
    b, t, _ = proj3.shape
    p = cache_k.shape[1] // N_HEADS
    tkb = min(1024, p)
    nj = p // tkb
    cache_spec = pl.BlockSpec((1, tkb * N_HEADS, HEAD_DIM), lambda bi, j: (bi, nj - 1 - j, 0))
    return pl.pallas_call(
        functools.partial(_sb_step_kernel, tkb=tkb, t=t),
        grid=(b, nj),
        in_specs=[
            pl.BlockSpec((1, t, W_HEADS), lambda bi, j: (bi, 0, 0)),
            pl.BlockSpec((1, t, W_HEADS), lambda bi, j: (bi, 0, 1)),
            pl.BlockSpec((1, t, W_HEADS), lambda bi, j: (bi, 0, 2)),
            cache_spec, cache_spec,
            pl.BlockSpec((2 * SB_T, SB_T), lambda bi, j: (0, 0)),
        ],
        out_specs=pl.BlockSpec((1, t, W_HEADS), lambda bi, j: (bi, 0, 0)),
        out_shape=jax.ShapeDtypeStruct((b, t, W_HEADS), BF16),
        scratch_shapes=[pltpu.VMEM((t, W_HEADS), F32), pltpu.VMEM((t, W_HEADS), F32)],
        compiler_params=_params("arbitrary", "arbitrary"),
        name="sb_step",
    )(proj3, proj3, proj3, cache_k, cache_v, u)


BAND_QB = 4 * CHUNK
BAND_KB = 3


BIAS_NEAR = REL_CLIP // CHUNK + 1


def _bias_blocks_kernel(table_ref, o_ref):
    h = pl.program_id(0)
    i = lax.broadcasted_iota(jnp.int32, (CHUNK, CHUNK), 0)
    j = lax.broadcasted_iota(jnp.int32, (CHUNK, CHUNK), 1)
    for d in range(BIAS_NEAR):
        idx = jnp.clip(i - j + d * CHUNK, -REL_CLIP, REL_CLIP) + REL_CLIP

        def body(r, acc, idx=idx):
            return jnp.where(idx == r, table_ref[h, r], acc)

        lo = max(d * CHUNK - (CHUNK - 1), -REL_CLIP) + REL_CLIP
        hi = min(d * CHUNK + (CHUNK - 1), REL_CLIP) + REL_CLIP
        o_ref[0, d] = lax.fori_loop(lo, hi + 1, body, jnp.zeros((CHUNK, CHUNK), F32))


def _bias_blocks(table):
    nh = table.shape[0]
    return pl.pallas_call(
        _bias_blocks_kernel,
        grid=(nh,),
        in_specs=[pl.BlockSpec(memory_space=pltpu.SMEM)],
        out_specs=pl.BlockSpec((1, BIAS_NEAR, CHUNK, CHUNK), lambda h: (h, 0, 0, 0)),
        out_shape=jax.ShapeDtypeStruct((nh, BIAS_NEAR, CHUNK, CHUNK), F32),
        compiler_params=_params("arbitrary"),
        name="bias_blocks",
    )(table.astype(F32))


def _chunk_bias(blocks, far, dist):
    if dist < 0 or dist > BAND_LEFT_CHUNKS:
        return jnp.full(far.shape, NEG_BIG, F32)
    return blocks[:, dist] if dist < BIAS_NEAR else far


def _band_bias(blocks, table):
    far = jnp.broadcast_to(table[:, -1].astype(F32)[:, None, None], (table.shape[0], CHUNK, CHUNK))
    qc, kc = BAND_QB // CHUNK, BAND_KB * BAND_QB // CHUNK
    rows = [jnp.concatenate([_chunk_bias(blocks, far, r - c + BAND_LEFT_CHUNKS) for c in range(kc)], axis=2)
            for r in range(qc)]
    return jnp.concatenate(rows, axis=1)


def _band_step_bias(blocks, table, lb):
    far = jnp.broadcast_to(table[:, -1].astype(F32)[:, None, None], (table.shape[0], CHUNK, CHUNK))
    nc = lb // CHUNK
    return jnp.concatenate([blocks[:, nc - c] if nc - c < BIAS_NEAR else far for c in range(nc + 1)], axis=2)


def _softmax_pv_heads(scores, values):
    ps, dens = [], []
    for sc in scores:
        m = sc[0].max(axis=1, keepdims=True)
        for s in sc[1:]:
            m = jnp.maximum(m, s.max(axis=1, keepdims=True))
        p = [jnp.exp(s - m) for s in sc]
        den = p[0].sum(axis=1, keepdims=True)
        for pc in p[1:]:
            den = den + pc.sum(axis=1, keepdims=True)
        ps.append(p)
        dens.append(den)
    outs = []
    for p, den, vals in zip(ps, dens, values):
        num = _dot(p[0].astype(BF16), vals[0])
        for pc, v in zip(p[1:], vals[1:]):
            num = num + _dot(pc.astype(BF16), v)
        outs.append(num / den)
    return jnp.concatenate(outs, axis=1)


def _band_prompt_kernel(q_ref, k0_ref, k1_ref, k2_ref, v0_ref, v1_ref, v2_ref, bias_ref, o_ref):
    qi = pl.program_id(1)
    scores, values = [], []
    for h in range(N_HEADS):
        hs = _head_slice(h)
        q = q_ref[0, :, hs]
        sc = []
        for c, k_ref in enumerate((k0_ref, k1_ref, k2_ref)):
            s = _dot_nt(q, k_ref[0, :, hs]) + bias_ref[h, :, c * BAND_QB:(c + 1) * BAND_QB]
            if c < BAND_KB - 1:
                s = jnp.where(qi + c >= BAND_KB - 1, s, NEG_BIG)
            sc.append(s)
        scores.append(sc)
        values.append([v_ref[0, :, hs] for v_ref in (v0_ref, v1_ref, v2_ref)])
    o_ref[0] = _softmax_pv_heads(scores, values).astype(BF16)


def _band_prompt(proj3, bias):
    b, s, _ = proj3.shape
    qcol, kcol, vcol = 3, 4, 5

    def kv_spec(col, back):
        return pl.BlockSpec((1, BAND_QB, W_HEADS), lambda bi, qi: (bi, jnp.maximum(qi - back, 0), col))

    return pl.pallas_call(
        _band_prompt_kernel,
        grid=(b, s // BAND_QB),
        in_specs=[
            pl.BlockSpec((1, BAND_QB, W_HEADS), lambda bi, qi: (bi, qi, qcol)),
            kv_spec(kcol, 2), kv_spec(kcol, 1), kv_spec(kcol, 0),
            kv_spec(vcol, 2), kv_spec(vcol, 1), kv_spec(vcol, 0),
            pl.BlockSpec((N_HEADS, BAND_QB, BAND_KB * BAND_QB), lambda bi, qi: (0, 0, 0)),
        ],
        out_specs=pl.BlockSpec((1, BAND_QB, W_HEADS), lambda bi, qi: (bi, qi, 0)),
        out_shape=jax.ShapeDtypeStruct((b, s, W_HEADS), BF16),
        compiler_params=_params("arbitrary", "arbitrary"),
        name="band_prompt",
    )(proj3, proj3, proj3, proj3, proj3, proj3, proj3, bias)


def _band_step_kernel(q_ref, kn_ref, vn_ref, ck_ref, cv_ref, bias_ref, o_ref, *, lb):
    scores, values = [], []
    for h in range(N_HEADS):
        hs = _head_slice(h)
        q = q_ref[0, :, hs]
        rows = pl.ds(h, lb, stride=N_HEADS)
        scores.append([_dot_nt(q, ck_ref[0, rows, :].astype(BF16)) + bias_ref[h, :, :lb],
                       _dot_nt(q, kn_ref[0, :, hs]) + bias_ref[h, :, lb:]])
        values.append([cv_ref[0, rows, :].astype(BF16), vn_ref[0, :, hs]])
    o_ref[0] = _softmax_pv_heads(scores, values).astype(BF16)


def _band_step(proj3, cache_k, cache_v, bias):
    b, t, _ = proj3.shape
    lb = cache_k.shape[1] // N_HEADS
    cache_spec = pl.BlockSpec((1, lb * N_HEADS, HEAD_DIM), lambda bi: (bi, 0, 0))
    return pl.pallas_call(
        functools.partial(_band_step_kernel, lb=lb),
        grid=(b,),
        in_specs=[
            pl.BlockSpec((1, t, W_HEADS), lambda bi: (bi, 0, 3)),
            pl.BlockSpec((1, t, W_HEADS), lambda bi: (bi, 0, 4)),
            pl.BlockSpec((1, t, W_HEADS), lambda bi: (bi, 0, 5)),
            cache_spec, cache_spec,
            pl.BlockSpec((N_HEADS, t, lb + t), lambda bi: (0, 0, 0)),
        ],
        out_specs=pl.BlockSpec((1, t, W_HEADS), lambda bi: (bi, 0, 0)),
        out_shape=jax.ShapeDtypeStruct((b, t, W_HEADS), BF16),
        compiler_params=_params("arbitrary"),
        name="band_step",
    )(proj3, proj3, proj3, cache_k, cache_v, bias)


def _merge_kernel(osb_ref, obd_ref, wsb_ref, wbd_ref, ga_ref, gb_ref, o_ref):
    a = _dot(osb_ref[...], wsb_ref[...])
    b = _dot(obd_ref[...], wbd_ref[...])
    merged = _sigmoid(ga_ref[...].astype(F32)) * a + _sigmoid(gb_ref[...].astype(F32)) * b
    o_ref[...] = merged.astype(BF16)


def _merge(o_sb, o_band, w_sb_bf, w_band_bf, proj, tm):
    m = o_sb.shape[0]
    d = w_sb_bf.shape[1]
    tn = min(MATMUL_TN, d)
    ga_col = 6 * W_HEADS // tn
    gb_col = ga_col + d // tn
    return pl.pallas_call(
        _merge_kernel,
        grid=(m // tm, d // tn),
        in_specs=[
            pl.BlockSpec((tm, W_HEADS), lambda i, j: (i, 0)),
            pl.BlockSpec((tm, W_HEADS), lambda i, j: (i, 0)),
            pl.BlockSpec((W_HEADS, tn), lambda i, j: (0, j)),
            pl.BlockSpec((W_HEADS, tn), lambda i, j: (0, j)),
            pl.BlockSpec((tm, tn), lambda i, j: (i, ga_col + j)),
            pl.BlockSpec((tm, tn), lambda i, j: (i, gb_col + j)),
        ],
        out_specs=pl.BlockSpec((tm, tn), lambda i, j: (i, j)),
        out_shape=jax.ShapeDtypeStruct((m, d), BF16),
        compiler_params=_params("arbitrary", "arbitrary"),
        name="merge",
    )(o_sb, o_band, w_sb_bf, w_band_bf, proj, proj)


def _outproj_kernel(m_ref, w_ref, x_ref, gate_ref, o_ref, *, gb):
    acc = _dot(m_ref[...], w_ref[...])
    acc = acc.reshape(gb, ROW_GROUP, acc.shape[1])
    o_ref[...] = x_ref[...] + gate_ref[0] * acc


def _outproj(merged, w_out_bf, x3, mods, gb):
    ng, _, d = x3.shape
    tm = gb * ROW_GROUP
    tn = min(MATMUL_TN, d)
    return pl.pallas_call(
        functools.partial(_outproj_kernel, gb=gb),
        grid=(ng // gb, d // tn),
        in_specs=[
            pl.BlockSpec((tm, d), lambda i, j: (i, 0)),
            pl.BlockSpec((d, tn), lambda i, j: (0, j)),
            pl.BlockSpec((gb, ROW_GROUP, tn), lambda i, j: (i, 0, j)),
            pl.BlockSpec((1, gb, 1, tn), lambda i, j: (MOD_GATE_M, i, 0, j)),
        ],
        out_specs=pl.BlockSpec((gb, ROW_GROUP, tn), lambda i, j: (i, 0, j)),
        out_shape=jax.ShapeDtypeStruct(x3.shape, F32),
        compiler_params=_params("arbitrary", "arbitrary"),
        name="outproj",
    )(merged, w_out_bf, x3, mods)


HIGH_HALF = -65536
GATHER_ROWS = 256


def _tile_sublanes(d):
    assert d % (2 * V7X_LANES) == 0
    return d // (2 * V7X_LANES)


def _store_token_tiles(ref, first_row, x):
    n, d = x.shape
    s_per = _tile_sublanes(d)
    for s in range(s_per):
        lo = x[:, (2 * s) * V7X_LANES:(2 * s + 1) * V7X_LANES].astype(BF16).astype(F32)
        hi = x[:, (2 * s + 1) * V7X_LANES:(2 * s + 2) * V7X_LANES].astype(BF16).astype(F32)
        word = lax.shift_right_logical(pltpu.bitcast(lo, jnp.int32), 16) | (pltpu.bitcast(hi, jnp.int32) & HIGH_HALF)
        ref[pl.ds(first_row * s_per + s, n, stride=s_per), :] = word


def _load_token_tiles(ref, n, d, dtype):
    s_per = _tile_sublanes(d)
    pieces = []
    for s in range(s_per):
        word = ref[pl.ds(s, n, stride=s_per), :]
        lo = pltpu.bitcast(lax.shift_left(word, 16), F32)
        hi = pltpu.bitcast(word & HIGH_HALF, F32)
        pieces += [lo.astype(dtype), hi.astype(dtype)]
    return jnp.concatenate(pieces, axis=1)


def _gather_kernel(idx_ref, table_ref, out_ref, sems, *, rows, s_per):
    c = pl.program_id(0)

    def issue(i, carry):
        dst = c * rows + i
        pltpu.make_async_copy(table_ref.at[pl.ds(idx_ref[dst] * s_per, s_per)],
                              out_ref.at[pl.ds(dst * s_per, s_per)], sems.at[c % 2]).start()
        return carry

    lax.fori_loop(0, rows, issue, 0, unroll=8)

    def wait_chunk(chunk):
        block = out_ref.at[pl.ds(chunk * rows * s_per, rows * s_per)]
        pltpu.make_async_copy(block, block, sems.at[chunk % 2]).wait()

    @pl.when(c > 0)
    def _():
        wait_chunk(c - 1)

    @pl.when(c == pl.num_programs(0) - 1)
    def _():
        wait_chunk(c)


def _gather_tokens(table, idx, s_per):
    n = idx.shape[0]
    rows = min(GATHER_ROWS, n)
    assert n % rows == 0
    grid_spec = pltpu.PrefetchScalarGridSpec(
        num_scalar_prefetch=1,
        grid=(n // rows,),
        in_specs=[pl.BlockSpec(memory_space=pl.ANY)],
        out_specs=pl.BlockSpec(memory_space=pl.ANY),
        scratch_shapes=[pltpu.SemaphoreType.DMA((2,))],
    )
    return pl.pallas_call(
        functools.partial(_gather_kernel, rows=rows, s_per=s_per),
        grid_spec=grid_spec,
        out_shape=jax.ShapeDtypeStruct((n * s_per, V7X_LANES), jnp.int32),
        compiler_params=_params("arbitrary"),
        name="gather_tokens",
    )(idx, table)


ROUTE_E1, ROUTE_E2, ROUTE_W1, ROUTE_W2 = 0, 1, 2, 3


def _router_kernel(x_ref, g_ref, sc_ref, sh_ref, whi_ref, wlo_ref, b_ref, h_ref, route_ref, hi_scr, lo_scr, *, gb):
    def body(s, carry):
        h = _modulated_norm(x_ref[s], g_ref[...], sc_ref[0, s], sh_ref[0, s])
        hi = h.astype(BF16)
        first = pl.multiple_of(s * ROW_GROUP, ROW_GROUP)
        _store_token_tiles(h_ref, first, h)
        hi_scr[pl.ds(first, ROW_GROUP), :] = hi
        lo_scr[pl.ds(first, ROW_GROUP), :] = (h - hi.astype(F32)).astype(BF16)
        return carry
    lax.fori_loop(0, gb, body, 0)

    hi = hi_scr[...]
    logits = _dot(hi, whi_ref[...]) + _dot(hi, wlo_ref[...]) + _dot(lo_scr[...], whi_ref[...]) + b_ref[...]
    lane = lax.broadcasted_iota(jnp.int32, logits.shape, 1)
    ninf = -jnp.inf

    lane_f = lane.astype(F32)

    def first_max(vals):
        mx = vals.max(axis=1, keepdims=True)
        idx = jnp.where(vals == mx, lane_f, float(V7X_LANES)).min(axis=1, keepdims=True)
        return mx, idx

    gl = jnp.where(lane < N_GROUPS, logits, ninf)
    gmax, gidx = first_max(gl)
    g_weight = 1.0 / jnp.exp(gl - gmax).sum(axis=1, keepdims=True)
    lo_lane = N_GROUPS + EXPERTS_PER_GROUP * gidx
    el = jnp.where(lane_f >= lo_lane, jnp.where(lane_f < lo_lane + EXPERTS_PER_GROUP, logits, ninf), ninf)
    m1, i1 = first_max(el)
    m2, i2 = first_max(jnp.where(lane_f == i1, ninf, el))
    e21 = jnp.exp(m2 - m1)
    p1 = 1.0 / (1.0 + e21)
    p2 = e21 / (1.0 + e21)
    route = jnp.where(lane == ROUTE_E1, i1 - N_GROUPS,
                      jnp.where(lane == ROUTE_E2, i2 - N_GROUPS,
                                jnp.where(lane == ROUTE_W1, g_weight * p1,
                                          jnp.where(lane == ROUTE_W2, g_weight * p2, 0.0))))
    route_ref[...] = route


def _router(x3, g, mods, w_hi, w_lo, bias, gb):
    ng, _, d = x3.shape
    m = ng * ROW_GROUP
    tm = gb * ROW_GROUP
    s_per = _tile_sublanes(d)

    def mod_spec(which):
        return pl.BlockSpec((1, gb, 1, d), lambda i: (which, i, 0, 0))

    w_spec = pl.BlockSpec((d, V7X_LANES), lambda i: (0, 0))
    return pl.pallas_call(
        functools.partial(_router_kernel, gb=gb),
        grid=(ng // gb,),
        in_specs=[
            pl.BlockSpec((gb, ROW_GROUP, d), lambda i: (i, 0, 0)),
            pl.BlockSpec((1, d), lambda i: (0, 0)),
            mod_spec(MOD_SCALE_F), mod_spec(MOD_SHIFT_F), w_spec, w_spec,
            pl.BlockSpec((1, V7X_LANES), lambda i: (0, 0)),
        ],
        out_specs=[
            pl.BlockSpec((tm * s_per, V7X_LANES), lambda i: (i, 0)),
            pl.BlockSpec((tm, V7X_LANES), lambda i: (i, 0)),
        ],
        out_shape=[jax.ShapeDtypeStruct((m * s_per, V7X_LANES), jnp.int32),
                   jax.ShapeDtypeStruct((m, V7X_LANES), F32)],
        scratch_shapes=[pltpu.VMEM((tm, d), BF16), pltpu.VMEM((tm, d), BF16)],
        compiler_params=_params("arbitrary"),
        name="router",
    )(x3, g, mods, mods, w_hi, w_lo, bias)


def _moe_kernel(te_ref, nv_ref, x_ref, wg_ref, wu_ref, wd_ref, o_ref, wg_scr, wu_scr, wd_scr):
    t = pl.program_id(0)
    changed = te_ref[t] != te_ref[jnp.maximum(t - 1, 0)]

    @pl.when((t == 0) | changed)
    def _():
        wg_scr[...] = wg_ref[0].astype(BF16)
        wu_scr[...] = wu_ref[0].astype(BF16)
        wd_scr[...] = wd_ref[0].astype(BF16)

    @pl.when(t < nv_ref[0])
    def _():
        x = _load_token_tiles(x_ref, MOE_TILE, wg_scr.shape[0], BF16)
        g = _dot(x, wg_scr[...])
        u = _dot(x, wu_scr[...])
        hidden = (g * _sigmoid(g)) * u
        _store_token_tiles(o_ref, 0, _dot(hidden.astype(BF16), wd_scr[...]))

    @pl.when(t >= nv_ref[0])
    def _():
        o_ref[...] = jnp.zeros(o_ref.shape, jnp.int32)


def _moe(tile_expert, n_valid, xs, w_gate, w_up, w_down):
    _, d, f = w_gate.shape
    s_per = _tile_sublanes(d)
    n_tiles = xs.shape[0] // (MOE_TILE * s_per)

    def x_map(t, te, nv):
        return (jnp.minimum(t, nv[0] - 1), 0)

    grid_spec = pltpu.PrefetchScalarGridSpec(
        num_scalar_prefetch=2,
        grid=(n_tiles,),
        in_specs=[
            pl.BlockSpec((MOE_TILE * s_per, V7X_LANES), x_map),
            pl.BlockSpec((1, d, f), lambda t, te, nv: (te[t], 0, 0)),
            pl.BlockSpec((1, d, f), lambda t, te, nv: (te[t], 0, 0)),
            pl.BlockSpec((1, f, d), lambda t, te, nv: (te[t], 0, 0)),
        ],
        out_specs=pl.BlockSpec((MOE_TILE * s_per, V7X_LANES), lambda t, te, nv: (t, 0)),
        scratch_shapes=[pltpu.VMEM((d, f), BF16), pltpu.VMEM((d, f), BF16), pltpu.VMEM((f, d), BF16)],
    )
    return pl.pallas_call(
        _moe_kernel,
        grid_spec=grid_spec,
        out_shape=jax.ShapeDtypeStruct(xs.shape, jnp.int32),
        compiler_params=_params("arbitrary"),
        name="moe",
    )(tile_expert, n_valid, xs, w_gate, w_up, w_down)


def _dispatch(route):
    n = route.shape[0]
    e = jnp.concatenate([route[:, ROUTE_E1], route[:, ROUTE_E2]]).astype(jnp.int32)
    tok = jnp.concatenate([jnp.arange(n, dtype=jnp.int32)] * 2)
    onehot = (e[:, None] == jnp.arange(N_EXPERTS, dtype=jnp.int32)[None, :]).astype(jnp.int32)
    before = jnp.cumsum(onehot, axis=0) - onehot
    rank = jnp.sum(before * onehot, axis=1)
    counts = jnp.sum(onehot, axis=0)
    padded = ((counts + MOE_TILE - 1) // MOE_TILE) * MOE_TILE
    ends = jnp.cumsum(padded)
    pos = (ends - padded)[e] + rank
    n_tiles = -(-(2 * n + N_EXPERTS * (MOE_TILE - 1)) // (4 * MOE_TILE)) * 4
    tile_start = jnp.arange(n_tiles, dtype=jnp.int32) * MOE_TILE
    n_valid = (ends[-1] // MOE_TILE).astype(jnp.int32)
    tile_expert = jnp.sum((ends[None, :] <= tile_start[:, None]).astype(jnp.int32), axis=1)
    last_expert = tile_expert[jnp.maximum(n_valid - 1, 0)]
    tile_expert = jnp.where(tile_start < ends[-1], tile_expert, last_expert)
    src_tok = jnp.zeros((n_tiles * MOE_TILE,), jnp.int32).at[pos].set(tok)
    return pos[:n], pos[n:], src_tok, tile_expert, n_valid.reshape(1)


def _final_kernel(x_ref, y1_ref, y2_ref, route_ref, gate_ref, o_ref):
    gb, rg, d = x_ref.shape
    r = route_ref[...]
    y = (r[:, ROUTE_W1:ROUTE_W1 + 1] * _load_token_tiles(y1_ref, gb * rg, d, F32)
         + r[:, ROUTE_W2:ROUTE_W2 + 1] * _load_token_tiles(y2_ref, gb * rg, d, F32))
    o_ref[...] = x_ref[...] + gate_ref[0] * y.reshape(gb, rg, d)


def _final(x3, y1, y2, route, mods, gb):
    ng, _, d = x3.shape
    tm = gb * ROW_GROUP
    s_per = _tile_sublanes(d)
    blk = pl.BlockSpec((gb, ROW_GROUP, d), lambda i: (i, 0, 0))
    tiles = pl.BlockSpec((tm * s_per, V7X_LANES), lambda i: (i, 0))
    return pl.pallas_call(
        _final_kernel,
        grid=(ng // gb,),
        in_specs=[
            blk, tiles, tiles,
            pl.BlockSpec((tm, V7X_LANES), lambda i: (i, 0)),
            pl.BlockSpec((1, gb, 1, d), lambda i: (MOD_GATE_F, i, 0, 0)),
        ],
        out_specs=blk,
        out_shape=jax.ShapeDtypeStruct(x3.shape, F32),
        compiler_params=_params("arbitrary"),
        name="final",
    )(x3, y1, y2, route, mods)


def _layer(xp, xs, csk, csv, cbk, cbv, c_prompt, c_sample, norm_mix, norm_ffn, w_ada, b_ada, w_in, q_norm,
           k_norm, rel_table, w_proj_sb, w_proj_band, w_out, w_rg, b_rg, w_re, b_re, w_gate, w_up, w_down):
    bp, sp_len, d = xp.shape
    bs, ts, _ = xs.shape
    d_in = w_in.shape[1]
    assert sp_len % BAND_QB == 0 and ts == ROW_GROUP and sp_len % ROW_GROUP == 0

    n_c = bp + bs
    c_pad = jnp.concatenate([c_prompt, c_sample, jnp.zeros((-n_c % 8, d), F32)], axis=0)
    mod = _ada(c_pad, w_ada, b_ada)
    gp = sp_len // ROW_GROUP
    mod_p = jnp.repeat(mod[:, :bp], gp, axis=1)[:, :, None, :]
    mod_s = mod[:, bp:n_c][:, :, None, :]

    w_in_bf = w_in.astype(BF16)
    w_sb_bf = w_proj_sb.astype(BF16)
    w_band_bf = w_proj_band.astype(BF16)
    w_out_bf = w_out.astype(BF16)
    qn = q_norm.reshape(1, W_HEADS)
    kn = k_norm.reshape(1, W_HEADS)
    g_mix = norm_mix.reshape(1, d)
    g_ffn = norm_ffn.reshape(1, d)
    lb = cbk.shape[1]
    assert lb % CHUNK == 0 and ts == CHUNK
    bias_blocks = _bias_blocks(rel_table)
    bias = _band_bias(bias_blocks, rel_table)
    bias_step = _band_step_bias(bias_blocks, rel_table, lb)
    u = _strict_lower_twice(SB_T)

    w_router = jnp.zeros((d, V7X_LANES), F32).at[:, :N_GROUPS].set(w_rg)
    w_router = w_router.at[:, N_GROUPS:N_GROUPS + N_EXPERTS].set(w_re)
    w_router_hi = w_router.astype(BF16)
    w_router_lo = (w_router - w_router_hi.astype(F32)).astype(BF16)
    b_router = jnp.zeros((1, V7X_LANES), F32).at[0, :N_GROUPS].set(b_rg)
    b_router = b_router.at[0, N_GROUPS:N_GROUPS + N_EXPERTS].set(b_re.reshape(-1))

    xp3 = xp.reshape(bp * gp, ROW_GROUP, d)
    xs3 = xs.reshape(bs, ROW_GROUP, d)
    gb_p = min(ROWWISE_GROUPS, bp * gp)
    gb_s = min(ROWWISE_GROUPS, bs)

    def mixer(x3, mods, gb, attend):
        proj, ka, va, kb, vb = _proj(x3, g_mix, mods, w_in_bf, qn, kn, gb)
        o_sb, o_band = attend(proj)
        merged = _merge(o_sb, o_band, w_sb_bf, w_band_bf, proj, gb * ROW_GROUP)
        x1 = _outproj(merged, w_out_bf, x3, mods, gb)
        return x1, ka, va, kb, vb

    def attend_prompt(proj):
        p3 = proj.reshape(bp, sp_len, d_in)
        return (_sb_prompt(p3, u).reshape(bp * sp_len, W_HEADS),
                _band_prompt(p3, bias).reshape(bp * sp_len, W_HEADS))

    def attend_sample(proj):
        p3 = proj.reshape(bs, ts, d_in)
        def rows(cache):
            return cache.reshape(bs, -1, HEAD_DIM)

        o_sb = _sb_step(p3, rows(csk), rows(csv), u)
        o_band = _band_step(p3, rows(cbk), rows(cbv), bias_step)
        return o_sb.reshape(bs * ts, W_HEADS), o_band.reshape(bs * ts, W_HEADS)

    x1p, kap, vap, kbp, vbp = mixer(xp3, mod_p, min(MATMUL_GROUPS, bp * gp), attend_prompt)
    x1s, kas, vas, kbs, vbs = mixer(xs3, mod_s, min(MATMUL_GROUPS, bs), attend_sample)

    s_per = _tile_sublanes(d)

    def ffn(x1, mods, gb):
        h2, route = _router(x1, g_ffn, mods, w_router_hi, w_router_lo, b_router, gb)
        pos1, pos2, src_tok, tile_expert, n_valid = _dispatch(route)
        y_sorted = _moe(tile_expert, n_valid, _gather_tokens(h2, src_tok, s_per), w_gate, w_up, w_down)
        return _final(x1, _gather_tokens(y_sorted, pos1, s_per), _gather_tokens(y_sorted, pos2, s_per),
                      route, mods, gb)

    yp = ffn(x1p, mod_p, gb_p).reshape(bp, sp_len, d)
    ys = ffn(x1s, mod_s, gb_s).reshape(bs, ts, d)

    def heads(a, b):
        return a.reshape(b, -1, N_HEADS, HEAD_DIM)

    n_band = min(BAND_PAST, sp_len)
    return (yp, ys, heads(kap, bp), heads(vap, bp), heads(kbp, bp)[:, -n_band:], heads(vbp, bp)[:, -n_band:],
            heads(kas, bs), heads(vas, bs), heads(kbs, bs), heads(vbs, bs))


def kernel(x_prompt, x_sample, cache_sb_k, cache_sb_v, cache_band_k, cache_band_v, c_prompt, c_sample, norm_mix, norm_ffn, w_ada, b_ada, w_in, q_norm_band, k_norm_band, rel_bias_band, w_proj_sb, w_proj_band, w_out, w_router_group, b_router_group, w_router_expert, b_router_expert, w_gate, w_up, w_down):
    depth = w_in.shape[0]
    xp, xs = x_prompt, x_sample
    outs = [[] for _ in range(8)]
    for l in range(depth):
        res = _layer(xp, xs, cache_sb_k[l], cache_sb_v[l], cache_band_k[l], cache_band_v[l], c_prompt, c_sample,
                     norm_mix[l], norm_ffn[l], w_ada[l], b_ada[l], w_in[l], q_norm_band[l], k_norm_band[l],
                     rel_bias_band[l], w_proj_sb[l], w_proj_band[l], w_out[l], w_router_group[l],
                     b_router_group[l], w_router_expert[l], b_router_expert[l], w_gate[l], w_up[l], w_down[l])
        xp, xs = res[0], res[1]
        for acc, r in zip(outs, res[2:]):
            acc.append(r)
    return (xp, xs) + tuple(jnp.stack(o, axis=0) for o in outs)
```

```python
import functools

import jax
import jax.numpy as jnp
from jax import lax
from jax.experimental import pallas as pl
from jax.experimental.pallas import tpu as pltpu

F32 = jnp.float32
BF16 = jnp.bfloat16

EPS = 1e-6
HEAD_DIM = 128
N_HEADS = 8
W_HEADS = N_HEADS * HEAD_DIM
CHUNK = 64
BAND_LEFT_CHUNKS = 8
BAND_PAST = BAND_LEFT_CHUNKS * CHUNK
REL_CLIP = 128
N_GROUPS = 4
EXPERTS_PER_GROUP = 8
N_EXPERTS = N_GROUPS * EXPERTS_PER_GROUP
QK_SCALE = HEAD_DIM ** -0.5
NEG_BIG = -1e30

V7X_LANES = 128
V7X_VMEM_LIMIT = 56 * 1024 * 1024
ROW_GROUP = 64
MOE_TILE = 256
MATMUL_GROUPS = 16
MATMUL_TN = 1024
ROWWISE_GROUPS = 8
FINAL_GROUPS = 4


def _params(*sem):
    return pltpu.CompilerParams(dimension_semantics=sem, vmem_limit_bytes=V7X_VMEM_LIMIT)


def _sigmoid(x):
    return 1.0 / (1.0 + jnp.exp(-x))


def _dot(a, b):
    return jnp.dot(a, b, preferred_element_type=F32)


def _dot_nt(a, b):
    return lax.dot_general(a, b, (((1,), (1,)), ((), ())), preferred_element_type=F32)


def _ada_kernel(c_ref, w_ref, b_ref, o_ref):
    c = c_ref[...]
    a = (c * _sigmoid(c)).astype(BF16)
    o_ref[0] = _dot(a, w_ref[...].astype(BF16)) + b_ref[...]


def _ada(c_pad, w_ada, b_ada):
    r, d = c_pad.shape
    tn = min(1024, d)
    per = d // tn
    return pl.pallas_call(
        _ada_kernel,
        grid=(6 * per,),
        in_specs=[
            pl.BlockSpec((r, d), lambda j: (0, 0)),
            pl.BlockSpec((d, tn), lambda j: (0, j)),
            pl.BlockSpec((1, tn), lambda j: (0, j)),
        ],
        out_specs=pl.BlockSpec((1, r, tn), lambda j: (j // per, 0, j % per)),
        out_shape=jax.ShapeDtypeStruct((6, r, d), F32),
        compiler_params=_params("arbitrary"),
        name="ada",
    )(c_pad, w_ada, b_ada.reshape(1, 6 * d))


def _modulated_norm(x, g, scale, shift):
    ms = jnp.mean(x * x, axis=-1, keepdims=True)
    return (x * lax.rsqrt(ms + EPS) * g) * (1.0 + scale) + shift


def _head_norm(a, gain_ref):
    outs = []
    for hh in range(a.shape[1] // HEAD_DIM):
        blk = a[:, hh * HEAD_DIM:(hh + 1) * HEAD_DIM]
        ms = jnp.mean(blk * blk, axis=-1, keepdims=True)
        outs.append(blk * lax.rsqrt(ms + EPS) * gain_ref[:, hh * HEAD_DIM:(hh + 1) * HEAD_DIM])
    return jnp.concatenate(outs, axis=1)


def _proj_kernel(x_ref, g_ref, sc_ref, sh_ref, w_ref, qn_ref, kn_ref,
                 proj_ref, ka_ref, va_ref, kb_ref, vb_ref, h_scr, *, gb, per):
    j = pl.program_id(1)

    @pl.when(j == 0)
    def _():
        def body(s, carry):
            h = _modulated_norm(x_ref[s], g_ref[...], sc_ref[0, s], sh_ref[0, s])
            h_scr[pl.ds(pl.multiple_of(s * ROW_GROUP, ROW_GROUP), ROW_GROUP), :] = h.astype(BF16)
            return carry
        lax.fori_loop(0, gb, body, 0)

    acc = _dot(h_scr[...], w_ref[...])
    sec = j // per

    @pl.when(sec == 0)
    def _():
        proj_ref[...] = (acc * QK_SCALE).astype(BF16)

    @pl.when(sec == 1)
    def _():
        proj_ref[...] = acc.astype(BF16)
        ka_ref[...] = acc

    @pl.when(sec == 2)
    def _():
        proj_ref[...] = acc.astype(BF16)
        va_ref[...] = acc

    @pl.when(sec == 3)
    def _():
        proj_ref[...] = (_head_norm(acc, qn_ref) * QK_SCALE).astype(BF16)

    @pl.when(sec == 4)
    def _():
        n = _head_norm(acc, kn_ref)
        proj_ref[...] = n.astype(BF16)
        kb_ref[...] = n

    @pl.when(sec == 5)
    def _():
        proj_ref[...] = acc.astype(BF16)
        vb_ref[...] = acc

    @pl.when(sec >= 6)
    def _():
        proj_ref[...] = acc.astype(BF16)


MOD_SHIFT_M, MOD_SCALE_M, MOD_GATE_M, MOD_SHIFT_F, MOD_SCALE_F, MOD_GATE_F = range(6)


def _proj(x3, g, mods, w_in_bf, qn, kn, gb):
    ng, _, d = x3.shape
    d_in = w_in_bf.shape[1]
    m = ng * ROW_GROUP
    tm = gb * ROW_GROUP
    tn = 512
    per = W_HEADS // tn

    def sect(s):
        return lambda i, j: (i, jnp.clip(j - s * per, 0, per - 1))

    def mod_spec(which):
        return pl.BlockSpec((1, gb, 1, d), lambda i, j: (which, i, 0, 0))

    kv_shape = jax.ShapeDtypeStruct((m, W_HEADS), F32)
    return pl.pallas_call(
        functools.partial(_proj_kernel, gb=gb, per=per),
        grid=(ng // gb, d_in // tn),
        in_specs=[
            pl.BlockSpec((gb, ROW_GROUP, d), lambda i, j: (i, 0, 0)),
            pl.BlockSpec((1, d), lambda i, j: (0, 0)),
            mod_spec(MOD_SCALE_M), mod_spec(MOD_SHIFT_M),
            pl.BlockSpec((d, tn), lambda i, j: (0, j)),
            pl.BlockSpec((1, tn), lambda i, j: (0, jnp.clip(j - 3 * per, 0, per - 1))),
            pl.BlockSpec((1, tn), lambda i, j: (0, jnp.clip(j - 4 * per, 0, per - 1))),
        ],
        out_specs=[
            pl.BlockSpec((tm, tn), lambda i, j: (i, j)),
            pl.BlockSpec((tm, tn), sect(1)),
            pl.BlockSpec((tm, tn), sect(2)),
            pl.BlockSpec((tm, tn), sect(4)),
            pl.BlockSpec((tm, tn), sect(5)),
        ],
        out_shape=[jax.ShapeDtypeStruct((m, d_in), BF16), kv_shape, kv_shape, kv_shape, kv_shape],
        scratch_shapes=[pltpu.VMEM((tm, d), BF16)],
        compiler_params=_params("arbitrary", "arbitrary"),
        name="proj",
    )(x3, g, mods, mods, w_in_bf, qn, kn)


def _strict_lower_twice(n):
    row = lax.broadcasted_iota(jnp.int32, (2 * n, n), 0)
    col = lax.broadcasted_iota(jnp.int32, (2 * n, n), 1)
    row = jnp.where(row >= n, row - n, row)
    return jnp.where(row > col, 1.0, 0.0).astype(BF16)


def _sb_heads(loads, n_heads, u, carries, diagonal):
    qkv = [[load(h) for h in range(n_heads)] for load in loads]
    zs = [[_dot_nt(q, k) for q, k, _ in blk] for blk in qkv]
    if diagonal:
        row = lax.broadcasted_iota(jnp.int32, zs[0][0].shape, 0)
        col = lax.broadcasted_iota(jnp.int32, zs[0][0].shape, 1)
        valid = col < row
    sps, tails = [], []
    for blk in zs:
        sps.append([])
        tails.append([])
        for z in blk:
            sp = jnp.maximum(z, 0.0) + jnp.log(1.0 + jnp.exp(-jnp.abs(z)))
            if diagonal:
                sp = jnp.where(valid, sp, 0.0)
            hi = sp.astype(BF16)
            lo = (sp - hi.astype(F32)).astype(BF16)
            sps[-1].append(sp)
            tails[-1].append(_dot(jnp.concatenate([hi, lo], axis=1), u))
    pvs, new = [], []
    for h in range(n_heads):
        tq = zs[0][h].shape[0]
        carry = jnp.zeros((tq, 1), F32) if carries is None else carries[:, h * HEAD_DIM:h * HEAD_DIM + 1]
        pv = None
        for b in range(len(loads)):
            z, sp, tail = zs[b][h], sps[b][h], tails[b][h]
            loga = z - sp - tail - carry
            if diagonal:
                loga = jnp.where(valid, loga, NEG_BIG)
            term = _dot(jnp.exp(loga).astype(BF16), qkv[b][h][2])
            pv = term if pv is None else pv + term
            carry = carry + tail[:, 0:1] + sp[:, 0:1]
        pvs.append(pv)
        new.append(jnp.broadcast_to(carry, (tq, HEAD_DIM)))
    return jnp.concatenate(pvs, axis=1), jnp.concatenate(new, axis=1)


SB_T = 256
SB_HEADS_PER_STEP = 8
SB_STEP_BLOCKS = 1


def _head_slice(h):
    return slice(h * HEAD_DIM, (h + 1) * HEAD_DIM)


def _sb_prompt_kernel(q_ref, k_ref, v_ref, u_ref, o_ref, acc_scr, carry_scr):
    qi = pl.program_id(2)
    u = u_ref[...]

    def block(r, carries, diagonal):
        def load(h):
            hs = _head_slice(h)
            return q_ref[0, :, hs], k_ref[0, pl.ds(r, SB_T), hs], v_ref[0, pl.ds(r, SB_T), hs]
        return _sb_heads([load], SB_HEADS_PER_STEP, u, carries, diagonal)

    acc_scr[...], carry_scr[...] = block(pl.multiple_of(qi * SB_T, SB_T), None, True)

    def body(step, c):
        pv, carries = block(pl.multiple_of((qi - 1 - step) * SB_T, SB_T), carry_scr[...], False)
        acc_scr[...] += pv
        carry_scr[...] = carries
        return c

    lax.fori_loop(0, qi, body, 0)
    o_ref[0] = acc_scr[...].astype(BF16)


def _sb_prompt(proj3, u):
    b, s, _ = proj3.shape
    hp = SB_HEADS_PER_STEP
    w = hp * HEAD_DIM
    kcol, vcol = W_HEADS // w, 2 * W_HEADS // w
    return pl.pallas_call(
        _sb_prompt_kernel,
        grid=(b, N_HEADS // hp, s // SB_T),
        in_specs=[
            pl.BlockSpec((1, SB_T, w), lambda bi, h, qi: (bi, qi, h)),
            pl.BlockSpec((1, s, w), lambda bi, h, qi: (bi, 0, kcol + h)),
            pl.BlockSpec((1, s, w), lambda bi, h, qi: (bi, 0, vcol + h)),
            pl.BlockSpec((2 * SB_T, SB_T), lambda bi, h, qi: (0, 0)),
        ],
        out_specs=pl.BlockSpec((1, SB_T, w), lambda bi, h, qi: (bi, qi, h)),
        out_shape=jax.ShapeDtypeStruct((b, s, W_HEADS), BF16),
        scratch_shapes=[pltpu.VMEM((SB_T, w), F32), pltpu.VMEM((SB_T, w), F32)],
        compiler_params=_params("arbitrary", "arbitrary", "arbitrary"),
        name="sb_prompt",
    )(proj3, proj3, proj3, u)


def _sb_step_kernel(q_ref, kn_ref, vn_ref, ck_ref, cv_ref, u_ref, o_ref, acc_scr, carry_scr, *, tkb, t):
    j = pl.program_id(1)
    u = u_ref[...]

    @pl.when(j == 0)
    def _():
        def load(h):
            hs = _head_slice(h)
            return q_ref[0, :, hs], kn_ref[0, :, hs], vn_ref[0, :, hs]
        acc_scr[...], carry_scr[...] = _sb_heads([load], N_HEADS, _strict_lower_twice(t), None, True)

    def body(step, c):
        def loader(blk):
            r = pl.multiple_of((tkb - (step * SB_STEP_BLOCKS + blk + 1) * SB_T) * N_HEADS, SB_T * N_HEADS)

            def load(h):
                rows = pl.ds(r + h, SB_T, stride=N_HEADS)
                return (q_ref[0, :, _head_slice(h)], ck_ref[0, rows, :].astype(BF16),
                        cv_ref[0, rows, :].astype(BF16))
            return load

        pv, carries = _sb_heads([loader(blk) for blk in range(SB_STEP_BLOCKS)], N_HEADS, u, carry_scr[...], False)
        acc_scr[...] += pv
        carry_scr[...] = carries
        return c

    lax.fori_loop(0, tkb // (SB_T * SB_STEP_BLOCKS), body, 0)

    @pl.when(j == pl.num_programs(1) - 1)
    def _():
        o_ref[0] = acc_scr[...].astype(BF16)


def _sb_step(proj3, cache_k, cache_v, u):
    b, t, _ = proj3.shape
    p = cache_k.shape[1] // N_HEADS
    tkb = min(1024, p)
    nj = p // tkb
    cache_spec = pl.BlockSpec((1, tkb * N_HEADS, HEAD_DIM), lambda bi, j: (bi, nj - 1 - j, 0))
    return pl.pallas_call(
        functools.partial(_sb_step_kernel, tkb=tkb, t=t),
        grid=(b, nj),
        in_specs=[
            pl.BlockSpec((1, t, W_HEADS), lambda bi, j: (bi, 0, 0)),
            pl.BlockSpec((1, t, W_HEADS), lambda bi, j: (bi, 0, 1)),
            pl.BlockSpec((1, t, W_HEADS), lambda bi, j: (bi, 0, 2)),
            cache_spec, cache_spec,
            pl.BlockSpec((2 * SB_T, SB_T), lambda bi, j: (0, 0)),
        ],
        out_specs=pl.BlockSpec((1, t, W_HEADS), lambda bi, j: (bi, 0, 0)),
        out_shape=jax.ShapeDtypeStruct((b, t, W_HEADS), BF16),
        scratch_shapes=[pltpu.VMEM((t, W_HEADS), F32), pltpu.VMEM((t, W_HEADS), F32)],
        compiler_params=_params("arbitrary", "arbitrary"),
        name="sb_step",
    )(proj3, proj3, proj3, cache_k, cache_v, u)


BAND_QB = 4 * CHUNK
BAND_KB = 3


BIAS_NEAR = REL_CLIP // CHUNK + 1


def _bias_blocks_kernel(table_ref, o_ref):
    h = pl.program_id(0)
    i = lax.broadcasted_iota(jnp.int32, (CHUNK, CHUNK), 0)
    j = lax.broadcasted_iota(jnp.int32, (CHUNK, CHUNK), 1)
    for d in range(BIAS_NEAR):
        idx = jnp.clip(i - j + d * CHUNK, -REL_CLIP, REL_CLIP) + REL_CLIP

        def body(r, acc, idx=idx):
            return jnp.where(idx == r, table_ref[h, r], acc)

        lo = max(d * CHUNK - (CHUNK - 1), -REL_CLIP) + REL_CLIP
        hi = min(d * CHUNK + (CHUNK - 1), REL_CLIP) + REL_CLIP
        o_ref[0, d] = lax.fori_loop(lo, hi + 1, body, jnp.zeros((CHUNK, CHUNK), F32))


def _bias_blocks(table):
    nh = table.shape[0]
    return pl.pallas_call(
        _bias_blocks_kernel,
        grid=(nh,),
        in_specs=[pl.BlockSpec(memory_space=pltpu.SMEM)],
        out_specs=pl.BlockSpec((1, BIAS_NEAR, CHUNK, CHUNK), lambda h: (h, 0, 0, 0)),
        out_shape=jax.ShapeDtypeStruct((nh, BIAS_NEAR, CHUNK, CHUNK), F32),
        compiler_params=_params("arbitrary"),
        name="bias_blocks",
    )(table.astype(F32))


def _chunk_bias(blocks, far, dist):
    if dist < 0 or dist > BAND_LEFT_CHUNKS:
        return jnp.full(far.shape, NEG_BIG, F32)
    return blocks[:, dist] if dist < BIAS_NEAR else far


def _band_bias(blocks, table):
    far = jnp.broadcast_to(table[:, -1].astype(F32)[:, None, None], (table.shape[0], CHUNK, CHUNK))
    qc, kc = BAND_QB // CHUNK, BAND_KB * BAND_QB // CHUNK
    rows = [jnp.concatenate([_chunk_bias(blocks, far, r - c + BAND_LEFT_CHUNKS) for c in range(kc)], axis=2)
            for r in range(qc)]
    return jnp.concatenate(rows, axis=1)


def _band_step_bias(blocks, table, lb):
    far = jnp.broadcast_to(table[:, -1].astype(F32)[:, None, None], (table.shape[0], CHUNK, CHUNK))
    nc = lb // CHUNK
    return jnp.concatenate([blocks[:, nc - c] if nc - c < BIAS_NEAR else far for c in range(nc + 1)], axis=2)


def _softmax_pv_heads(scores, values):
    ps, dens = [], []
    for sc in scores:
        m = sc[0].max(axis=1, keepdims=True)
        for s in sc[1:]:
            m = jnp.maximum(m, s.max(axis=1, keepdims=True))
        p = [jnp.exp(s - m) for s in sc]
        den = p[0].sum(axis=1, keepdims=True)
        for pc in p[1:]:
            den = den + pc.sum(axis=1, keepdims=True)
        ps.append(p)
        dens.append(den)
    outs = []
    for p, den, vals in zip(ps, dens, values):
        num = _dot(p[0].astype(BF16), vals[0])
        for pc, v in zip(p[1:], vals[1:]):
            num = num + _dot(pc.astype(BF16), v)
        outs.append(num / den)
    return jnp.concatenate(outs, axis=1)


def _band_prompt_kernel(q_ref, k0_ref, k1_ref, k2_ref, v0_ref, v1_ref, v2_ref, bias_ref, o_ref):
    qi = pl.program_id(1)
    scores, values = [], []
    for h in range(N_HEADS):
        hs = _head_slice(h)
        q = q_ref[0, :, hs]
        sc = []
        for c, k_ref in enumerate((k0_ref, k1_ref, k2_ref)):
            s = _dot_nt(q, k_ref[0, :, hs]) + bias_ref[h, :, c * BAND_QB:(c + 1) * BAND_QB]
            if c < BAND_KB - 1:
                s = jnp.where(qi + c >= BAND_KB - 1, s, NEG_BIG)
            sc.append(s)
        scores.append(sc)
        values.append([v_ref[0, :, hs] for v_ref in (v0_ref, v1_ref, v2_ref)])
    o_ref[0] = _softmax_pv_heads(scores, values).astype(BF16)


def _band_prompt(proj3, bias):
    b, s, _ = proj3.shape
    qcol, kcol, vcol = 3, 4, 5

    def kv_spec(col, back):
        return pl.BlockSpec((1, BAND_QB, W_HEADS), lambda bi, qi: (bi, jnp.maximum(qi - back, 0), col))

    return pl.pallas_call(
        _band_prompt_kernel,
        grid=(b, s // BAND_QB),
        in_specs=[
            pl.BlockSpec((1, BAND_QB, W_HEADS), lambda bi, qi: (bi, qi, qcol)),
            kv_spec(kcol, 2), kv_spec(kcol, 1), kv_spec(kcol, 0),
            kv_spec(vcol, 2), kv_spec(vcol, 1), kv_spec(vcol, 0),
            pl.BlockSpec((N_HEADS, BAND_QB, BAND_KB * BAND_QB), lambda bi, qi: (0, 0, 0)),
        ],
        out_specs=pl.BlockSpec((1, BAND_QB, W_HEADS), lambda bi, qi: (bi, qi, 0)),
        out_shape=jax.ShapeDtypeStruct((b, s, W_HEADS), BF16),
        compiler_params=_params("arbitrary", "arbitrary"),
        name="band_prompt",
    )(proj3, proj3, proj3, proj3, proj3, proj3, proj3, bias)


def _band_step_kernel(q_ref, kn_ref, vn_ref, ck_ref, cv_ref, bias_ref, o_ref, *, lb):
    scores, values = [], []
    for h in range(N_HEADS):
        hs = _head_slice(h)
        q = q_ref[0, :, hs]
        rows = pl.ds(h, lb, stride=N_HEADS)
        scores.append([_dot_nt(q, ck_ref[0, rows, :].astype(BF16)) + bias_ref[h, :, :lb],
                       _dot_nt(q, kn_ref[0, :, hs]) + bias_ref[h, :, lb:]])
        values.append([cv_ref[0, rows, :].astype(BF16), vn_ref[0, :, hs]])
    o_ref[0] = _softmax_pv_heads(scores, values).astype(BF16)


def _band_step(proj3, cache_k, cache_v, bias):
    b, t, _ = proj3.shape
    lb = cache_k.shape[1] // N_HEADS
    cache_spec = pl.BlockSpec((1, lb * N_HEADS, HEAD_DIM), lambda bi: (bi, 0, 0))
    return pl.pallas_call(
        functools.partial(_band_step_kernel, lb=lb),
        grid=(b,),
        in_specs=[
            pl.BlockSpec((1, t, W_HEADS), lambda bi: (bi, 0, 3)),
            pl.BlockSpec((1, t, W_HEADS), lambda bi: (bi, 0, 4)),
            pl.BlockSpec((1, t, W_HEADS), lambda bi: (bi, 0, 5)),
            cache_spec, cache_spec,
            pl.BlockSpec((N_HEADS, t, lb + t), lambda bi: (0, 0, 0)),
        ],
        out_specs=pl.BlockSpec((1, t, W_HEADS), lambda bi: (bi, 0, 0)),
        out_shape=jax.ShapeDtypeStruct((b, t, W_HEADS), BF16),
        compiler_params=_params("arbitrary"),
        name="band_step",
    )(proj3, proj3, proj3, cache_k, cache_v, bias)


def _merge_kernel(osb_ref, obd_ref, wsb_ref, wbd_ref, ga_ref, gb_ref, o_ref):
    a = _dot(osb_ref[...], wsb_ref[...])
    b = _dot(obd_ref[...], wbd_ref[...])
    merged = _sigmoid(ga_ref[...].astype(F32)) * a + _sigmoid(gb_ref[...].astype(F32)) * b
    o_ref[...] = merged.astype(BF16)


def _merge(o_sb, o_band, w_sb_bf, w_band_bf, proj, tm):
    m = o_sb.shape[0]
    d = w_sb_bf.shape[1]
    tn = min(MATMUL_TN, d)
    ga_col = 6 * W_HEADS // tn
    gb_col = ga_col + d // tn
    return pl.pallas_call(
        _merge_kernel,
        grid=(m // tm, d // tn),
        in_specs=[
            pl.BlockSpec((tm, W_HEADS), lambda i, j: (i, 0)),
            pl.BlockSpec((tm, W_HEADS), lambda i, j: (i, 0)),
            pl.BlockSpec((W_HEADS, tn), lambda i, j: (0, j)),
            pl.BlockSpec((W_HEADS, tn), lambda i, j: (0, j)),
            pl.BlockSpec((tm, tn), lambda i, j: (i, ga_col + j)),
            pl.BlockSpec((tm, tn), lambda i, j: (i, gb_col + j)),
        ],
        out_specs=pl.BlockSpec((tm, tn), lambda i, j: (i, j)),
        out_shape=jax.ShapeDtypeStruct((m, d), BF16),
        compiler_params=_params("arbitrary", "arbitrary"),
        name="merge",
    )(o_sb, o_band, w_sb_bf, w_band_bf, proj, proj)


def _outproj_kernel(m_ref, w_ref, x_ref, gate_ref, o_ref, *, gb):
    acc = _dot(m_ref[...], w_ref[...])
    acc = acc.reshape(gb, ROW_GROUP, acc.shape[1])
    o_ref[...] = x_ref[...] + gate_ref[0] * acc


def _outproj(merged, w_out_bf, x3, mods, gb):
    ng, _, d = x3.shape
    tm = gb * ROW_GROUP
    tn = min(MATMUL_TN, d)
    return pl.pallas_call(
        functools.partial(_outproj_kernel, gb=gb),
        grid=(ng // gb, d // tn),
        in_specs=[
            pl.BlockSpec((tm, d), lambda i, j: (i, 0)),
            pl.BlockSpec((d, tn), lambda i, j: (0, j)),
            pl.BlockSpec((gb, ROW_GROUP, tn), lambda i, j: (i, 0, j)),
            pl.BlockSpec((1, gb, 1, tn), lambda i, j: (MOD_GATE_M, i, 0, j)),
        ],
        out_specs=pl.BlockSpec((gb, ROW_GROUP, tn), lambda i, j: (i, 0, j)),
        out_shape=jax.ShapeDtypeStruct(x3.shape, F32),
        compiler_params=_params("arbitrary", "arbitrary"),
        name="outproj",
    )(merged, w_out_bf, x3, mods)


GATHER_HALF = 128


def _tile_rows(d):
    assert d % V7X_LANES == 0
    return d // V7X_LANES


def _store_token_tiles(ref, first_token, x):
    n, d = x.shape
    s_per = _tile_rows(d)
    for c in range(s_per):
        ref[pl.ds(first_token * s_per + c, n, stride=s_per), :] = x[:, c * V7X_LANES:(c + 1) * V7X_LANES]


def _load_token_tiles(ref, n, d):
    s_per = _tile_rows(d)
    return jnp.concatenate([ref[pl.ds(c, n, stride=s_per), :] for c in range(s_per)], axis=1)


def _start_token_gather(table_hbm, idx_ref, first, buf, sems, n, s_per):
    for half in range(n // GATHER_HALF):
        def issue(i, carry, half=half):
            row = half * GATHER_HALF + i
            pltpu.make_async_copy(table_hbm.at[pl.ds(idx_ref[first + row] * s_per, s_per)],
                                  buf.at[pl.ds(row * s_per, s_per)], sems.at[half]).start()
            return carry
        lax.fori_loop(0, GATHER_HALF, issue, 0, unroll=8)


def _wait_token_gather(buf, sems, n, s_per):
    for half in range(n // GATHER_HALF):
        part = buf.at[pl.ds(half * GATHER_HALF * s_per, GATHER_HALF * s_per)]
        pltpu.make_async_copy(part, part, sems.at[half]).wait()


ROUTE_E1, ROUTE_E2, ROUTE_W1, ROUTE_W2 = 0, 1, 2, 3


def _router_kernel(x_ref, g_ref, sc_ref, sh_ref, whi_ref, wlo_ref, b_ref, h_ref, route_ref, hi_scr, lo_scr, *, gb):
    def body(s, carry):
        h = _modulated_norm(x_ref[s], g_ref[...], sc_ref[0, s], sh_ref[0, s])
        hi = h.astype(BF16)
        first = pl.multiple_of(s * ROW_GROUP, ROW_GROUP)
        _store_token_tiles(h_ref, first, h)
        hi_scr[pl.ds(first, ROW_GROUP), :] = hi
        lo_scr[pl.ds(first, ROW_GROUP), :] = (h - hi.astype(F32)).astype(BF16)
        return carry
    lax.fori_loop(0, gb, body, 0)

    hi = hi_scr[...]
    logits = _dot(hi, whi_ref[...]) + _dot(hi, wlo_ref[...]) + _dot(lo_scr[...], whi_ref[...]) + b_ref[...]
    lane = lax.broadcasted_iota(jnp.int32, logits.shape, 1)
    ninf = -jnp.inf

    lane_f = lane.astype(F32)

    def first_max(vals):
        mx = vals.max(axis=1, keepdims=True)
        idx = jnp.where(vals == mx, lane_f, float(V7X_LANES)).min(axis=1, keepdims=True)
        return mx, idx

    gl = jnp.where(lane < N_GROUPS, logits, ninf)
    gmax, gidx = first_max(gl)
    g_weight = 1.0 / jnp.exp(gl - gmax).sum(axis=1, keepdims=True)
    lo_lane = N_GROUPS + EXPERTS_PER_GROUP * gidx
    el = jnp.where(lane_f >= lo_lane, jnp.where(lane_f < lo_lane + EXPERTS_PER_GROUP, logits, ninf), ninf)
    m1, i1 = first_max(el)
    m2, i2 = first_max(jnp.where(lane_f == i1, ninf, el))
    e21 = jnp.exp(m2 - m1)
    p1 = 1.0 / (1.0 + e21)
    p2 = e21 / (1.0 + e21)
    route = jnp.where(lane == ROUTE_E1, i1 - N_GROUPS,
                      jnp.where(lane == ROUTE_E2, i2 - N_GROUPS,
                                jnp.where(lane == ROUTE_W1, g_weight * p1,
                                          jnp.where(lane == ROUTE_W2, g_weight * p2, 0.0))))
    route_ref[...] = route


def _router(x3, g, mods, w_hi, w_lo, bias, gb):
    ng, _, d = x3.shape
    m = ng * ROW_GROUP
    tm = gb * ROW_GROUP
    s_per = _tile_rows(d)

    def mod_spec(which):
        return pl.BlockSpec((1, gb, 1, d), lambda i: (which, i, 0, 0))

    w_spec = pl.BlockSpec((d, V7X_LANES), lambda i: (0, 0))
    return pl.pallas_call(
        functools.partial(_router_kernel, gb=gb),
        grid=(ng // gb,),
        in_specs=[
            pl.BlockSpec((gb, ROW_GROUP, d), lambda i: (i, 0, 0)),
            pl.BlockSpec((1, d), lambda i: (0, 0)),
            mod_spec(MOD_SCALE_F), mod_spec(MOD_SHIFT_F), w_spec, w_spec,
            pl.BlockSpec((1, V7X_LANES), lambda i: (0, 0)),
        ],
        out_specs=[
            pl.BlockSpec((tm * s_per, V7X_LANES), lambda i: (i, 0)),
            pl.BlockSpec((tm, V7X_LANES), lambda i: (i, 0)),
        ],
        out_shape=[jax.ShapeDtypeStruct((m * s_per, V7X_LANES), F32),
                   jax.ShapeDtypeStruct((m, V7X_LANES), F32)],
        scratch_shapes=[pltpu.VMEM((tm, d), BF16), pltpu.VMEM((tm, d), BF16)],
        compiler_params=_params("arbitrary"),
        name="router",
    )(x3, g, mods, mods, w_hi, w_lo, bias)


def _moe_kernel(te_ref, nv_ref, src_ref, h_hbm, wg_ref, wu_ref, wd_ref, o_ref, xbuf, sems, wg_scr, wu_scr, wd_scr):
    t = pl.program_id(0)
    n_valid = nv_ref[0]
    d = wg_scr.shape[0]
    s_per = _tile_rows(d)

    def start(tile, slot):
        _start_token_gather(h_hbm, src_ref, tile * MOE_TILE, xbuf.at[slot], sems.at[slot], MOE_TILE, s_per)

    @pl.when(t == 0)
    def _():
        start(0, 0)

    @pl.when(t + 1 < n_valid)
    def _():
        start(t + 1, (t + 1) % 2)

    changed = te_ref[t] != te_ref[jnp.maximum(t - 1, 0)]

    @pl.when((t == 0) | changed)
    def _():
        wg_scr[...] = wg_ref[0].astype(BF16)
        wu_scr[...] = wu_ref[0].astype(BF16)
        wd_scr[...] = wd_ref[0].astype(BF16)

    @pl.when(t < n_valid)
    def _():
        slot = t % 2
        _wait_token_gather(xbuf.at[slot], sems.at[slot], MOE_TILE, s_per)
        x = _load_token_tiles(xbuf.at[slot], MOE_TILE, d).astype(BF16)
        g = _dot(x, wg_scr[...])
        u = _dot(x, wu_scr[...])
        hidden = (g * _sigmoid(g)) * u
        _store_token_tiles(o_ref, 0, _dot(hidden.astype(BF16), wd_scr[...]))

    @pl.when(t >= n_valid)
    def _():
        o_ref[...] = jnp.zeros(o_ref.shape, F32)


def _moe(tile_expert, n_valid, src_tok, h_tiles, w_gate, w_up, w_down):
    _, d, f = w_gate.shape
    s_per = _tile_rows(d)
    n_tiles = src_tok.shape[0] // MOE_TILE
    halves = MOE_TILE // GATHER_HALF
    grid_spec = pltpu.PrefetchScalarGridSpec(
        num_scalar_prefetch=3,
        grid=(n_tiles,),
        in_specs=[
            pl.BlockSpec(memory_space=pl.ANY),
            pl.BlockSpec((1, d, f), lambda t, te, nv, src: (te[t], 0, 0)),
            pl.BlockSpec((1, d, f), lambda t, te, nv, src: (te[t], 0, 0)),
            pl.BlockSpec((1, f, d), lambda t, te, nv, src: (te[t], 0, 0)),
        ],
        out_specs=pl.BlockSpec((MOE_TILE * s_per, V7X_LANES), lambda t, te, nv, src: (t, 0)),
        scratch_shapes=[pltpu.VMEM((2, MOE_TILE * s_per, V7X_LANES), F32), pltpu.SemaphoreType.DMA((2, halves)),
                        pltpu.VMEM((d, f), BF16), pltpu.VMEM((d, f), BF16), pltpu.VMEM((f, d), BF16)],
    )
    return pl.pallas_call(
        _moe_kernel,
        grid_spec=grid_spec,
        out_shape=jax.ShapeDtypeStruct((n_tiles * MOE_TILE * s_per, V7X_LANES), F32),
        compiler_params=_params("arbitrary"),
        name="moe",
    )(tile_expert, n_valid, src_tok, h_tiles, w_gate, w_up, w_down)


def _dispatch(route):
    n = route.shape[0]
    e = jnp.concatenate([route[:, ROUTE_E1], route[:, ROUTE_E2]]).astype(jnp.int32)
    tok = jnp.concatenate([jnp.arange(n, dtype=jnp.int32)] * 2)
    onehot = (e[:, None] == jnp.arange(N_EXPERTS, dtype=jnp.int32)[None, :]).astype(jnp.int32)
    before = jnp.cumsum(onehot, axis=0) - onehot
    rank = jnp.sum(before * onehot, axis=1)
    counts = jnp.sum(onehot, axis=0)
    padded = ((counts + MOE_TILE - 1) // MOE_TILE) * MOE_TILE
    ends = jnp.cumsum(padded)
    pos = (ends - padded)[e] + rank
    n_tiles = -(-(2 * n + N_EXPERTS * (MOE_TILE - 1)) // (4 * MOE_TILE)) * 4
    tile_start = jnp.arange(n_tiles, dtype=jnp.int32) * MOE_TILE
    n_valid = (ends[-1] // MOE_TILE).astype(jnp.int32)
    tile_expert = jnp.sum((ends[None, :] <= tile_start[:, None]).astype(jnp.int32), axis=1)
    last_expert = tile_expert[jnp.maximum(n_valid - 1, 0)]
    tile_expert = jnp.where(tile_start < ends[-1], tile_expert, last_expert)
    src_tok = jnp.zeros((n_tiles * MOE_TILE,), jnp.int32).at[pos].set(tok)
    return pos[:n], pos[n:], src_tok, tile_expert, n_valid.reshape(1)


def _final_kernel(p1_ref, p2_ref, x_ref, y_hbm, route_ref, gate_ref, o_ref, ybuf, sems):
    i = pl.program_id(0)
    gb, rg, d = x_ref.shape
    tm = gb * rg
    s_per = _tile_rows(d)

    def start(step, slot):
        for which, pos_ref in enumerate((p1_ref, p2_ref)):
            _start_token_gather(y_hbm, pos_ref, step * tm, ybuf.at[slot, which], sems.at[slot, which], tm, s_per)

    @pl.when(i == 0)
    def _():
        start(0, 0)

    @pl.when(i + 1 < pl.num_programs(0))
    def _():
        start(i + 1, (i + 1) % 2)

    slot = i % 2
    for which in range(2):
        _wait_token_gather(ybuf.at[slot, which], sems.at[slot, which], tm, s_per)
    r = route_ref[...]
    y = (r[:, ROUTE_W1:ROUTE_W1 + 1] * _load_token_tiles(ybuf.at[slot, 0], tm, d)
         + r[:, ROUTE_W2:ROUTE_W2 + 1] * _load_token_tiles(ybuf.at[slot, 1], tm, d))
    o_ref[...] = x_ref[...] + gate_ref[0] * y.reshape(gb, rg, d)


def _final(x3, y_tiles, pos1, pos2, route, mods, gb):
    ng, _, d = x3.shape
    tm = gb * ROW_GROUP
    s_per = _tile_rows(d)
    blk = pl.BlockSpec((gb, ROW_GROUP, d), lambda i, p1, p2: (i, 0, 0))
    grid_spec = pltpu.PrefetchScalarGridSpec(
        num_scalar_prefetch=2,
        grid=(ng // gb,),
        in_specs=[
            blk,
            pl.BlockSpec(memory_space=pl.ANY),
            pl.BlockSpec((tm, V7X_LANES), lambda i, p1, p2: (i, 0)),
            pl.BlockSpec((1, gb, 1, d), lambda i, p1, p2: (MOD_GATE_F, i, 0, 0)),
        ],
        out_specs=blk,
        scratch_shapes=[pltpu.VMEM((2, 2, tm * s_per, V7X_LANES), F32),
                        pltpu.SemaphoreType.DMA((2, 2, tm // GATHER_HALF))],
    )
    return pl.pallas_call(
        _final_kernel,
        grid_spec=grid_spec,
        out_shape=jax.ShapeDtypeStruct(x3.shape, F32),
        compiler_params=_params("arbitrary"),
        name="final",
    )(pos1, pos2, x3, y_tiles, route, mods)


def _layer(xp, xs, csk, csv, cbk, cbv, c_prompt, c_sample, norm_mix, norm_ffn, w_ada, b_ada, w_in, q_norm,
           k_norm, rel_table, w_proj_sb, w_proj_band, w_out, w_rg, b_rg, w_re, b_re, w_gate, w_up, w_down):
    bp, sp_len, d = xp.shape
    bs, ts, _ = xs.shape
    d_in = w_in.shape[1]
    assert sp_len % BAND_QB == 0 and ts == ROW_GROUP and sp_len % ROW_GROUP == 0

    n_c = bp + bs
    c_pad = jnp.concatenate([c_prompt, c_sample, jnp.zeros((-n_c % 8, d), F32)], axis=0)
    mod = _ada(c_pad, w_ada, b_ada)
    gp = sp_len // ROW_GROUP
    mod_p = jnp.repeat(mod[:, :bp], gp, axis=1)[:, :, None, :]
    mod_s = mod[:, bp:n_c][:, :, None, :]

    w_in_bf = w_in.astype(BF16)
    w_sb_bf = w_proj_sb.astype(BF16)
    w_band_bf = w_proj_band.astype(BF16)
    w_out_bf = w_out.astype(BF16)
    qn = q_norm.reshape(1, W_HEADS)
    kn = k_norm.reshape(1, W_HEADS)
    g_mix = norm_mix.reshape(1, d)
    g_ffn = norm_ffn.reshape(1, d)
    lb = cbk.shape[1]
    assert lb % CHUNK == 0 and ts == CHUNK
    bias_blocks = _bias_blocks(rel_table)
    bias = _band_bias(bias_blocks, rel_table)
    bias_step = _band_step_bias(bias_blocks, rel_table, lb)
    u = _strict_lower_twice(SB_T)

    w_router = jnp.zeros((d, V7X_LANES), F32).at[:, :N_GROUPS].set(w_rg)
    w_router = w_router.at[:, N_GROUPS:N_GROUPS + N_EXPERTS].set(w_re)
    w_router_hi = w_router.astype(BF16)
    w_router_lo = (w_router - w_router_hi.astype(F32)).astype(BF16)
    b_router = jnp.zeros((1, V7X_LANES), F32).at[0, :N_GROUPS].set(b_rg)
    b_router = b_router.at[0, N_GROUPS:N_GROUPS + N_EXPERTS].set(b_re.reshape(-1))

    xp3 = xp.reshape(bp * gp, ROW_GROUP, d)
    xs3 = xs.reshape(bs, ROW_GROUP, d)
    gb_p = min(ROWWISE_GROUPS, bp * gp)
    gb_s = min(ROWWISE_GROUPS, bs)

    def mixer(x3, mods, gb, attend):
        proj, ka, va, kb, vb = _proj(x3, g_mix, mods, w_in_bf, qn, kn, gb)
        o_sb, o_band = attend(proj)
        merged = _merge(o_sb, o_band, w_sb_bf, w_band_bf, proj, gb * ROW_GROUP)
        x1 = _outproj(merged, w_out_bf, x3, mods, gb)
        return x1, ka, va, kb, vb

    def attend_prompt(proj):
        p3 = proj.reshape(bp, sp_len, d_in)
        return (_sb_prompt(p3, u).reshape(bp * sp_len, W_HEADS),
                _band_prompt(p3, bias).reshape(bp * sp_len, W_HEADS))

    def attend_sample(proj):
        p3 = proj.reshape(bs, ts, d_in)
        def rows(cache):
            return cache.reshape(bs, -1, HEAD_DIM)

        o_sb = _sb_step(p3, rows(csk), rows(csv), u)
        o_band = _band_step(p3, rows(cbk), rows(cbv), bias_step)
        return o_sb.reshape(bs * ts, W_HEADS), o_band.reshape(bs * ts, W_HEADS)

    x1p, kap, vap, kbp, vbp = mixer(xp3, mod_p, min(MATMUL_GROUPS, bp * gp), attend_prompt)
    x1s, kas, vas, kbs, vbs = mixer(xs3, mod_s, min(MATMUL_GROUPS, bs), attend_sample)

    def ffn(x1, mods, gb):
        h2, route = _router(x1, g_ffn, mods, w_router_hi, w_router_lo, b_router, gb)
        pos1, pos2, src_tok, tile_expert, n_valid = _dispatch(route)
        y_sorted = _moe(tile_expert, n_valid, src_tok, h2, w_gate, w_up, w_down)
        return _final(x1, y_sorted, pos1, pos2, route, mods, min(FINAL_GROUPS, x1.shape[0]))

    yp = ffn(x1p, mod_p, gb_p).reshape(bp, sp_len, d)
    ys = ffn(x1s, mod_s, gb_s).reshape(bs, ts, d)

    def heads(a, b):
        return a.reshape(b, -1, N_HEADS, HEAD_DIM)

    n_band = min(BAND_PAST, sp_len)
    return (yp, ys, heads(kap, bp), heads(vap, bp), heads(kbp, bp)[:, -n_band:], heads(vbp, bp)[:, -n_band:],
            heads(kas, bs), heads(vas, bs), heads(kbs, bs), heads(vbs, bs))


def kernel(x_prompt, x_sample, cache_sb_k, cache_sb_v, cache_band_k, cache_band_v, c_prompt, c_sample, norm_mix, norm_ffn, w_ada, b_ada, w_in, q_norm_band, k_norm_band, rel_bias_band, w_proj_sb, w_proj_band, w_out, w_router_group, b_router_group, w_router_expert, b_router_expert, w_gate, w_up, w_down):
    depth = w_in.shape[0]
    xp, xs = x_prompt, x_sample
    outs = [[] for _ in range(8)]
    for l in range(depth):
        res = _layer(xp, xs, cache_sb_k[l], cache_sb_v[l], cache_band_k[l], cache_band_v[l], c_prompt, c_sample,
                     norm_mix[l], norm_ffn[l], w_ada[l], b_ada[l], w_in[l], q_norm_band[l], k_norm_band[l],
                     rel_bias_band[l], w_proj_sb[l], w_proj_band[l], w_out[l], w_router_group[l],
                     b_router_group[l], w_router_expert[l], b_router_expert[l], w_gate[l], w_up[l], w_down[l])
        xp, xs = res[0], res[1]
        for acc, r in zip(outs, res[2:]):
            acc.append(r)
    return (xp, xs) + tuple(jnp.stack(o, axis=0) for o in outs)
```

```python
import functools

import jax
import jax.numpy as jnp
from jax import lax
from jax.experimental import pallas as pl
from jax.experimental.pallas import tpu as pltpu

F32 = jnp.float32
BF16 = jnp.bfloat16

EPS = 1e-6
HEAD_DIM = 128
N_HEADS = 8
W_HEADS = N_HEADS * HEAD_DIM
CHUNK = 64
BAND_LEFT_CHUNKS = 8
BAND_PAST = BAND_LEFT_CHUNKS * CHUNK
REL_CLIP = 128
N_GROUPS = 4
EXPERTS_PER_GROUP = 8
N_EXPERTS = N_GROUPS * EXPERTS_PER_GROUP
QK_SCALE = HEAD_DIM ** -0.5
NEG_BIG = -1e30

V7X_LANES = 128
V7X_VMEM_LIMIT = 56 * 1024 * 1024
ROW_GROUP = 64
MOE_TILE = 256
MATMUL_GROUPS = 16
MATMUL_TN = 1024
ROWWISE_GROUPS = 8
FINAL_GROUPS = 4


def _params(*sem):
    return pltpu.CompilerParams(dimension_semantics=sem, vmem_limit_bytes=V7X_VMEM_LIMIT)


def _sigmoid(x):
    return 1.0 / (1.0 + jnp.exp(-x))


def _dot(a, b):
    return jnp.dot(a, b, preferred_element_type=F32)


def _dot_nt(a, b):
    return lax.dot_general(a, b, (((1,), (1,)), ((), ())), preferred_element_type=F32)


def _ada_kernel(c_ref, w_ref, b_ref, o_ref):
    c = c_ref[...]
    a = (c * _sigmoid(c)).astype(BF16)
    o_ref[0] = _dot(a, w_ref[...].astype(BF16)) + b_ref[...]


def _ada(c_pad, w_ada, b_ada):
    r, d = c_pad.shape
    tn = min(1024, d)
    per = d // tn
    return pl.pallas_call(
        _ada_kernel,
        grid=(6 * per,),
        in_specs=[
            pl.BlockSpec((r, d), lambda j: (0, 0)),
            pl.BlockSpec((d, tn), lambda j: (0, j)),
            pl.BlockSpec((1, tn), lambda j: (0, j)),
        ],
        out_specs=pl.BlockSpec((1, r, tn), lambda j: (j // per, 0, j % per)),
        out_shape=jax.ShapeDtypeStruct((6, r, d), F32),
        compiler_params=_params("arbitrary"),
        name="ada",
    )(c_pad, w_ada, b_ada.reshape(1, 6 * d))


def _modulated_norm(x, g, scale, shift):
    ms = jnp.mean(x * x, axis=-1, keepdims=True)
    return (x * lax.rsqrt(ms + EPS) * g) * (1.0 + scale) + shift


def _head_norm(a, gain_ref):
    outs = []
    for hh in range(a.shape[1] // HEAD_DIM):
        blk = a[:, hh * HEAD_DIM:(hh + 1) * HEAD_DIM]
        ms = jnp.mean(blk * blk, axis=-1, keepdims=True)
        outs.append(blk * lax.rsqrt(ms + EPS) * gain_ref[:, hh * HEAD_DIM:(hh + 1) * HEAD_DIM])
    return jnp.concatenate(outs, axis=1)


def _proj_kernel(x_ref, g_ref, sc_ref, sh_ref, w_ref, qn_ref, kn_ref,
                 proj_ref, ka_ref, va_ref, kb_ref, vb_ref, h_scr, *, gb, per):
    j = pl.program_id(1)

    @pl.when(j == 0)
    def _():
        def body(s, carry):
            h = _modulated_norm(x_ref[s], g_ref[...], sc_ref[0, s], sh_ref[0, s])
            h_scr[pl.ds(pl.multiple_of(s * ROW_GROUP, ROW_GROUP), ROW_GROUP), :] = h.astype(BF16)
            return carry
        lax.fori_loop(0, gb, body, 0)

    acc = _dot(h_scr[...], w_ref[...])
    sec = j // per

    @pl.when(sec == 0)
    def _():
        proj_ref[...] = (acc * QK_SCALE).astype(BF16)

    @pl.when(sec == 1)
    def _():
        proj_ref[...] = acc.astype(BF16)
        ka_ref[...] = acc

    @pl.when(sec == 2)
    def _():
        proj_ref[...] = acc.astype(BF16)
        va_ref[...] = acc

    @pl.when(sec == 3)
    def _():
        proj_ref[...] = (_head_norm(acc, qn_ref) * QK_SCALE).astype(BF16)

    @pl.when(sec == 4)
    def _():
        n = _head_norm(acc, kn_ref)
        proj_ref[...] = n.astype(BF16)
        kb_ref[...] = n

    @pl.when(sec == 5)
    def _():
        proj_ref[...] = acc.astype(BF16)
        vb_ref[...] = acc

    @pl.when(sec >= 6)
    def _():
        proj_ref[...] = acc.astype(BF16)


MOD_SHIFT_M, MOD_SCALE_M, MOD_GATE_M, MOD_SHIFT_F, MOD_SCALE_F, MOD_GATE_F = range(6)


def _proj(x3, g, mods, w_in_bf, qn, kn, gb):
    ng, _, d = x3.shape
    d_in = w_in_bf.shape[1]
    m = ng * ROW_GROUP
    tm = gb * ROW_GROUP
    tn = 512
    per = W_HEADS // tn

    def sect(s):
        return lambda i, j: (i, jnp.clip(j - s * per, 0, per - 1))

    def mod_spec(which):
        return pl.BlockSpec((1, gb, 1, d), lambda i, j: (which, i, 0, 0))

    kv_shape = jax.ShapeDtypeStruct((m, W_HEADS), F32)
    return pl.pallas_call(
        functools.partial(_proj_kernel, gb=gb, per=per),
        grid=(ng // gb, d_in // tn),
        in_specs=[
            pl.BlockSpec((gb, ROW_GROUP, d), lambda i, j: (i, 0, 0)),
            pl.BlockSpec((1, d), lambda i, j: (0, 0)),
            mod_spec(MOD_SCALE_M), mod_spec(MOD_SHIFT_M),
            pl.BlockSpec((d, tn), lambda i, j: (0, j)),
            pl.BlockSpec((1, tn), lambda i, j: (0, jnp.clip(j - 3 * per, 0, per - 1))),
            pl.BlockSpec((1, tn), lambda i, j: (0, jnp.clip(j - 4 * per, 0, per - 1))),
        ],
        out_specs=[
            pl.BlockSpec((tm, tn), lambda i, j: (i, j)),
            pl.BlockSpec((tm, tn), sect(1)),
            pl.BlockSpec((tm, tn), sect(2)),
            pl.BlockSpec((tm, tn), sect(4)),
            pl.BlockSpec((tm, tn), sect(5)),
        ],
        out_shape=[jax.ShapeDtypeStruct((m, d_in), BF16), kv_shape, kv_shape, kv_shape, kv_shape],
        scratch_shapes=[pltpu.VMEM((tm, d), BF16)],
        compiler_params=_params("arbitrary", "arbitrary"),
        name="proj",
    )(x3, g, mods, mods, w_in_bf, qn, kn)


def _strict_lower_twice(n):
    row = lax.broadcasted_iota(jnp.int32, (2 * n, n), 0)
    col = lax.broadcasted_iota(jnp.int32, (2 * n, n), 1)
    row = jnp.where(row >= n, row - n, row)
    return jnp.where(row > col, 1.0, 0.0).astype(BF16)


def _sb_heads(loads, n_heads, u, carries, diagonal):
    qkv = [[load(h) for h in range(n_heads)] for load in loads]
    zs = [[_dot_nt(q, k) for q, k, _ in blk] for blk in qkv]
    if diagonal:
        row = lax.broadcasted_iota(jnp.int32, zs[0][0].shape, 0)
        col = lax.broadcasted_iota(jnp.int32, zs[0][0].shape, 1)
        valid = col < row
    sps, tails = [], []
    for blk in zs:
        sps.append([])
        tails.append([])
        for z in blk:
            sp = jnp.maximum(z, 0.0) + jnp.log(1.0 + jnp.exp(-jnp.abs(z)))
            if diagonal:
                sp = jnp.where(valid, sp, 0.0)
            hi = sp.astype(BF16)
            lo = (sp - hi.astype(F32)).astype(BF16)
            sps[-1].append(sp)
            tails[-1].append(_dot(jnp.concatenate([hi, lo], axis=1), u))
    pvs, new = [], []
    for h in range(n_heads):
        tq = zs[0][h].shape[0]
        carry = jnp.zeros((tq, 1), F32) if carries is None else carries[:, h * HEAD_DIM:h * HEAD_DIM + 1]
        pv = None
        for b in range(len(loads)):
            z, sp, tail = zs[b][h], sps[b][h], tails[b][h]
            loga = z - sp - tail - carry
            if diagonal:
                loga = jnp.where(valid, loga, NEG_BIG)
            term = _dot(jnp.exp(loga).astype(BF16), qkv[b][h][2])
            pv = term if pv is None else pv + term
            carry = carry + tail[:, 0:1] + sp[:, 0:1]
        pvs.append(pv)
        new.append(jnp.broadcast_to(carry, (tq, HEAD_DIM)))
    return jnp.concatenate(pvs, axis=1), jnp.concatenate(new, axis=1)


SB_T = 256
SB_HEADS_PER_STEP = 8
SB_STEP_BLOCKS = 1


def _head_slice(h):
    return slice(h * HEAD_DIM, (h + 1) * HEAD_DIM)


def _sb_prompt_kernel(q_ref, k_ref, v_ref, u_ref, o_ref, acc_scr, carry_scr):
    qi = pl.program_id(2)
    u = u_ref[...]

    def block(r, carries, diagonal):
        def load(h):
            hs = _head_slice(h)
            return q_ref[0, :, hs], k_ref[0, pl.ds(r, SB_T), hs], v_ref[0, pl.ds(r, SB_T), hs]
        return _sb_heads([load], SB_HEADS_PER_STEP, u, carries, diagonal)

    acc_scr[...], carry_scr[...] = block(pl.multiple_of(qi * SB_T, SB_T), None, True)

    def body(step, c):
        pv, carries = block(pl.multiple_of((qi - 1 - step) * SB_T, SB_T), carry_scr[...], False)
        acc_scr[...] += pv
        carry_scr[...] = carries
        return c

    lax.fori_loop(0, qi, body, 0)
    o_ref[0] = acc_scr[...].astype(BF16)


def _sb_prompt(proj3, u):
    b, s, _ = proj3.shape
    hp = SB_HEADS_PER_STEP
    w = hp * HEAD_DIM
    kcol, vcol = W_HEADS // w, 2 * W_HEADS // w
    return pl.pallas_call(
        _sb_prompt_kernel,
        grid=(b, N_HEADS // hp, s // SB_T),
        in_specs=[
            pl.BlockSpec((1, SB_T, w), lambda bi, h, qi: (bi, qi, h)),
            pl.BlockSpec((1, s, w), lambda bi, h, qi: (bi, 0, kcol + h)),
            pl.BlockSpec((1, s, w), lambda bi, h, qi: (bi, 0, vcol + h)),
            pl.BlockSpec((2 * SB_T, SB_T), lambda bi, h, qi: (0, 0)),
        ],
        out_specs=pl.BlockSpec((1, SB_T, w), lambda bi, h, qi: (bi, qi, h)),
        out_shape=jax.ShapeDtypeStruct((b, s, W_HEADS), BF16),
        scratch_shapes=[pltpu.VMEM((SB_T, w), F32), pltpu.VMEM((SB_T, w), F32)],
        compiler_params=_params("arbitrary", "arbitrary", "arbitrary"),
        name="sb_prompt",
    )(proj3, proj3, proj3, u)


def _sb_step_kernel(q_ref, kn_ref, vn_ref, ck_ref, cv_ref, u_ref, o_ref, acc_scr, carry_scr, *, tkb, t):
    j = pl.program_id(1)
    u = u_ref[...]

    @pl.when(j == 0)
    def _():
        def load(h):
            hs = _head_slice(h)
            return q_ref[0, :, hs], kn_ref[0, :, hs], vn_ref[0, :, hs]
        acc_scr[...], carry_scr[...] = _sb_heads([load], N_HEADS, _strict_lower_twice(t), None, True)

    def body(step, c):
        def loader(blk):
            r = pl.multiple_of((tkb - (step * SB_STEP_BLOCKS + blk + 1) * SB_T) * N_HEADS, SB_T * N_HEADS)

            def load(h):
                rows = pl.ds(r + h, SB_T, stride=N_HEADS)
                return (q_ref[0, :, _head_slice(h)], ck_ref[0, rows, :].astype(BF16),
                        cv_ref[0, rows, :].astype(BF16))
            return load

        pv, carries = _sb_heads([loader(blk) for blk in range(SB_STEP_BLOCKS)], N_HEADS, u, carry_scr[...], False)
        acc_scr[...] += pv
        carry_scr[...] = carries
        return c

    lax.fori_loop(0, tkb // (SB_T * SB_STEP_BLOCKS), body, 0)

    @pl.when(j == pl.num_programs(1) - 1)
    def _():
        o_ref[0] = acc_scr[...].astype(BF16)


def _sb_step(proj3, cache_k, cache_v, u):
    b, t, _ = proj3.shape
    p = cache_k.shape[1] // N_HEADS
    tkb = min(1024, p)
    nj = p // tkb
    cache_spec = pl.BlockSpec((1, tkb * N_HEADS, HEAD_DIM), lambda bi, j: (bi, nj - 1 - j, 0))
    return pl.pallas_call(
        functools.partial(_sb_step_kernel, tkb=tkb, t=t),
        grid=(b, nj),
        in_specs=[
            pl.BlockSpec((1, t, W_HEADS), lambda bi, j: (bi, 0, 0)),
            pl.BlockSpec((1, t, W_HEADS), lambda bi, j: (bi, 0, 1)),
            pl.BlockSpec((1, t, W_HEADS), lambda bi, j: (bi, 0, 2)),
            cache_spec, cache_spec,
            pl.BlockSpec((2 * SB_T, SB_T), lambda bi, j: (0, 0)),
        ],
        out_specs=pl.BlockSpec((1, t, W_HEADS), lambda bi, j: (bi, 0, 0)),
        out_shape=jax.ShapeDtypeStruct((b, t, W_HEADS), BF16),
        scratch_shapes=[pltpu.VMEM((t, W_HEADS), F32), pltpu.VMEM((t, W_HEADS), F32)],
        compiler_params=_params("arbitrary", "arbitrary"),
        name="sb_step",
    )(proj3, proj3, proj3, cache_k, cache_v, u)


BAND_QB = 4 * CHUNK
BAND_KB = 3


BIAS_NEAR = REL_CLIP // CHUNK + 1


def _bias_blocks_kernel(table_ref, o_ref):
    h = pl.program_id(0)
    i = lax.broadcasted_iota(jnp.int32, (CHUNK, CHUNK), 0)
    j = lax.broadcasted_iota(jnp.int32, (CHUNK, CHUNK), 1)
    for d in range(BIAS_NEAR):
        idx = jnp.clip(i - j + d * CHUNK, -REL_CLIP, REL_CLIP) + REL_CLIP

        def body(r, acc, idx=idx):
            return jnp.where(idx == r, table_ref[h, r], acc)

        lo = max(d * CHUNK - (CHUNK - 1), -REL_CLIP) + REL_CLIP
        hi = min(d * CHUNK + (CHUNK - 1), REL_CLIP) + REL_CLIP
        o_ref[0, d] = lax.fori_loop(lo, hi + 1, body, jnp.zeros((CHUNK, CHUNK), F32))


def _bias_blocks(table):
    nh = table.shape[0]
    return pl.pallas_call(
        _bias_blocks_kernel,
        grid=(nh,),
        in_specs=[pl.BlockSpec(memory_space=pltpu.SMEM)],
        out_specs=pl.BlockSpec((1, BIAS_NEAR, CHUNK, CHUNK), lambda h: (h, 0, 0, 0)),
        out_shape=jax.ShapeDtypeStruct((nh, BIAS_NEAR, CHUNK, CHUNK), F32),
        compiler_params=_params("arbitrary"),
        name="bias_blocks",
    )(table.astype(F32))


def _chunk_bias(blocks, far, dist):
    if dist < 0 or dist > BAND_LEFT_CHUNKS:
        return jnp.full(far.shape, NEG_BIG, F32)
    return blocks[:, dist] if dist < BIAS_NEAR else far


def _band_bias(blocks, table):
    far = jnp.broadcast_to(table[:, -1].astype(F32)[:, None, None], (table.shape[0], CHUNK, CHUNK))
    qc, kc = BAND_QB // CHUNK, BAND_KB * BAND_QB // CHUNK
    rows = [jnp.concatenate([_chunk_bias(blocks, far, r - c + BAND_LEFT_CHUNKS) for c in range(kc)], axis=2)
            for r in range(qc)]
    return jnp.concatenate(rows, axis=1)


def _band_step_bias(blocks, table, lb):
    far = jnp.broadcast_to(table[:, -1].astype(F32)[:, None, None], (table.shape[0], CHUNK, CHUNK))
    nc = lb // CHUNK
    return jnp.concatenate([blocks[:, nc - c] if nc - c < BIAS_NEAR else far for c in range(nc + 1)], axis=2)


def _softmax_pv_heads(scores, values):
    ps, dens = [], []
    for sc in scores:
        m = sc[0].max(axis=1, keepdims=True)
        for s in sc[1:]:
            m = jnp.maximum(m, s.max(axis=1, keepdims=True))
        p = [jnp.exp(s - m) for s in sc]
        den = p[0].sum(axis=1, keepdims=True)
        for pc in p[1:]:
            den = den + pc.sum(axis=1, keepdims=True)
        ps.append(p)
        dens.append(den)
    outs = []
    for p, den, vals in zip(ps, dens, values):
        num = _dot(p[0].astype(BF16), vals[0])
        for pc, v in zip(p[1:], vals[1:]):
            num = num + _dot(pc.astype(BF16), v)
        outs.append(num / den)
    return jnp.concatenate(outs, axis=1)


def _band_prompt_kernel(q_ref, k0_ref, k1_ref, k2_ref, v0_ref, v1_ref, v2_ref, bias_ref, o_ref):
    qi = pl.program_id(1)
    scores, values = [], []
    for h in range(N_HEADS):
        hs = _head_slice(h)
        q = q_ref[0, :, hs]
        sc = []
        for c, k_ref in enumerate((k0_ref, k1_ref, k2_ref)):
            s = _dot_nt(q, k_ref[0, :, hs]) + bias_ref[h, :, c * BAND_QB:(c + 1) * BAND_QB]
            if c < BAND_KB - 1:
                s = jnp.where(qi + c >= BAND_KB - 1, s, NEG_BIG)
            sc.append(s)
        scores.append(sc)
        values.append([v_ref[0, :, hs] for v_ref in (v0_ref, v1_ref, v2_ref)])
    o_ref[0] = _softmax_pv_heads(scores, values).astype(BF16)


def _band_prompt(proj3, bias):
    b, s, _ = proj3.shape
    qcol, kcol, vcol = 3, 4, 5

    def kv_spec(col, back):
        return pl.BlockSpec((1, BAND_QB, W_HEADS), lambda bi, qi: (bi, jnp.maximum(qi - back, 0), col))

    return pl.pallas_call(
        _band_prompt_kernel,
        grid=(b, s // BAND_QB),
        in_specs=[
            pl.BlockSpec((1, BAND_QB, W_HEADS), lambda bi, qi: (bi, qi, qcol)),
            kv_spec(kcol, 2), kv_spec(kcol, 1), kv_spec(kcol, 0),
            kv_spec(vcol, 2), kv_spec(vcol, 1), kv_spec(vcol, 0),
            pl.BlockSpec((N_HEADS, BAND_QB, BAND_KB * BAND_QB), lambda bi, qi: (0, 0, 0)),
        ],
        out_specs=pl.BlockSpec((1, BAND_QB, W_HEADS), lambda bi, qi: (bi, qi, 0)),
        out_shape=jax.ShapeDtypeStruct((b, s, W_HEADS), BF16),
        compiler_params=_params("arbitrary", "arbitrary"),
        name="band_prompt",
    )(proj3, proj3, proj3, proj3, proj3, proj3, proj3, bias)


def _band_step_kernel(q_ref, kn_ref, vn_ref, ck_ref, cv_ref, bias_ref, o_ref, *, lb):
    scores, values = [], []
    for h in range(N_HEADS):
        hs = _head_slice(h)
        q = q_ref[0, :, hs]
        rows = pl.ds(h, lb, stride=N_HEADS)
        scores.append([_dot_nt(q, ck_ref[0, rows, :].astype(BF16)) + bias_ref[h, :, :lb],
                       _dot_nt(q, kn_ref[0, :, hs]) + bias_ref[h, :, lb:]])
        values.append([cv_ref[0, rows, :].astype(BF16), vn_ref[0, :, hs]])
    o_ref[0] = _softmax_pv_heads(scores, values).astype(BF16)


def _band_step(proj3, cache_k, cache_v, bias):
    b, t, _ = proj3.shape
    lb = cache_k.shape[1] // N_HEADS
    cache_spec = pl.BlockSpec((1, lb * N_HEADS, HEAD_DIM), lambda bi: (bi, 0, 0))
    return pl.pallas_call(
        functools.partial(_band_step_kernel, lb=lb),
        grid=(b,),
        in_specs=[
            pl.BlockSpec((1, t, W_HEADS), lambda bi: (bi, 0, 3)),
            pl.BlockSpec((1, t, W_HEADS), lambda bi: (bi, 0, 4)),
            pl.BlockSpec((1, t, W_HEADS), lambda bi: (bi, 0, 5)),
            cache_spec, cache_spec,
            pl.BlockSpec((N_HEADS, t, lb + t), lambda bi: (0, 0, 0)),
        ],
        out_specs=pl.BlockSpec((1, t, W_HEADS), lambda bi: (bi, 0, 0)),
        out_shape=jax.ShapeDtypeStruct((b, t, W_HEADS), BF16),
        compiler_params=_params("arbitrary"),
        name="band_step",
    )(proj3, proj3, proj3, cache_k, cache_v, bias)


def _merge_kernel(osb_ref, obd_ref, wsb_ref, wbd_ref, ga_ref, gb_ref, o_ref):
    a = _dot(osb_ref[...], wsb_ref[...])
    b = _dot(obd_ref[...], wbd_ref[...])
    merged = _sigmoid(ga_ref[...].astype(F32)) * a + _sigmoid(gb_ref[...].astype(F32)) * b
    o_ref[...] = merged.astype(BF16)


def _merge(o_sb, o_band, w_sb_bf, w_band_bf, proj, tm):
    m = o_sb.shape[0]
    d = w_sb_bf.shape[1]
    tn = min(MATMUL_TN, d)
    ga_col = 6 * W_HEADS // tn
    gb_col = ga_col + d // tn
    return pl.pallas_call(
        _merge_kernel,
        grid=(m // tm, d // tn),
        in_specs=[
            pl.BlockSpec((tm, W_HEADS), lambda i, j: (i, 0)),
            pl.BlockSpec((tm, W_HEADS), lambda i, j: (i, 0)),
            pl.BlockSpec((W_HEADS, tn), lambda i, j: (0, j)),
            pl.BlockSpec((W_HEADS, tn), lambda i, j: (0, j)),
            pl.BlockSpec((tm, tn), lambda i, j: (i, ga_col + j)),
            pl.BlockSpec((tm, tn), lambda i, j: (i, gb_col + j)),
        ],
        out_specs=pl.BlockSpec((tm, tn), lambda i, j: (i, j)),
        out_shape=jax.ShapeDtypeStruct((m, d), BF16),
        compiler_params=_params("arbitrary", "arbitrary"),
        name="merge",
    )(o_sb, o_band, w_sb_bf, w_band_bf, proj, proj)


def _outproj_kernel(m_ref, w_ref, x_ref, gate_ref, o_ref, *, gb):
    acc = _dot(m_ref[...], w_ref[...])
    acc = acc.reshape(gb, ROW_GROUP, acc.shape[1])
    o_ref[...] = x_ref[...] + gate_ref[0] * acc


def _outproj(merged, w_out_bf, x3, mods, gb):
    ng, _, d = x3.shape
    tm = gb * ROW_GROUP
    tn = min(MATMUL_TN, d)
    return pl.pallas_call(
        functools.partial(_outproj_kernel, gb=gb),
        grid=(ng // gb, d // tn),
        in_specs=[
            pl.BlockSpec((tm, d), lambda i, j: (i, 0)),
            pl.BlockSpec((d, tn), lambda i, j: (0, j)),
            pl.BlockSpec((gb, ROW_GROUP, tn), lambda i, j: (i, 0, j)),
            pl.BlockSpec((1, gb, 1, tn), lambda i, j: (MOD_GATE_M, i, 0, j)),
        ],
        out_specs=pl.BlockSpec((gb, ROW_GROUP, tn), lambda i, j: (i, 0, j)),
        out_shape=jax.ShapeDtypeStruct(x3.shape, F32),
        compiler_params=_params("arbitrary", "arbitrary"),
        name="outproj",
    )(merged, w_out_bf, x3, mods)


GATHER_HALF = 128


def _tile_rows(d):
    assert d % V7X_LANES == 0
    return d // V7X_LANES


def _store_token_tiles(ref, first_token, x):
    n, d = x.shape
    s_per = _tile_rows(d)
    for c in range(s_per):
        ref[pl.ds(first_token * s_per + c, n, stride=s_per), :] = x[:, c * V7X_LANES:(c + 1) * V7X_LANES]


def _load_token_tiles(ref, n, d):
    s_per = _tile_rows(d)
    return jnp.concatenate([ref[pl.ds(c, n, stride=s_per), :] for c in range(s_per)], axis=1)


def _start_token_gather(table_hbm, idx_ref, first, buf, sem_of, n, s_per):
    for half in range(n // GATHER_HALF):
        def issue(i, carry, half=half):
            row = half * GATHER_HALF + i
            pltpu.make_async_copy(table_hbm.at[pl.ds(idx_ref[first + row] * s_per, s_per)],
                                  buf.at[pl.ds(row * s_per, s_per)], sem_of(half)).start()
            return carry
        lax.fori_loop(0, GATHER_HALF, issue, 0, unroll=8)


def _wait_token_gather(buf, sem_of, n, s_per):
    for half in range(n // GATHER_HALF):
        part = buf.at[pl.ds(half * GATHER_HALF * s_per, GATHER_HALF * s_per)]
        pltpu.make_async_copy(part, part, sem_of(half)).wait()


ROUTE_E1, ROUTE_E2, ROUTE_W1, ROUTE_W2 = 0, 1, 2, 3


def _router_kernel(xa_ref, sca_ref, sha_ref, xb_ref, scb_ref, shb_ref, g_ref, whi_ref, wlo_ref, b_ref,
                   h_ref, route_ref, hi_scr, lo_scr, *, gb, na):
    def fill(x_ref, sc_ref, sh_ref):
        def body(s, carry):
            h = _modulated_norm(x_ref[s], g_ref[...], sc_ref[0, s], sh_ref[0, s])
            hi = h.astype(BF16)
            first = pl.multiple_of(s * ROW_GROUP, ROW_GROUP)
            _store_token_tiles(h_ref, first, h)
            hi_scr[pl.ds(first, ROW_GROUP), :] = hi
            lo_scr[pl.ds(first, ROW_GROUP), :] = (h - hi.astype(F32)).astype(BF16)
            return carry
        lax.fori_loop(0, gb, body, 0)

    @pl.when(pl.program_id(0) < na)
    def _():
        fill(xa_ref, sca_ref, sha_ref)

    @pl.when(pl.program_id(0) >= na)
    def _():
        fill(xb_ref, scb_ref, shb_ref)

    hi = hi_scr[...]
    logits = _dot(hi, whi_ref[...]) + _dot(hi, wlo_ref[...]) + _dot(lo_scr[...], whi_ref[...]) + b_ref[...]
    lane = lax.broadcasted_iota(jnp.int32, logits.shape, 1)
    ninf = -jnp.inf

    lane_f = lane.astype(F32)

    def first_max(vals):
        mx = vals.max(axis=1, keepdims=True)
        idx = jnp.where(vals == mx, lane_f, float(V7X_LANES)).min(axis=1, keepdims=True)
        return mx, idx

    gl = jnp.where(lane < N_GROUPS, logits, ninf)
    gmax, gidx = first_max(gl)
    g_weight = 1.0 / jnp.exp(gl - gmax).sum(axis=1, keepdims=True)
    lo_lane = N_GROUPS + EXPERTS_PER_GROUP * gidx
    el = jnp.where(lane_f >= lo_lane, jnp.where(lane_f < lo_lane + EXPERTS_PER_GROUP, logits, ninf), ninf)
    m1, i1 = first_max(el)
    m2, i2 = first_max(jnp.where(lane_f == i1, ninf, el))
    e21 = jnp.exp(m2 - m1)
    p1 = 1.0 / (1.0 + e21)
    p2 = e21 / (1.0 + e21)
    route = jnp.where(lane == ROUTE_E1, i1 - N_GROUPS,
                      jnp.where(lane == ROUTE_E2, i2 - N_GROUPS,
                                jnp.where(lane == ROUTE_W1, g_weight * p1,
                                          jnp.where(lane == ROUTE_W2, g_weight * p2, 0.0))))
    route_ref[...] = route


def _router(xa3, mods_a, xb3, mods_b, g, w_hi, w_lo, bias, gb):
    na, nb = xa3.shape[0] // gb, xb3.shape[0] // gb
    d = xa3.shape[2]
    n_tokens = (na + nb) * gb * ROW_GROUP
    tm = gb * ROW_GROUP
    s_per = _tile_rows(d)

    def first(i):
        return jnp.minimum(i, na - 1)

    def second(i):
        return jnp.maximum(i - na, 0)

    def mod_spec(which, blk):
        return pl.BlockSpec((1, gb, 1, d), lambda i: (which, blk(i), 0, 0))

    w_spec = pl.BlockSpec((d, V7X_LANES), lambda i: (0, 0))
    return pl.pallas_call(
        functools.partial(_router_kernel, gb=gb, na=na),
        grid=(na + nb,),
        in_specs=[
            pl.BlockSpec((gb, ROW_GROUP, d), lambda i: (first(i), 0, 0)),
            mod_spec(MOD_SCALE_F, first), mod_spec(MOD_SHIFT_F, first),
            pl.BlockSpec((gb, ROW_GROUP, d), lambda i: (second(i), 0, 0)),
            mod_spec(MOD_SCALE_F, second), mod_spec(MOD_SHIFT_F, second),
            pl.BlockSpec((1, d), lambda i: (0, 0)),
            w_spec, w_spec,
            pl.BlockSpec((1, V7X_LANES), lambda i: (0, 0)),
        ],
        out_specs=[
            pl.BlockSpec((tm * s_per, V7X_LANES), lambda i: (i, 0)),
            pl.BlockSpec((tm, V7X_LANES), lambda i: (i, 0)),
        ],
        out_shape=[jax.ShapeDtypeStruct((n_tokens * s_per, V7X_LANES), F32),
                   jax.ShapeDtypeStruct((n_tokens, V7X_LANES), F32)],
        scratch_shapes=[pltpu.VMEM((tm, d), BF16), pltpu.VMEM((tm, d), BF16)],
        compiler_params=_params("arbitrary"),
        name="router",
    )(xa3, mods_a, mods_a, xb3, mods_b, mods_b, g, w_hi, w_lo, bias)


def _moe_kernel(te_ref, nv_ref, src_ref, h_hbm, wg_ref, wu_ref, wd_ref, o_ref, xbuf, sems, wg_scr, wu_scr, wd_scr):
    t = pl.program_id(0)
    n_valid = nv_ref[0]
    d = wg_scr.shape[0]
    s_per = _tile_rows(d)

    def start(tile, slot):
        _start_token_gather(h_hbm, src_ref, tile * MOE_TILE, xbuf.at[slot], lambda half: sems.at[slot, half],
                            MOE_TILE, s_per)

    @pl.when(t == 0)
    def _():
        start(0, 0)

    @pl.when(t + 1 < n_valid)
    def _():
        start(t + 1, (t + 1) % 2)

    changed = te_ref[t] != te_ref[jnp.maximum(t - 1, 0)]

    @pl.when((t == 0) | changed)
    def _():
        wg_scr[...] = wg_ref[0].astype(BF16)
        wu_scr[...] = wu_ref[0].astype(BF16)
        wd_scr[...] = wd_ref[0].astype(BF16)

    @pl.when(t < n_valid)
    def _():
        slot = t % 2
        _wait_token_gather(xbuf.at[slot], lambda half: sems.at[slot, half], MOE_TILE, s_per)
        x = _load_token_tiles(xbuf.at[slot], MOE_TILE, d).astype(BF16)
        g = _dot(x, wg_scr[...])
        u = _dot(x, wu_scr[...])
        hidden = (g * _sigmoid(g)) * u
        _store_token_tiles(o_ref, 0, _dot(hidden.astype(BF16), wd_scr[...]))

    @pl.when(t >= n_valid)
    def _():
        o_ref[...] = jnp.zeros(o_ref.shape, F32)


def _moe(tile_expert, n_valid, src_tok, h_tiles, w_gate, w_up, w_down):
    _, d, f = w_gate.shape
    s_per = _tile_rows(d)
    n_tiles = src_tok.shape[0] // MOE_TILE
    halves = MOE_TILE // GATHER_HALF
    grid_spec = pltpu.PrefetchScalarGridSpec(
        num_scalar_prefetch=3,
        grid=(n_tiles,),
        in_specs=[
            pl.BlockSpec(memory_space=pl.ANY),
            pl.BlockSpec((1, d, f), lambda t, te, nv, src: (te[t], 0, 0)),
            pl.BlockSpec((1, d, f), lambda t, te, nv, src: (te[t], 0, 0)),
            pl.BlockSpec((1, f, d), lambda t, te, nv, src: (te[t], 0, 0)),
        ],
        out_specs=pl.BlockSpec((MOE_TILE * s_per, V7X_LANES), lambda t, te, nv, src: (t, 0)),
        scratch_shapes=[pltpu.VMEM((2, MOE_TILE * s_per, V7X_LANES), F32), pltpu.SemaphoreType.DMA((2, halves)),
                        pltpu.VMEM((d, f), BF16), pltpu.VMEM((d, f), BF16), pltpu.VMEM((f, d), BF16)],
    )
    return pl.pallas_call(
        _moe_kernel,
        grid_spec=grid_spec,
        out_shape=jax.ShapeDtypeStruct((n_tiles * MOE_TILE * s_per, V7X_LANES), F32),
        compiler_params=_params("arbitrary"),
        name="moe",
    )(tile_expert, n_valid, src_tok, h_tiles, w_gate, w_up, w_down)


def _dispatch(route):
    n = route.shape[0]
    e = jnp.concatenate([route[:, ROUTE_E1], route[:, ROUTE_E2]]).astype(jnp.int32)
    tok = jnp.concatenate([jnp.arange(n, dtype=jnp.int32)] * 2)
    onehot = (e[:, None] == jnp.arange(N_EXPERTS, dtype=jnp.int32)[None, :]).astype(jnp.int32)
    before = jnp.cumsum(onehot, axis=0) - onehot
    rank = jnp.sum(before * onehot, axis=1)
    counts = jnp.sum(onehot, axis=0)
    padded = ((counts + MOE_TILE - 1) // MOE_TILE) * MOE_TILE
    ends = jnp.cumsum(padded)
    pos = (ends - padded)[e] + rank
    n_tiles = -(-(2 * n + N_EXPERTS * (MOE_TILE - 1)) // (4 * MOE_TILE)) * 4
    tile_start = jnp.arange(n_tiles, dtype=jnp.int32) * MOE_TILE
    n_valid = (ends[-1] // MOE_TILE).astype(jnp.int32)
    tile_expert = jnp.sum((ends[None, :] <= tile_start[:, None]).astype(jnp.int32), axis=1)
    last_expert = tile_expert[jnp.maximum(n_valid - 1, 0)]
    tile_expert = jnp.where(tile_start < ends[-1], tile_expert, last_expert)
    src_tok = jnp.zeros((n_tiles * MOE_TILE,), jnp.int32).at[pos].set(tok)
    return pos[:n], pos[n:], src_tok, tile_expert, n_valid.reshape(1)


def _final_kernel(p1_ref, p2_ref, x_ref, y_hbm, route_ref, gate_ref, o_ref, ybuf, sems, *, first_token):
    i = pl.program_id(0)
    gb, rg, d = x_ref.shape
    tm = gb * rg
    s_per = _tile_rows(d)

    def start(step, slot):
        for which, pos_ref in enumerate((p1_ref, p2_ref)):
            _start_token_gather(y_hbm, pos_ref, first_token + step * tm, ybuf.at[slot, which],
                                lambda half, which=which: sems.at[slot, which, half], tm, s_per)

    @pl.when(i == 0)
    def _():
        start(0, 0)

    @pl.when(i + 1 < pl.num_programs(0))
    def _():
        start(i + 1, (i + 1) % 2)

    slot = i % 2
    for which in range(2):
        _wait_token_gather(ybuf.at[slot, which], lambda half, which=which: sems.at[slot, which, half], tm, s_per)
    r = route_ref[...]
    y = (r[:, ROUTE_W1:ROUTE_W1 + 1] * _load_token_tiles(ybuf.at[slot, 0], tm, d)
         + r[:, ROUTE_W2:ROUTE_W2 + 1] * _load_token_tiles(ybuf.at[slot, 1], tm, d))
    o_ref[...] = x_ref[...] + gate_ref[0] * y.reshape(gb, rg, d)


def _final(x3, y_tiles, pos1, pos2, route, first_token, mods, gb):
    ng, _, d = x3.shape
    tm = gb * ROW_GROUP
    s_per = _tile_rows(d)
    assert first_token % tm == 0
    b0 = first_token // tm
    blk = pl.BlockSpec((gb, ROW_GROUP, d), lambda i, p1, p2: (i, 0, 0))
    grid_spec = pltpu.PrefetchScalarGridSpec(
        num_scalar_prefetch=2,
        grid=(ng // gb,),
        in_specs=[
            blk,
            pl.BlockSpec(memory_space=pl.ANY),
            pl.BlockSpec((tm, V7X_LANES), lambda i, p1, p2: (b0 + i, 0)),
            pl.BlockSpec((1, gb, 1, d), lambda i, p1, p2: (MOD_GATE_F, i, 0, 0)),
        ],
        out_specs=blk,
        scratch_shapes=[pltpu.VMEM((2, 2, tm * s_per, V7X_LANES), F32),
                        pltpu.SemaphoreType.DMA((2, 2, tm // GATHER_HALF))],
    )
    return pl.pallas_call(
        functools.partial(_final_kernel, first_token=first_token),
        grid_spec=grid_spec,
        out_shape=jax.ShapeDtypeStruct(x3.shape, F32),
        compiler_params=_params("arbitrary"),
        name="final",
    )(pos1, pos2, x3, y_tiles, route, mods)


def _layer(xp, xs, csk, csv, cbk, cbv, c_prompt, c_sample, norm_mix, norm_ffn, w_ada, b_ada, w_in, q_norm,
           k_norm, rel_table, w_proj_sb, w_proj_band, w_out, w_rg, b_rg, w_re, b_re, w_gate, w_up, w_down):
    bp, sp_len, d = xp.shape
    bs, ts, _ = xs.shape
    d_in = w_in.shape[1]
    assert sp_len % BAND_QB == 0 and ts == ROW_GROUP and sp_len % ROW_GROUP == 0

    n_c = bp + bs
    c_pad = jnp.concatenate([c_prompt, c_sample, jnp.zeros((-n_c % 8, d), F32)], axis=0)
    mod = _ada(c_pad, w_ada, b_ada)
    gp = sp_len // ROW_GROUP
    mod_p = jnp.repeat(mod[:, :bp], gp, axis=1)[:, :, None, :]
    mod_s = mod[:, bp:n_c][:, :, None, :]

    w_in_bf = w_in.astype(BF16)
    w_sb_bf = w_proj_sb.astype(BF16)
    w_band_bf = w_proj_band.astype(BF16)
    w_out_bf = w_out.astype(BF16)
    qn = q_norm.reshape(1, W_HEADS)
    kn = k_norm.reshape(1, W_HEADS)
    g_mix = norm_mix.reshape(1, d)
    g_ffn = norm_ffn.reshape(1, d)
    lb = cbk.shape[1]
    assert lb % CHUNK == 0 and ts == CHUNK
    bias_blocks = _bias_blocks(rel_table)
    bias = _band_bias(bias_blocks, rel_table)
    bias_step = _band_step_bias(bias_blocks, rel_table, lb)
    u = _strict_lower_twice(SB_T)

    w_router = jnp.zeros((d, V7X_LANES), F32).at[:, :N_GROUPS].set(w_rg)
    w_router = w_router.at[:, N_GROUPS:N_GROUPS + N_EXPERTS].set(w_re)
    w_router_hi = w_router.astype(BF16)
    w_router_lo = (w_router - w_router_hi.astype(F32)).astype(BF16)
    b_router = jnp.zeros((1, V7X_LANES), F32).at[0, :N_GROUPS].set(b_rg)
    b_router = b_router.at[0, N_GROUPS:N_GROUPS + N_EXPERTS].set(b_re.reshape(-1))

    xp3 = xp.reshape(bp * gp, ROW_GROUP, d)
    xs3 = xs.reshape(bs, ROW_GROUP, d)
    gb_p = min(ROWWISE_GROUPS, bp * gp)
    gb_s = min(ROWWISE_GROUPS, bs)

    def mixer(x3, mods, gb, attend):
        proj, ka, va, kb, vb = _proj(x3, g_mix, mods, w_in_bf, qn, kn, gb)
        o_sb, o_band = attend(proj)
        merged = _merge(o_sb, o_band, w_sb_bf, w_band_bf, proj, gb * ROW_GROUP)
        x1 = _outproj(merged, w_out_bf, x3, mods, gb)
        return x1, ka, va, kb, vb

    def attend_prompt(proj):
        p3 = proj.reshape(bp, sp_len, d_in)
        return (_sb_prompt(p3, u).reshape(bp * sp_len, W_HEADS),
                _band_prompt(p3, bias).reshape(bp * sp_len, W_HEADS))

    def attend_sample(proj):
        p3 = proj.reshape(bs, ts, d_in)
        def rows(cache):
            return cache.reshape(bs, -1, HEAD_DIM)

        o_sb = _sb_step(p3, rows(csk), rows(csv), u)
        o_band = _band_step(p3, rows(cbk), rows(cbv), bias_step)
        return o_sb.reshape(bs * ts, W_HEADS), o_band.reshape(bs * ts, W_HEADS)

    x1p, kap, vap, kbp, vbp = mixer(xp3, mod_p, min(MATMUL_GROUPS, bp * gp), attend_prompt)
    x1s, kas, vas, kbs, vbs = mixer(xs3, mod_s, min(MATMUL_GROUPS, bs), attend_sample)

    gb_r = min(gb_p, gb_s)
    h2, route = _router(x1p, mod_p, x1s, mod_s, g_ffn, w_router_hi, w_router_lo, b_router, gb_r)
    pos1, pos2, src_tok, tile_expert, n_valid = _dispatch(route)
    y_sorted = _moe(tile_expert, n_valid, src_tok, h2, w_gate, w_up, w_down)
    gb_f = min(FINAL_GROUPS, bs)
    yp = _final(x1p, y_sorted, pos1, pos2, route, 0, mod_p, gb_f).reshape(bp, sp_len, d)
    ys = _final(x1s, y_sorted, pos1, pos2, route, bp * sp_len, mod_s, gb_f).reshape(bs, ts, d)

    def heads(a, b):
        return a.reshape(b, -1, N_HEADS, HEAD_DIM)

    n_band = min(BAND_PAST, sp_len)
    return (yp, ys, heads(kap, bp), heads(vap, bp), heads(kbp, bp)[:, -n_band:], heads(vbp, bp)[:, -n_band:],
            heads(kas, bs), heads(vas, bs), heads(kbs, bs), heads(vbs, bs))


def kernel(x_prompt, x_sample, cache_sb_k, cache_sb_v, cache_band_k, cache_band_v, c_prompt, c_sample, norm_mix, norm_ffn, w_ada, b_ada, w_in, q_norm_band, k_norm_band, rel_bias_band, w_proj_sb, w_proj_band, w_out, w_router_group, b_router_group, w_router_expert, b_router_expert, w_gate, w_up, w_down):
    depth = w_in.shape[0]
    xp, xs = x_prompt, x_sample
    outs = [[] for _ in range(8)]
    for l in range(depth):
        res = _layer(xp, xs, cache_sb_k[l], cache_sb_v[l], cache_band_k[l], cache_band_v[l], c_prompt, c_sample,
                     norm_mix[l], norm_ffn[l], w_ada[l], b_ada[l], w_in[l], q_norm_band[l], k_norm_band[l],
                     rel_bias_band[l], w_proj_sb[l], w_proj_band[l], w_out[l], w_router_group[l],
                     b_router_group[l], w_router_expert[l], b_router_expert[l], w_gate[l], w_up[l], w_down[l])
        xp, xs = res[0], res[1]
        for acc, r in zip(outs, res[2:]):
            acc.append(r)
    return (xp, xs) + tuple(jnp.stack(o, axis=0) for o in outs)
```

```python
import functools

import jax
import jax.numpy as jnp
from jax import lax
from jax.experimental import pallas as pl
from jax.experimental.pallas import tpu as pltpu

F32 = jnp.float32
BF16 = jnp.bfloat16

EPS = 1e-6
HEAD_DIM = 128
N_HEADS = 8
W_HEADS = N_HEADS * HEAD_DIM
CHUNK = 64
BAND_LEFT_CHUNKS = 8
BAND_PAST = BAND_LEFT_CHUNKS * CHUNK
REL_CLIP = 128
N_GROUPS = 4
EXPERTS_PER_GROUP = 8
N_EXPERTS = N_GROUPS * EXPERTS_PER_GROUP
QK_SCALE = HEAD_DIM ** -0.5
NEG_BIG = -1e30

V7X_LANES = 128
V7X_VMEM_LIMIT = 56 * 1024 * 1024
ROW_GROUP = 64
MOE_TILE = 256
MOE_SUB = 128
MATMUL_GROUPS = 16
MATMUL_TN = 1024
PROJ_TN = 1024
ROWWISE_GROUPS = 8
FINAL_GROUPS = 4


def _params(*sem):
    return pltpu.CompilerParams(dimension_semantics=sem, vmem_limit_bytes=V7X_VMEM_LIMIT)


def _sigmoid(x):
    return 1.0 / (1.0 + jnp.exp(-x))


def _dot(a, b):
    return jnp.dot(a, b, preferred_element_type=F32)


def _dot_nt(a, b):
    return lax.dot_general(a, b, (((1,), (1,)), ((), ())), preferred_element_type=F32)


def _ada_kernel(c_ref, w_ref, b_ref, o_ref):
    c = c_ref[...]
    a = (c * _sigmoid(c)).astype(BF16)
    o_ref[0] = _dot(a, w_ref[...].astype(BF16)) + b_ref[...]


def _ada(c_pad, w_ada, b_ada):
    r, d = c_pad.shape
    tn = min(1024, d)
    per = d // tn
    return pl.pallas_call(
        _ada_kernel,
        grid=(6 * per,),
        in_specs=[
            pl.BlockSpec((r, d), lambda j: (0, 0)),
            pl.BlockSpec((d, tn), lambda j: (0, j)),
            pl.BlockSpec((1, tn), lambda j: (0, j)),
        ],
        out_specs=pl.BlockSpec((1, r, tn), lambda j: (j // per, 0, j % per)),
        out_shape=jax.ShapeDtypeStruct((6, r, d), F32),
        compiler_params=_params("arbitrary"),
        name="ada",
    )(c_pad, w_ada, b_ada.reshape(1, 6 * d))


def _modulated_norm(x, g, scale, shift):
    ms = jnp.mean(x * x, axis=-1, keepdims=True)
    return (x * lax.rsqrt(ms + EPS) * g) * (1.0 + scale) + shift


def _head_norm(a, gain_ref):
    outs = []
    for hh in range(a.shape[1] // HEAD_DIM):
        blk = a[:, hh * HEAD_DIM:(hh + 1) * HEAD_DIM]
        ms = jnp.mean(blk * blk, axis=-1, keepdims=True)
        outs.append(blk * lax.rsqrt(ms + EPS) * gain_ref[:, hh * HEAD_DIM:(hh + 1) * HEAD_DIM])
    return jnp.concatenate(outs, axis=1)


def _proj_kernel(x_ref, g_ref, sc_ref, sh_ref, w_ref, qn_ref, kn_ref,
                 proj_ref, ka_ref, va_ref, kb_ref, vb_ref, h_scr, *, gb, per):
    j = pl.program_id(1)

    @pl.when(j == 0)
    def _():
        def body(s, carry):
            h = _modulated_norm(x_ref[s], g_ref[...], sc_ref[0, s], sh_ref[0, s])
            h_scr[pl.ds(pl.multiple_of(s * ROW_GROUP, ROW_GROUP), ROW_GROUP), :] = h.astype(BF16)
            return carry
        lax.fori_loop(0, gb, body, 0)

    acc = _dot(h_scr[...], w_ref[...])
    sec = j // per

    @pl.when(sec == 0)
    def _():
        proj_ref[...] = (acc * QK_SCALE).astype(BF16)

    @pl.when(sec == 1)
    def _():
        proj_ref[...] = acc.astype(BF16)
        ka_ref[...] = acc

    @pl.when(sec == 2)
    def _():
        proj_ref[...] = acc.astype(BF16)
        va_ref[...] = acc

    @pl.when(sec == 3)
    def _():
        proj_ref[...] = (_head_norm(acc, qn_ref) * QK_SCALE).astype(BF16)

    @pl.when(sec == 4)
    def _():
        n = _head_norm(acc, kn_ref)
        proj_ref[...] = n.astype(BF16)
        kb_ref[...] = n

    @pl.when(sec == 5)
    def _():
        proj_ref[...] = acc.astype(BF16)
        vb_ref[...] = acc

    @pl.when(sec >= 6)
    def _():
        proj_ref[...] = acc.astype(BF16)


MOD_SHIFT_M, MOD_SCALE_M, MOD_GATE_M, MOD_SHIFT_F, MOD_SCALE_F, MOD_GATE_F = range(6)


def _proj(x3, g, mods, w_in_bf, qn, kn, gb):
    ng, _, d = x3.shape
    d_in = w_in_bf.shape[1]
    m = ng * ROW_GROUP
    tm = gb * ROW_GROUP
    tn = PROJ_TN
    per = W_HEADS // tn

    def sect(s):
        return pl.BlockSpec((tm, tn), lambda i, j: (i, jnp.clip(j - s * per, 0, per - 1)),
                            pipeline_mode=pl.Buffered(1))

    def mod_spec(which):
        return pl.BlockSpec((1, gb, 1, d), lambda i, j: (which, i, 0, 0))

    kv_shape = jax.ShapeDtypeStruct((m, W_HEADS), F32)
    return pl.pallas_call(
        functools.partial(_proj_kernel, gb=gb, per=per),
        grid=(ng // gb, d_in // tn),
        in_specs=[
            pl.BlockSpec((gb, ROW_GROUP, d), lambda i, j: (i, 0, 0)),
            pl.BlockSpec((1, d), lambda i, j: (0, 0)),
            mod_spec(MOD_SCALE_M), mod_spec(MOD_SHIFT_M),
            pl.BlockSpec((d, tn), lambda i, j: (0, j)),
            pl.BlockSpec((1, tn), lambda i, j: (0, jnp.clip(j - 3 * per, 0, per - 1))),
            pl.BlockSpec((1, tn), lambda i, j: (0, jnp.clip(j - 4 * per, 0, per - 1))),
        ],
        out_specs=[
            pl.BlockSpec((tm, tn), lambda i, j: (i, j)),
            sect(1), sect(2), sect(4), sect(5),
        ],
        out_shape=[jax.ShapeDtypeStruct((m, d_in), BF16), kv_shape, kv_shape, kv_shape, kv_shape],
        scratch_shapes=[pltpu.VMEM((tm, d), BF16)],
        compiler_params=_params("arbitrary", "arbitrary"),
        name="proj",
    )(x3, g, mods, mods, w_in_bf, qn, kn)


def _strict_lower_twice(n):
    row = lax.broadcasted_iota(jnp.int32, (2 * n, n), 0)
    col = lax.broadcasted_iota(jnp.int32, (2 * n, n), 1)
    row = jnp.where(row >= n, row - n, row)
    return jnp.where(row > col, 1.0, 0.0).astype(BF16)


def _sb_scores(load_qk, n_heads):
    return jnp.concatenate([_dot_nt(*load_qk(h)) for h in range(n_heads)], axis=1)


def _sb_apply(z_all, load_v, n_heads, u, carries, diagonal):
    tq = z_all.shape[0]
    tk = z_all.shape[1] // n_heads
    zs = [z_all[:, h * tk:(h + 1) * tk] for h in range(n_heads)]
    if diagonal:
        row = lax.broadcasted_iota(jnp.int32, (tq, tk), 0)
        col = lax.broadcasted_iota(jnp.int32, (tq, tk), 1)
        valid = col < row
    sps, tails = [], []
    for z in zs:
        sp = jnp.maximum(z, 0.0) + jnp.log(1.0 + jnp.exp(-jnp.abs(z)))
        if diagonal:
            sp = jnp.where(valid, sp, 0.0)
        hi = sp.astype(BF16)
        lo = (sp - hi.astype(F32)).astype(BF16)
        sps.append(sp)
        tails.append(_dot(jnp.concatenate([hi, lo], axis=1), u))
    pvs, new = [], []
    for h, (z, sp, tail) in enumerate(zip(zs, sps, tails)):
        carry = jnp.zeros((tq, 1), F32) if carries is None else carries[:, h * HEAD_DIM:h * HEAD_DIM + 1]
        loga = z - sp - tail - carry
        if diagonal:
            loga = jnp.where(valid, loga, NEG_BIG)
        pvs.append(_dot(jnp.exp(loga).astype(BF16), load_v(h)))
        new.append(jnp.broadcast_to(carry + tail[:, 0:1] + sp[:, 0:1], (tq, HEAD_DIM)))
    return jnp.concatenate(pvs, axis=1), jnp.concatenate(new, axis=1)


SB_T = 256
SB_HEADS_PER_STEP = 8


def _head_slice(h):
    return slice(h * HEAD_DIM, (h + 1) * HEAD_DIM)


def _sb_prompt_kernel(q_ref, k_ref, v_ref, u_ref, o_ref, acc_scr, carry_scr):
    qi = pl.program_id(2)
    u = u_ref[...]
    nh = SB_HEADS_PER_STEP

    def block(blk, carries, diagonal):
        r = pl.multiple_of(blk * SB_T, SB_T)
        z_all = _sb_scores(lambda h: (q_ref[0, :, _head_slice(h)], k_ref[0, pl.ds(r, SB_T), _head_slice(h)]), nh)
        return _sb_apply(z_all, lambda h: v_ref[0, pl.ds(r, SB_T), _head_slice(h)], nh, u, carries, diagonal)

    acc_scr[...], carry_scr[...] = block(qi, None, True)

    def body(step, c):
        pv, carries = block(qi - 1 - step, carry_scr[...], False)
        acc_scr[...] += pv
        carry_scr[...] = carries
        return c

    lax.fori_loop(0, qi, body, 0)
    o_ref[0] = acc_scr[...].astype(BF16)


def _sb_prompt(proj3, u):
    b, s, _ = proj3.shape
    hp = SB_HEADS_PER_STEP
    w = hp * HEAD_DIM
    kcol, vcol = W_HEADS // w, 2 * W_HEADS // w
    return pl.pallas_call(
        _sb_prompt_kernel,
        grid=(b, N_HEADS // hp, s // SB_T),
        in_specs=[
            pl.BlockSpec((1, SB_T, w), lambda bi, h, qi: (bi, qi, h)),
            pl.BlockSpec((1, s, w), lambda bi, h, qi: (bi, 0, kcol + h)),
            pl.BlockSpec((1, s, w), lambda bi, h, qi: (bi, 0, vcol + h)),
            pl.BlockSpec((2 * SB_T, SB_T), lambda bi, h, qi: (0, 0)),
        ],
        out_specs=pl.BlockSpec((1, SB_T, w), lambda bi, h, qi: (bi, qi, h)),
        out_shape=jax.ShapeDtypeStruct((b, s, W_HEADS), BF16),
        scratch_shapes=[pltpu.VMEM((SB_T, w), F32), pltpu.VMEM((SB_T, w), F32)],
        compiler_params=_params("arbitrary", "arbitrary", "arbitrary"),
        name="sb_prompt",
    )(proj3, proj3, proj3, u)


def _sb_step_kernel(q_ref, kn_ref, vn_ref, ck_ref, cv_ref, u_ref, o_ref, acc_scr, carry_scr, *, tkb, t):
    j = pl.program_id(1)
    u = u_ref[...]

    @pl.when(j == 0)
    def _():
        z_new = _sb_scores(lambda h: (q_ref[0, :, _head_slice(h)], kn_ref[0, :, _head_slice(h)]), N_HEADS)
        acc_scr[...], carry_scr[...] = _sb_apply(z_new, lambda h: vn_ref[0, :, _head_slice(h)], N_HEADS,
                                                 _strict_lower_twice(t), None, True)

    def rows(sub, h):
        first = pl.multiple_of((tkb - (sub + 1) * SB_T) * N_HEADS, SB_T * N_HEADS)
        return pl.ds(first + h, SB_T, stride=N_HEADS)

    def body(sub, c):
        z_all = _sb_scores(lambda h: (q_ref[0, :, _head_slice(h)], ck_ref[0, rows(sub, h), :].astype(BF16)), N_HEADS)
        pv, carries = _sb_apply(z_all, lambda h: cv_ref[0, rows(sub, h), :].astype(BF16), N_HEADS, u,
                                carry_scr[...], False)
        acc_scr[...] += pv
        carry_scr[...] = carries
        return c

    lax.fori_loop(0, tkb // SB_T, body, 0)

    @pl.when(j == pl.num_programs(1) - 1)
    def _():
        o_ref[0] = acc_scr[...].astype(BF16)


def _sb_step(proj3, cache_k, cache_v, u):
    b, t, _ = proj3.shape
    p = cache_k.shape[1] // N_HEADS
    tkb = min(1024, p)
    nj = p // tkb
    cache_spec = pl.BlockSpec((1, tkb * N_HEADS, HEAD_DIM), lambda bi, j: (bi, nj - 1 - j, 0))
    return pl.pallas_call(
        functools.partial(_sb_step_kernel, tkb=tkb, t=t),
        grid=(b, nj),
        in_specs=[
            pl.BlockSpec((1, t, W_HEADS), lambda bi, j: (bi, 0, 0)),
            pl.BlockSpec((1, t, W_HEADS), lambda bi, j: (bi, 0, 1)),
            pl.BlockSpec((1, t, W_HEADS), lambda bi, j: (bi, 0, 2)),
            cache_spec, cache_spec,
            pl.BlockSpec((2 * SB_T, SB_T), lambda bi, j: (0, 0)),
        ],
        out_specs=pl.BlockSpec((1, t, W_HEADS), lambda bi, j: (bi, 0, 0)),
        out_shape=jax.ShapeDtypeStruct((b, t, W_HEADS), BF16),
        scratch_shapes=[pltpu.VMEM((t, W_HEADS), F32), pltpu.VMEM((t, W_HEADS), F32)],
        compiler_params=_params("arbitrary", "arbitrary"),
        name="sb_step",
    )(proj3, proj3, proj3, cache_k, cache_v, u)


BAND_QB = 4 * CHUNK
BAND_KB = 3


BIAS_NEAR = REL_CLIP // CHUNK + 1


def _bias_blocks_kernel(table_ref, o_ref):
    h = pl.program_id(0)
    i = lax.broadcasted_iota(jnp.int32, (CHUNK, CHUNK), 0)
    j = lax.broadcasted_iota(jnp.int32, (CHUNK, CHUNK), 1)
    for d in range(BIAS_NEAR):
        idx = jnp.clip(i - j + d * CHUNK, -REL_CLIP, REL_CLIP) + REL_CLIP

        def body(r, acc, idx=idx):
            return jnp.where(idx == r, table_ref[h, r], acc)

        lo = max(d * CHUNK - (CHUNK - 1), -REL_CLIP) + REL_CLIP
        hi = min(d * CHUNK + (CHUNK - 1), REL_CLIP) + REL_CLIP
        o_ref[0, d] = lax.fori_loop(lo, hi + 1, body, jnp.zeros((CHUNK, CHUNK), F32))


def _bias_blocks(table):
    nh = table.shape[0]
    return pl.pallas_call(
        _bias_blocks_kernel,
        grid=(nh,),
        in_specs=[pl.BlockSpec(memory_space=pltpu.SMEM)],
        out_specs=pl.BlockSpec((1, BIAS_NEAR, CHUNK, CHUNK), lambda h: (h, 0, 0, 0)),
        out_shape=jax.ShapeDtypeStruct((nh, BIAS_NEAR, CHUNK, CHUNK), F32),
        compiler_params=_params("arbitrary"),
        name="bias_blocks",
    )(table.astype(F32))


def _chunk_bias(blocks, far, dist):
    if dist < 0 or dist > BAND_LEFT_CHUNKS:
        return jnp.full(far.shape, NEG_BIG, F32)
    return blocks[:, dist] if dist < BIAS_NEAR else far


def _band_bias(blocks, table):
    far = jnp.broadcast_to(table[:, -1].astype(F32)[:, None, None], (table.shape[0], CHUNK, CHUNK))
    qc, kc = BAND_QB // CHUNK, BAND_KB * BAND_QB // CHUNK
    rows = [jnp.concatenate([_chunk_bias(blocks, far, r - c + BAND_LEFT_CHUNKS) for c in range(kc)], axis=2)
            for r in range(qc)]
    return jnp.concatenate(rows, axis=1)


def _band_step_bias(blocks, table, lb):
    far = jnp.broadcast_to(table[:, -1].astype(F32)[:, None, None], (table.shape[0], CHUNK, CHUNK))
    nc = lb // CHUNK
    return jnp.concatenate([blocks[:, nc - c] if nc - c < BIAS_NEAR else far for c in range(nc + 1)], axis=2)


def _softmax_pv_heads(scores, values):
    ps, dens = [], []
    for sc in scores:
        m = sc[0].max(axis=1, keepdims=True)
        for s in sc[1:]:
            m = jnp.maximum(m, s.max(axis=1, keepdims=True))
        p = [jnp.exp(s - m) for s in sc]
        den = p[0].sum(axis=1, keepdims=True)
        for pc in p[1:]:
            den = den + pc.sum(axis=1, keepdims=True)
        ps.append(p)
        dens.append(den)
    outs = []
    for p, den, vals in zip(ps, dens, values):
        num = _dot(p[0].astype(BF16), vals[0])
        for pc, v in zip(p[1:], vals[1:]):
            num = num + _dot(pc.astype(BF16), v)
        outs.append(num / den)
    return jnp.concatenate(outs, axis=1)


def _band_prompt_kernel(q_ref, k0_ref, k1_ref, k2_ref, v0_ref, v1_ref, v2_ref, bias_ref, o_ref):
    qi = pl.program_id(1)
    scores, values = [], []
    for h in range(N_HEADS):
        hs = _head_slice(h)
        q = q_ref[0, :, hs]
        sc = []
        for c, k_ref in enumerate((k0_ref, k1_ref, k2_ref)):
            s = _dot_nt(q, k_ref[0, :, hs]) + bias_ref[h, :, c * BAND_QB:(c + 1) * BAND_QB]
            if c < BAND_KB - 1:
                s = jnp.where(qi + c >= BAND_KB - 1, s, NEG_BIG)
            sc.append(s)
        scores.append(sc)
        values.append([v_ref[0, :, hs] for v_ref in (v0_ref, v1_ref, v2_ref)])
    o_ref[0] = _softmax_pv_heads(scores, values).astype(BF16)


def _band_prompt(proj3, bias):
    b, s, _ = proj3.shape
    qcol, kcol, vcol = 3, 4, 5

    def kv_spec(col, back):
        return pl.BlockSpec((1, BAND_QB, W_HEADS), lambda bi, qi: (bi, jnp.maximum(qi - back, 0), col))

    return pl.pallas_call(
        _band_prompt_kernel,
        grid=(b, s // BAND_QB),
        in_specs=[
            pl.BlockSpec((1, BAND_QB, W_HEADS), lambda bi, qi: (bi, qi, qcol)),
            kv_spec(kcol, 2), kv_spec(kcol, 1), kv_spec(kcol, 0),
            kv_spec(vcol, 2), kv_spec(vcol, 1), kv_spec(vcol, 0),
            pl.BlockSpec((N_HEADS, BAND_QB, BAND_KB * BAND_QB), lambda bi, qi: (0, 0, 0)),
        ],
        out_specs=pl.BlockSpec((1, BAND_QB, W_HEADS), lambda bi, qi: (bi, qi, 0)),
        out_shape=jax.ShapeDtypeStruct((b, s, W_HEADS), BF16),
        compiler_params=_params("arbitrary", "arbitrary"),
        name="band_prompt",
    )(proj3, proj3, proj3, proj3, proj3, proj3, proj3, bias)


def _band_step_kernel(q_ref, kn_ref, vn_ref, ck_ref, cv_ref, bias_ref, o_ref, *, lb):
    scores, values = [], []
    for h in range(N_HEADS):
        hs = _head_slice(h)
        q = q_ref[0, :, hs]
        rows = pl.ds(h, lb, stride=N_HEADS)
        scores.append([_dot_nt(q, ck_ref[0, rows, :].astype(BF16)) + bias_ref[h, :, :lb],
                       _dot_nt(q, kn_ref[0, :, hs]) + bias_ref[h, :, lb:]])
        values.append([cv_ref[0, rows, :].astype(BF16), vn_ref[0, :, hs]])
    o_ref[0] = _softmax_pv_heads(scores, values).astype(BF16)


def _band_step(proj3, cache_k, cache_v, bias):
    b, t, _ = proj3.shape
    lb = cache_k.shape[1] // N_HEADS
    cache_spec = pl.BlockSpec((1, lb * N_HEADS, HEAD_DIM), lambda bi: (bi, 0, 0))
    return pl.pallas_call(
        functools.partial(_band_step_kernel, lb=lb),
        grid=(b,),
        in_specs=[
            pl.BlockSpec((1, t, W_HEADS), lambda bi: (bi, 0, 3)),
            pl.BlockSpec((1, t, W_HEADS), lambda bi: (bi, 0, 4)),
            pl.BlockSpec((1, t, W_HEADS), lambda bi: (bi, 0, 5)),
            cache_spec, cache_spec,
            pl.BlockSpec((N_HEADS, t, lb + t), lambda bi: (0, 0, 0)),
        ],
        out_specs=pl.BlockSpec((1, t, W_HEADS), lambda bi: (bi, 0, 0)),
        out_shape=jax.ShapeDtypeStruct((b, t, W_HEADS), BF16),
        compiler_params=_params("arbitrary"),
        name="band_step",
    )(proj3, proj3, proj3, cache_k, cache_v, bias)


def _merge_kernel(osb_ref, obd_ref, wsb_ref, wbd_ref, ga_ref, gb_ref, o_ref):
    a = _dot(osb_ref[...], wsb_ref[...])
    b = _dot(obd_ref[...], wbd_ref[...])
    merged = _sigmoid(ga_ref[...].astype(F32)) * a + _sigmoid(gb_ref[...].astype(F32)) * b
    o_ref[...] = merged.astype(BF16)


def _merge(o_sb, o_band, w_sb_bf, w_band_bf, proj, tm):
    m = o_sb.shape[0]
    d = w_sb_bf.shape[1]
    tn = min(MATMUL_TN, d)
    ga_col = 6 * W_HEADS // tn
    gb_col = ga_col + d // tn
    return pl.pallas_call(
        _merge_kernel,
        grid=(m // tm, d // tn),
        in_specs=[
            pl.BlockSpec((tm, W_HEADS), lambda i, j: (i, 0)),
            pl.BlockSpec((tm, W_HEADS), lambda i, j: (i, 0)),
            pl.BlockSpec((W_HEADS, tn), lambda i, j: (0, j)),
            pl.BlockSpec((W_HEADS, tn), lambda i, j: (0, j)),
            pl.BlockSpec((tm, tn), lambda i, j: (i, ga_col + j)),
            pl.BlockSpec((tm, tn), lambda i, j: (i, gb_col + j)),
        ],
        out_specs=pl.BlockSpec((tm, tn), lambda i, j: (i, j)),
        out_shape=jax.ShapeDtypeStruct((m, d), BF16),
        compiler_params=_params("arbitrary", "arbitrary"),
        name="merge",
    )(o_sb, o_band, w_sb_bf, w_band_bf, proj, proj)


def _outproj_kernel(m_ref, w_ref, x_ref, gate_ref, o_ref, *, gb):
    acc = _dot(m_ref[...], w_ref[...])
    acc = acc.reshape(gb, ROW_GROUP, acc.shape[1])
    o_ref[...] = x_ref[...] + gate_ref[0] * acc


def _outproj(merged, w_out_bf, x3, mods, gb):
    ng, _, d = x3.shape
    tm = gb * ROW_GROUP
    tn = min(MATMUL_TN, d)
    return pl.pallas_call(
        functools.partial(_outproj_kernel, gb=gb),
        grid=(ng // gb, d // tn),
        in_specs=[
            pl.BlockSpec((tm, d), lambda i, j: (i, 0)),
            pl.BlockSpec((d, tn), lambda i, j: (0, j)),
            pl.BlockSpec((gb, ROW_GROUP, tn), lambda i, j: (i, 0, j)),
            pl.BlockSpec((1, gb, 1, tn), lambda i, j: (MOD_GATE_M, i, 0, j)),
        ],
        out_specs=pl.BlockSpec((gb, ROW_GROUP, tn), lambda i, j: (i, 0, j)),
        out_shape=jax.ShapeDtypeStruct(x3.shape, F32),
        compiler_params=_params("arbitrary", "arbitrary"),
        name="outproj",
    )(merged, w_out_bf, x3, mods)


GATHER_HALF = 128


def _tile_rows(d):
    assert d % V7X_LANES == 0
    return d // V7X_LANES


def _store_token_tiles(ref, first_token, x):
    n, d = x.shape
    s_per = _tile_rows(d)
    for c in range(s_per):
        ref[pl.ds(first_token * s_per + c, n, stride=s_per), :] = x[:, c * V7X_LANES:(c + 1) * V7X_LANES]


def _load_token_tiles(ref, n, d, first_token=0):
    s_per = _tile_rows(d)
    return jnp.concatenate([ref[pl.ds(first_token * s_per + c, n, stride=s_per), :] for c in range(s_per)], axis=1)


def _start_token_gather(table_hbm, idx_ref, first, buf, sem_of, n, s_per, unrolled=False):
    for half in range(n // GATHER_HALF):
        def issue(i, carry, half=half):
            row = half * GATHER_HALF + i
            pltpu.make_async_copy(table_hbm.at[pl.ds(idx_ref[first + row] * s_per, s_per)],
                                  buf.at[pl.ds(row * s_per, s_per)], sem_of(half)).start()
            return carry
        if unrolled:
            for i in range(GATHER_HALF):
                issue(i, 0)
        else:
            lax.fori_loop(0, GATHER_HALF, issue, 0, unroll=8)


def _wait_token_gather(buf, sem_of, n, s_per):
    for half in range(n // GATHER_HALF):
        part = buf.at[pl.ds(half * GATHER_HALF * s_per, GATHER_HALF * s_per)]
        pltpu.make_async_copy(part, part, sem_of(half)).wait()


ROUTE_E1, ROUTE_E2, ROUTE_W1, ROUTE_W2 = 0, 1, 2, 3


def _router_kernel(xa_ref, sca_ref, sha_ref, xb_ref, scb_ref, shb_ref, g_ref, whi_ref, wlo_ref, b_ref,
                   h_ref, route_ref, hi_scr, lo_scr, *, gb, na):
    def fill(x_ref, sc_ref, sh_ref):
        def body(s, carry):
            h = _modulated_norm(x_ref[s], g_ref[...], sc_ref[0, s], sh_ref[0, s])
            hi = h.astype(BF16)
            first = pl.multiple_of(s * ROW_GROUP, ROW_GROUP)
            _store_token_tiles(h_ref, first, h)
            hi_scr[pl.ds(first, ROW_GROUP), :] = hi
            lo_scr[pl.ds(first, ROW_GROUP), :] = (h - hi.astype(F32)).astype(BF16)
            return carry
        lax.fori_loop(0, gb, body, 0)

    @pl.when(pl.program_id(0) < na)
    def _():
        fill(xa_ref, sca_ref, sha_ref)

    @pl.when(pl.program_id(0) >= na)
    def _():
        fill(xb_ref, scb_ref, shb_ref)

    hi = hi_scr[...]
    logits = _dot(hi, whi_ref[...]) + _dot(hi, wlo_ref[...]) + _dot(lo_scr[...], whi_ref[...]) + b_ref[...]
    lane = lax.broadcasted_iota(jnp.int32, logits.shape, 1)
    ninf = -jnp.inf

    lane_f = lane.astype(F32)

    def first_max(vals):
        mx = vals.max(axis=1, keepdims=True)
        idx = jnp.where(vals == mx, lane_f, float(V7X_LANES)).min(axis=1, keepdims=True)
        return mx, idx

    gl = jnp.where(lane < N_GROUPS, logits, ninf)
    gmax, gidx = first_max(gl)
    g_weight = 1.0 / jnp.exp(gl - gmax).sum(axis=1, keepdims=True)
    lo_lane = N_GROUPS + EXPERTS_PER_GROUP * gidx
    el = jnp.where(lane_f >= lo_lane, jnp.where(lane_f < lo_lane + EXPERTS_PER_GROUP, logits, ninf), ninf)
    m1, i1 = first_max(el)
    m2, i2 = first_max(jnp.where(lane_f == i1, ninf, el))
    e21 = jnp.exp(m2 - m1)
    p1 = 1.0 / (1.0 + e21)
    p2 = e21 / (1.0 + e21)
    route = jnp.where(lane == ROUTE_E1, i1 - N_GROUPS,
                      jnp.where(lane == ROUTE_E2, i2 - N_GROUPS,
                                jnp.where(lane == ROUTE_W1, g_weight * p1,
                                          jnp.where(lane == ROUTE_W2, g_weight * p2, 0.0))))
    route_ref[...] = route


def _router(xa3, mods_a, xb3, mods_b, g, w_hi, w_lo, bias, gb):
    na, nb = xa3.shape[0] // gb, xb3.shape[0] // gb
    d = xa3.shape[2]
    n_tokens = (na + nb) * gb * ROW_GROUP
    tm = gb * ROW_GROUP
    s_per = _tile_rows(d)

    def first(i):
        return jnp.minimum(i, na - 1)

    def second(i):
        return jnp.maximum(i - na, 0)

    def mod_spec(which, blk):
        return pl.BlockSpec((1, gb, 1, d), lambda i: (which, blk(i), 0, 0))

    w_spec = pl.BlockSpec((d, V7X_LANES), lambda i: (0, 0))
    return pl.pallas_call(
        functools.partial(_router_kernel, gb=gb, na=na),
        grid=(na + nb,),
        in_specs=[
            pl.BlockSpec((gb, ROW_GROUP, d), lambda i: (first(i), 0, 0)),
            mod_spec(MOD_SCALE_F, first), mod_spec(MOD_SHIFT_F, first),
            pl.BlockSpec((gb, ROW_GROUP, d), lambda i: (second(i), 0, 0)),
            mod_spec(MOD_SCALE_F, second), mod_spec(MOD_SHIFT_F, second),
            pl.BlockSpec((1, d), lambda i: (0, 0)),
            w_spec, w_spec,
            pl.BlockSpec((1, V7X_LANES), lambda i: (0, 0)),
        ],
        out_specs=[
            pl.BlockSpec((tm * s_per, V7X_LANES), lambda i: (i, 0)),
            pl.BlockSpec((tm, V7X_LANES), lambda i: (i, 0)),
        ],
        out_shape=[jax.ShapeDtypeStruct((n_tokens * s_per, V7X_LANES), F32),
                   jax.ShapeDtypeStruct((n_tokens, V7X_LANES), F32)],
        scratch_shapes=[pltpu.VMEM((tm, d), BF16), pltpu.VMEM((tm, d), BF16)],
        compiler_params=_params("arbitrary"),
        name="router",
    )(xa3, mods_a, mods_a, xb3, mods_b, mods_b, g, w_hi, w_lo, bias)


def _moe_kernel(te_ref, nv_ref, src_ref, h_hbm, wg_ref, wu_ref, wd_ref, o_ref, xbuf, sems, wg_scr, wu_scr, wd_scr):
    t = pl.program_id(0)
    n_valid = nv_ref[0]
    d = wg_scr.shape[0]
    s_per = _tile_rows(d)

    def start(tile, slot):
        _start_token_gather(h_hbm, src_ref, tile * MOE_TILE, xbuf.at[slot], lambda half: sems.at[slot, half],
                            MOE_TILE, s_per)

    @pl.when(t == 0)
    def _():
        start(0, 0)

    changed = te_ref[t] != te_ref[jnp.maximum(t - 1, 0)]

    @pl.when((t == 0) | changed)
    def _():
        wg_scr[...] = wg_ref[0].astype(BF16)
        wu_scr[...] = wu_ref[0].astype(BF16)
        wd_scr[...] = wd_ref[0].astype(BF16)

    @pl.when(t < n_valid)
    def _():
        slot = t % 2
        _wait_token_gather(xbuf.at[slot], lambda half: sems.at[slot, half], MOE_TILE, s_per)

        @pl.when(t + 1 < n_valid)
        def _():
            _start_token_gather(h_hbm, src_ref, (t + 1) * MOE_TILE, xbuf.at[1 - slot],
                                lambda half: sems.at[1 - slot, half], MOE_TILE, s_per, unrolled=True)

        firsts = range(0, MOE_TILE, MOE_SUB)
        xs = [_load_token_tiles(xbuf.at[slot], MOE_SUB, d, first).astype(BF16) for first in firsts]
        gu = [(_dot(x, wg_scr[...]), _dot(x, wu_scr[...])) for x in xs]
        hidden = [((g * _sigmoid(g)) * u).astype(BF16) for g, u in gu]
        for first, hid in zip(firsts, hidden):
            _store_token_tiles(o_ref, first, _dot(hid, wd_scr[...]))

    @pl.when(t >= n_valid)
    def _():
        o_ref[...] = jnp.zeros(o_ref.shape, F32)


def _moe(tile_expert, n_valid, src_tok, h_tiles, w_gate, w_up, w_down):
    _, d, f = w_gate.shape
    s_per = _tile_rows(d)
    n_tiles = src_tok.shape[0] // MOE_TILE
    halves = MOE_TILE // GATHER_HALF
    grid_spec = pltpu.PrefetchScalarGridSpec(
        num_scalar_prefetch=3,
        grid=(n_tiles,),
        in_specs=[
            pl.BlockSpec(memory_space=pl.ANY),
            pl.BlockSpec((1, d, f), lambda t, te, nv, src: (te[t], 0, 0)),
            pl.BlockSpec((1, d, f), lambda t, te, nv, src: (te[t], 0, 0)),
            pl.BlockSpec((1, f, d), lambda t, te, nv, src: (te[t], 0, 0)),
        ],
        out_specs=pl.BlockSpec((MOE_TILE * s_per, V7X_LANES), lambda t, te, nv, src: (t, 0)),
        scratch_shapes=[pltpu.VMEM((2, MOE_TILE * s_per, V7X_LANES), F32), pltpu.SemaphoreType.DMA((2, halves)),
                        pltpu.VMEM((d, f), BF16), pltpu.VMEM((d, f), BF16), pltpu.VMEM((f, d), BF16)],
    )
    return pl.pallas_call(
        _moe_kernel,
        grid_spec=grid_spec,
        out_shape=jax.ShapeDtypeStruct((n_tiles * MOE_TILE * s_per, V7X_LANES), F32),
        compiler_params=_params("arbitrary"),
        name="moe",
    )(tile_expert, n_valid, src_tok, h_tiles, w_gate, w_up, w_down)


def _dispatch(route):
    n = route.shape[0]
    e = jnp.concatenate([route[:, ROUTE_E1], route[:, ROUTE_E2]]).astype(jnp.int32)
    tok = jnp.concatenate([jnp.arange(n, dtype=jnp.int32)] * 2)
    onehot = (e[:, None] == jnp.arange(N_EXPERTS, dtype=jnp.int32)[None, :]).astype(jnp.int32)
    before = jnp.cumsum(onehot, axis=0) - onehot
    rank = jnp.sum(before * onehot, axis=1)
    counts = jnp.sum(onehot, axis=0)
    padded = ((counts + MOE_TILE - 1) // MOE_TILE) * MOE_TILE
    ends = jnp.cumsum(padded)
    pos = (ends - padded)[e] + rank
    n_tiles = -(-(2 * n + N_EXPERTS * (MOE_TILE - 1)) // (4 * MOE_TILE)) * 4
    tile_start = jnp.arange(n_tiles, dtype=jnp.int32) * MOE_TILE
    n_valid = (ends[-1] // MOE_TILE).astype(jnp.int32)
    tile_expert = jnp.sum((ends[None, :] <= tile_start[:, None]).astype(jnp.int32), axis=1)
    last_expert = tile_expert[jnp.maximum(n_valid - 1, 0)]
    tile_expert = jnp.where(tile_start < ends[-1], tile_expert, last_expert)
    src_tok = jnp.zeros((n_tiles * MOE_TILE,), jnp.int32).at[pos].set(tok)
    return pos[:n], pos[n:], src_tok, tile_expert, n_valid.reshape(1)


def _final_kernel(p1_ref, p2_ref, x_ref, y_hbm, route_ref, gate_ref, o_ref, ybuf, sems, *, first_token):
    i = pl.program_id(0)
    gb, rg, d = x_ref.shape
    tm = gb * rg
    s_per = _tile_rows(d)

    def start(step, slot):
        for which, pos_ref in enumerate((p1_ref, p2_ref)):
            _start_token_gather(y_hbm, pos_ref, first_token + step * tm, ybuf.at[slot, which],
                                lambda half, which=which: sems.at[slot, which, half], tm, s_per)

    @pl.when(i == 0)
    def _():
        start(0, 0)

    @pl.when(i + 1 < pl.num_programs(0))
    def _():
        start(i + 1, (i + 1) % 2)

    slot = i % 2
    for which in range(2):
        _wait_token_gather(ybuf.at[slot, which], lambda half, which=which: sems.at[slot, which, half], tm, s_per)
    r = route_ref[...]
    y = (r[:, ROUTE_W1:ROUTE_W1 + 1] * _load_token_tiles(ybuf.at[slot, 0], tm, d)
         + r[:, ROUTE_W2:ROUTE_W2 + 1] * _load_token_tiles(ybuf.at[slot, 1], tm, d))
    o_ref[...] = x_ref[...] + gate_ref[0] * y.reshape(gb, rg, d)


def _final(x3, y_tiles, pos1, pos2, route, first_token, mods, gb):
    ng, _, d = x3.shape
    tm = gb * ROW_GROUP
    s_per = _tile_rows(d)
    assert first_token % tm == 0
    b0 = first_token // tm
    blk = pl.BlockSpec((gb, ROW_GROUP, d), lambda i, p1, p2: (i, 0, 0))
    grid_spec = pltpu.PrefetchScalarGridSpec(
        num_scalar_prefetch=2,
        grid=(ng // gb,),
        in_specs=[
            blk,
            pl.BlockSpec(memory_space=pl.ANY),
            pl.BlockSpec((tm, V7X_LANES), lambda i, p1, p2: (b0 + i, 0)),
            pl.BlockSpec((1, gb, 1, d), lambda i, p1, p2: (MOD_GATE_F, i, 0, 0)),
        ],
        out_specs=blk,
        scratch_shapes=[pltpu.VMEM((2, 2, tm * s_per, V7X_LANES), F32),
                        pltpu.SemaphoreType.DMA((2, 2, tm // GATHER_HALF))],
    )
    return pl.pallas_call(
        functools.partial(_final_kernel, first_token=first_token),
        grid_spec=grid_spec,
        out_shape=jax.ShapeDtypeStruct(x3.shape, F32),
        compiler_params=_params("arbitrary"),
        name="final",
    )(pos1, pos2, x3, y_tiles, route, mods)


def _layer(xp, xs, csk, csv, cbk, cbv, c_prompt, c_sample, norm_mix, norm_ffn, w_ada, b_ada, w_in, q_norm,
           k_norm, rel_table, w_proj_sb, w_proj_band, w_out, w_rg, b_rg, w_re, b_re, w_gate, w_up, w_down):
    bp, sp_len, d = xp.shape
    bs, ts, _ = xs.shape
    d_in = w_in.shape[1]
    assert sp_len % BAND_QB == 0 and ts == ROW_GROUP and sp_len % ROW_GROUP == 0

    n_c = bp + bs
    c_pad = jnp.concatenate([c_prompt, c_sample, jnp.zeros((-n_c % 8, d), F32)], axis=0)
    mod = _ada(c_pad, w_ada, b_ada)
    gp = sp_len // ROW_GROUP
    mod_p = jnp.repeat(mod[:, :bp], gp, axis=1)[:, :, None, :]
    mod_s = mod[:, bp:n_c][:, :, None, :]

    w_in_bf = w_in.astype(BF16)
    w_sb_bf = w_proj_sb.astype(BF16)
    w_band_bf = w_proj_band.astype(BF16)
    w_out_bf = w_out.astype(BF16)
    qn = q_norm.reshape(1, W_HEADS)
    kn = k_norm.reshape(1, W_HEADS)
    g_mix = norm_mix.reshape(1, d)
    g_ffn = norm_ffn.reshape(1, d)
    lb = cbk.shape[1]
    assert lb % CHUNK == 0 and ts == CHUNK
    bias_blocks = _bias_blocks(rel_table)
    bias = _band_bias(bias_blocks, rel_table)
    bias_step = _band_step_bias(bias_blocks, rel_table, lb)
    u = _strict_lower_twice(SB_T)

    w_router = jnp.zeros((d, V7X_LANES), F32).at[:, :N_GROUPS].set(w_rg)
    w_router = w_router.at[:, N_GROUPS:N_GROUPS + N_EXPERTS].set(w_re)
    w_router_hi = w_router.astype(BF16)
    w_router_lo = (w_router - w_router_hi.astype(F32)).astype(BF16)
    b_router = jnp.zeros((1, V7X_LANES), F32).at[0, :N_GROUPS].set(b_rg)
    b_router = b_router.at[0, N_GROUPS:N_GROUPS + N_EXPERTS].set(b_re.reshape(-1))

    xp3 = xp.reshape(bp * gp, ROW_GROUP, d)
    xs3 = xs.reshape(bs, ROW_GROUP, d)
    gb_p = min(ROWWISE_GROUPS, bp * gp)
    gb_s = min(ROWWISE_GROUPS, bs)

    def mixer(x3, mods, gb, attend):
        proj, ka, va, kb, vb = _proj(x3, g_mix, mods, w_in_bf, qn, kn, gb)
        o_sb, o_band = attend(proj)
        merged = _merge(o_sb, o_band, w_sb_bf, w_band_bf, proj, gb * ROW_GROUP)
        x1 = _outproj(merged, w_out_bf, x3, mods, gb)
        return x1, ka, va, kb, vb

    def attend_prompt(proj):
        p3 = proj.reshape(bp, sp_len, d_in)
        return (_sb_prompt(p3, u).reshape(bp * sp_len, W_HEADS),
                _band_prompt(p3, bias).reshape(bp * sp_len, W_HEADS))

    def attend_sample(proj):
        p3 = proj.reshape(bs, ts, d_in)
        def rows(cache):
            return cache.reshape(bs, -1, HEAD_DIM)

        o_sb = _sb_step(p3, rows(csk), rows(csv), u)
        o_band = _band_step(p3, rows(cbk), rows(cbv), bias_step)
        return o_sb.reshape(bs * ts, W_HEADS), o_band.reshape(bs * ts, W_HEADS)

    x1p, kap, vap, kbp, vbp = mixer(xp3, mod_p, min(MATMUL_GROUPS, bp * gp), attend_prompt)
    x1s, kas, vas, kbs, vbs = mixer(xs3, mod_s, min(MATMUL_GROUPS, bs), attend_sample)

    gb_r = min(gb_p, gb_s)
    h2, route = _router(x1p, mod_p, x1s, mod_s, g_ffn, w_router_hi, w_router_lo, b_router, gb_r)
    pos1, pos2, src_tok, tile_expert, n_valid = _dispatch(route)
    y_sorted = _moe(tile_expert, n_valid, src_tok, h2, w_gate, w_up, w_down)
    gb_f = min(FINAL_GROUPS, bs)
    yp = _final(x1p, y_sorted, pos1, pos2, route, 0, mod_p, gb_f).reshape(bp, sp_len, d)
    ys = _final(x1s, y_sorted, pos1, pos2, route, bp * sp_len, mod_s, gb_f).reshape(bs, ts, d)

    def heads(a, b):
        return a.reshape(b, -1, N_HEADS, HEAD_DIM)

    n_band = min(BAND_PAST, sp_len)
    return (yp, ys, heads(kap, bp), heads(vap, bp), heads(kbp, bp)[:, -n_band:], heads(vbp, bp)[:, -n_band:],
            heads(kas, bs), heads(vas, bs), heads(kbs, bs), heads(vbs, bs))


def kernel(x_prompt, x_sample, cache_sb_k, cache_sb_v, cache_band_k, cache_band_v, c_prompt, c_sample, norm_mix, norm_ffn, w_ada, b_ada, w_in, q_norm_band, k_norm_band, rel_bias_band, w_proj_sb, w_proj_band, w_out, w_router_group, b_router_group, w_router_expert, b_router_expert, w_gate, w_up, w_down):
    depth = w_in.shape[0]
    xp, xs = x_prompt, x_sample
    outs = [[] for _ in range(8)]
    for l in range(depth):
        res = _layer(xp, xs, cache_sb_k[l], cache_sb_v[l], cache_band_k[l], cache_band_v[l], c_prompt, c_sample,
                     norm_mix[l], norm_ffn[l], w_ada[l], b_ada[l], w_in[l], q_norm_band[l], k_norm_band[l],
                     rel_bias_band[l], w_proj_sb[l], w_proj_band[l], w_out[l], w_router_group[l],
                     b_router_group[l], w_router_expert[l], b_router_expert[l], w_gate[l], w_up[l], w_down[l])
        xp, xs = res[0], res[1]
        for acc, r in zip(outs, res[2:]):
            acc.append(r)
    return (xp, xs) + tuple(jnp.stack(o, axis=0) for o in outs)
```

```python
import functools

import jax
import jax.numpy as jnp
from jax import lax
from jax.experimental import pallas as pl
from jax.experimental.pallas import tpu as pltpu

F32 = jnp.float32
BF16 = jnp.bfloat16

EPS = 1e-6
HEAD_DIM = 128
N_HEADS = 8
W_HEADS = N_HEADS * HEAD_DIM
CHUNK = 64
BAND_LEFT_CHUNKS = 8
BAND_PAST = BAND_LEFT_CHUNKS * CHUNK
REL_CLIP = 128
N_GROUPS = 4
EXPERTS_PER_GROUP = 8
N_EXPERTS = N_GROUPS * EXPERTS_PER_GROUP
QK_SCALE = HEAD_DIM ** -0.5
NEG_BIG = -1e30

V7X_LANES = 128
V7X_VMEM_LIMIT = 56 * 1024 * 1024
ROW_GROUP = 64
MOE_TILE = 256
MATMUL_GROUPS = 16
MATMUL_TN = 1024
PROJ_TN = 1024
ROWWISE_GROUPS = 8
FINAL_GROUPS = 4


def _params(*sem):
    return pltpu.CompilerParams(dimension_semantics=sem, vmem_limit_bytes=V7X_VMEM_LIMIT)


def _sigmoid(x):
    return 1.0 / (1.0 + jnp.exp(-x))


def _dot(a, b):
    return jnp.dot(a, b, preferred_element_type=F32)


def _dot_nt(a, b):
    return lax.dot_general(a, b, (((1,), (1,)), ((), ())), preferred_element_type=F32)


def _ada_kernel(c_ref, w_ref, b_ref, o_ref):
    c = c_ref[...]
    a = (c * _sigmoid(c)).astype(BF16)
    o_ref[0] = _dot(a, w_ref[...].astype(BF16)) + b_ref[...]


def _ada(c_pad, w_ada, b_ada):
    r, d = c_pad.shape
    tn = min(1024, d)
    per = d // tn
    return pl.pallas_call(
        _ada_kernel,
        grid=(6 * per,),
        in_specs=[
            pl.BlockSpec((r, d), lambda j: (0, 0)),
            pl.BlockSpec((d, tn), lambda j: (0, j)),
            pl.BlockSpec((1, tn), lambda j: (0, j)),
        ],
        out_specs=pl.BlockSpec((1, r, tn), lambda j: (j // per, 0, j % per)),
        out_shape=jax.ShapeDtypeStruct((6, r, d), F32),
        compiler_params=_params("arbitrary"),
        name="ada",
    )(c_pad, w_ada, b_ada.reshape(1, 6 * d))


def _modulated_norm(x, g, scale, shift):
    ms = jnp.mean(x * x, axis=-1, keepdims=True)
    return (x * lax.rsqrt(ms + EPS) * g) * (1.0 + scale) + shift


def _head_norm(a, gain_ref):
    outs = []
    for hh in range(a.shape[1] // HEAD_DIM):
        blk = a[:, hh * HEAD_DIM:(hh + 1) * HEAD_DIM]
        ms = jnp.mean(blk * blk, axis=-1, keepdims=True)
        outs.append(blk * lax.rsqrt(ms + EPS) * gain_ref[:, hh * HEAD_DIM:(hh + 1) * HEAD_DIM])
    return jnp.concatenate(outs, axis=1)


def _proj_kernel(x_ref, g_ref, sc_ref, sh_ref, w_ref, qn_ref, kn_ref,
                 proj_ref, ka_ref, va_ref, kb_ref, vb_ref, h_scr, *, gb, per):
    j = pl.program_id(1)

    @pl.when(j == 0)
    def _():
        def body(s, carry):
            h = _modulated_norm(x_ref[s], g_ref[...], sc_ref[0, s], sh_ref[0, s])
            h_scr[pl.ds(pl.multiple_of(s * ROW_GROUP, ROW_GROUP), ROW_GROUP), :] = h.astype(BF16)
            return carry
        lax.fori_loop(0, gb, body, 0)

    acc = _dot(h_scr[...], w_ref[...])
    sec = j // per

    @pl.when(sec == 0)
    def _():
        proj_ref[...] = (acc * QK_SCALE).astype(BF16)

    @pl.when(sec == 1)
    def _():
        proj_ref[...] = acc.astype(BF16)
        ka_ref[...] = acc

    @pl.when(sec == 2)
    def _():
        proj_ref[...] = acc.astype(BF16)
        va_ref[...] = acc

    @pl.when(sec == 3)
    def _():
        proj_ref[...] = (_head_norm(acc, qn_ref) * QK_SCALE).astype(BF16)

    @pl.when(sec == 4)
    def _():
        n = _head_norm(acc, kn_ref)
        proj_ref[...] = n.astype(BF16)
        kb_ref[...] = n

    @pl.when(sec == 5)
    def _():
        proj_ref[...] = acc.astype(BF16)
        vb_ref[...] = acc

    @pl.when(sec >= 6)
    def _():
        proj_ref[...] = acc.astype(BF16)


MOD_SHIFT_M, MOD_SCALE_M, MOD_GATE_M, MOD_SHIFT_F, MOD_SCALE_F, MOD_GATE_F = range(6)


def _proj(x3, g, mods, w_in_bf, qn, kn, gb):
    ng, _, d = x3.shape
    d_in = w_in_bf.shape[1]
    m = ng * ROW_GROUP
    tm = gb * ROW_GROUP
    tn = PROJ_TN
    per = W_HEADS // tn

    def sect(s):
        return pl.BlockSpec((tm, tn), lambda i, j: (i, jnp.clip(j - s * per, 0, per - 1)),
                            pipeline_mode=pl.Buffered(1))

    def mod_spec(which):
        return pl.BlockSpec((1, gb, 1, d), lambda i, j: (which, i, 0, 0))

    kv_shape = jax.ShapeDtypeStruct((m, W_HEADS), F32)
    return pl.pallas_call(
        functools.partial(_proj_kernel, gb=gb, per=per),
        grid=(ng // gb, d_in // tn),
        in_specs=[
            pl.BlockSpec((gb, ROW_GROUP, d), lambda i, j: (i, 0, 0)),
            pl.BlockSpec((1, d), lambda i, j: (0, 0)),
            mod_spec(MOD_SCALE_M), mod_spec(MOD_SHIFT_M),
            pl.BlockSpec((d, tn), lambda i, j: (0, j)),
            pl.BlockSpec((1, tn), lambda i, j: (0, jnp.clip(j - 3 * per, 0, per - 1))),
            pl.BlockSpec((1, tn), lambda i, j: (0, jnp.clip(j - 4 * per, 0, per - 1))),
        ],
        out_specs=[
            pl.BlockSpec((tm, tn), lambda i, j: (i, j)),
            sect(1), sect(2), sect(4), sect(5),
        ],
        out_shape=[jax.ShapeDtypeStruct((m, d_in), BF16), kv_shape, kv_shape, kv_shape, kv_shape],
        scratch_shapes=[pltpu.VMEM((tm, d), BF16)],
        compiler_params=_params("arbitrary", "arbitrary"),
        name="proj",
    )(x3, g, mods, mods, w_in_bf, qn, kn)


def _strict_lower_twice(n):
    row = lax.broadcasted_iota(jnp.int32, (2 * n, n), 0)
    col = lax.broadcasted_iota(jnp.int32, (2 * n, n), 1)
    row = jnp.where(row >= n, row - n, row)
    return jnp.where(row > col, 1.0, 0.0).astype(BF16)


def _sb_scores(load_qk, n_heads):
    return jnp.concatenate([_dot_nt(*load_qk(h)) for h in range(n_heads)], axis=1)


def _sb_apply(z_all, load_v, n_heads, u, carries, diagonal):
    tq = z_all.shape[0]
    tk = z_all.shape[1] // n_heads
    zs = [z_all[:, h * tk:(h + 1) * tk] for h in range(n_heads)]
    if diagonal:
        row = lax.broadcasted_iota(jnp.int32, (tq, tk), 0)
        col = lax.broadcasted_iota(jnp.int32, (tq, tk), 1)
        valid = col < row
    sps, tails = [], []
    for z in zs:
        sp = jnp.maximum(z, 0.0) + jnp.log(1.0 + jnp.exp(-jnp.abs(z)))
        if diagonal:
            sp = jnp.where(valid, sp, 0.0)
        hi = sp.astype(BF16)
        lo = (sp - hi.astype(F32)).astype(BF16)
        sps.append(sp)
        tails.append(_dot(jnp.concatenate([hi, lo], axis=1), u))
    pvs, new = [], []
    for h, (z, sp, tail) in enumerate(zip(zs, sps, tails)):
        carry = jnp.zeros((tq, 1), F32) if carries is None else carries[:, h * HEAD_DIM:h * HEAD_DIM + 1]
        loga = z - sp - tail - carry
        if diagonal:
            loga = jnp.where(valid, loga, NEG_BIG)
        pvs.append(_dot(jnp.exp(loga).astype(BF16), load_v(h)))
        new.append(jnp.broadcast_to(carry + tail[:, 0:1] + sp[:, 0:1], (tq, HEAD_DIM)))
    return jnp.concatenate(pvs, axis=1), jnp.concatenate(new, axis=1)


SB_T = 256
SB_HEADS_PER_STEP = 8
SB_CACHE_ROWS = 2048


def _head_slice(h):
    return slice(h * HEAD_DIM, (h + 1) * HEAD_DIM)


def _sb_prompt_kernel(q_ref, k_ref, v_ref, u_ref, o_ref, acc_scr, carry_scr):
    qi = pl.program_id(2)
    u = u_ref[...]
    nh = SB_HEADS_PER_STEP

    def block(blk, carries, diagonal):
        r = pl.multiple_of(blk * SB_T, SB_T)
        z_all = _sb_scores(lambda h: (q_ref[0, :, _head_slice(h)], k_ref[0, pl.ds(r, SB_T), _head_slice(h)]), nh)
        return _sb_apply(z_all, lambda h: v_ref[0, pl.ds(r, SB_T), _head_slice(h)], nh, u, carries, diagonal)

    acc_scr[...], carry_scr[...] = block(qi, None, True)

    def body(step, c):
        pv, carries = block(qi - 1 - step, carry_scr[...], False)
        acc_scr[...] += pv
        carry_scr[...] = carries
        return c

    lax.fori_loop(0, qi, body, 0)
    o_ref[0] = acc_scr[...].astype(BF16)


def _sb_prompt(proj3, u):
    b, s, _ = proj3.shape
    hp = SB_HEADS_PER_STEP
    w = hp * HEAD_DIM
    kcol, vcol = W_HEADS // w, 2 * W_HEADS // w
    return pl.pallas_call(
        _sb_prompt_kernel,
        grid=(b, N_HEADS // hp, s // SB_T),
        in_specs=[
            pl.BlockSpec((1, SB_T, w), lambda bi, h, qi: (bi, qi, h)),
            pl.BlockSpec((1, s, w), lambda bi, h, qi: (bi, 0, kcol + h)),
            pl.BlockSpec((1, s, w), lambda bi, h, qi: (bi, 0, vcol + h)),
            pl.BlockSpec((2 * SB_T, SB_T), lambda bi, h, qi: (0, 0)),
        ],
        out_specs=pl.BlockSpec((1, SB_T, w), lambda bi, h, qi: (bi, qi, h)),
        out_shape=jax.ShapeDtypeStruct((b, s, W_HEADS), BF16),
        scratch_shapes=[pltpu.VMEM((SB_T, w), F32), pltpu.VMEM((SB_T, w), F32)],
        compiler_params=_params("arbitrary", "arbitrary", "arbitrary"),
        name="sb_prompt",
    )(proj3, proj3, proj3, u)


def _sb_step_kernel(q_ref, kn_ref, vn_ref, ck_ref, cv_ref, u_ref, o_ref, acc_scr, carry_scr, *, tkb, t):
    j = pl.program_id(1)
    u = u_ref[...]

    @pl.when(j == 0)
    def _():
        z_new = _sb_scores(lambda h: (q_ref[0, :, _head_slice(h)], kn_ref[0, :, _head_slice(h)]), N_HEADS)
        acc_scr[...], carry_scr[...] = _sb_apply(z_new, lambda h: vn_ref[0, :, _head_slice(h)], N_HEADS,
                                                 _strict_lower_twice(t), None, True)

    def rows(sub, h):
        first = pl.multiple_of((tkb - (sub + 1) * SB_T) * N_HEADS, SB_T * N_HEADS)
        return pl.ds(first + h, SB_T, stride=N_HEADS)

    def body(sub, c):
        z_all = _sb_scores(lambda h: (q_ref[0, :, _head_slice(h)], ck_ref[0, rows(sub, h), :].astype(BF16)), N_HEADS)
        pv, carries = _sb_apply(z_all, lambda h: cv_ref[0, rows(sub, h), :].astype(BF16), N_HEADS, u,
                                carry_scr[...], False)
        acc_scr[...] += pv
        carry_scr[...] = carries
        return c

    lax.fori_loop(0, tkb // SB_T, body, 0)

    @pl.when(j == pl.num_programs(1) - 1)
    def _():
        o_ref[0] = acc_scr[...].astype(BF16)


def _sb_step(proj3, cache_k, cache_v, u):
    b, t, _ = proj3.shape
    p = cache_k.shape[1] // N_HEADS
    tkb = min(SB_CACHE_ROWS, p)
    nj = p // tkb
    cache_spec = pl.BlockSpec((1, tkb * N_HEADS, HEAD_DIM), lambda bi, j: (bi, nj - 1 - j, 0))
    return pl.pallas_call(
        functools.partial(_sb_step_kernel, tkb=tkb, t=t),
        grid=(b, nj),
        in_specs=[
            pl.BlockSpec((1, t, W_HEADS), lambda bi, j: (bi, 0, 0)),
            pl.BlockSpec((1, t, W_HEADS), lambda bi, j: (bi, 0, 1)),
            pl.BlockSpec((1, t, W_HEADS), lambda bi, j: (bi, 0, 2)),
            cache_spec, cache_spec,
            pl.BlockSpec((2 * SB_T, SB_T), lambda bi, j: (0, 0)),
        ],
        out_specs=pl.BlockSpec((1, t, W_HEADS), lambda bi, j: (bi, 0, 0)),
        out_shape=jax.ShapeDtypeStruct((b, t, W_HEADS), BF16),
        scratch_shapes=[pltpu.VMEM((t, W_HEADS), F32), pltpu.VMEM((t, W_HEADS), F32)],
        compiler_params=_params("arbitrary", "arbitrary"),
        name="sb_step",
    )(proj3, proj3, proj3, cache_k, cache_v, u)


BAND_QB = 4 * CHUNK
BAND_KB = 3


BIAS_NEAR = REL_CLIP // CHUNK + 1


def _bias_blocks_kernel(table_ref, o_ref):
    h = pl.program_id(0)
    i = lax.broadcasted_iota(jnp.int32, (CHUNK, CHUNK), 0)
    j = lax.broadcasted_iota(jnp.int32, (CHUNK, CHUNK), 1)
    for d in range(BIAS_NEAR):
        idx = jnp.clip(i - j + d * CHUNK, -REL_CLIP, REL_CLIP) + REL_CLIP

        def body(r, acc, idx=idx):
            return jnp.where(idx == r, table_ref[h, r], acc)

        lo = max(d * CHUNK - (CHUNK - 1), -REL_CLIP) + REL_CLIP
        hi = min(d * CHUNK + (CHUNK - 1), REL_CLIP) + REL_CLIP
        o_ref[0, d] = lax.fori_loop(lo, hi + 1, body, jnp.zeros((CHUNK, CHUNK), F32))


def _bias_blocks(table):
    nh = table.shape[0]
    return pl.pallas_call(
        _bias_blocks_kernel,
        grid=(nh,),
        in_specs=[pl.BlockSpec(memory_space=pltpu.SMEM)],
        out_specs=pl.BlockSpec((1, BIAS_NEAR, CHUNK, CHUNK), lambda h: (h, 0, 0, 0)),
        out_shape=jax.ShapeDtypeStruct((nh, BIAS_NEAR, CHUNK, CHUNK), F32),
        compiler_params=_params("arbitrary"),
        name="bias_blocks",
    )(table.astype(F32))


def _chunk_bias(blocks, far, dist):
    if dist < 0 or dist > BAND_LEFT_CHUNKS:
        return jnp.full(far.shape, NEG_BIG, F32)
    return blocks[:, dist] if dist < BIAS_NEAR else far


def _band_bias(blocks, table):
    far = jnp.broadcast_to(table[:, -1].astype(F32)[:, None, None], (table.shape[0], CHUNK, CHUNK))
    qc, kc = BAND_QB // CHUNK, BAND_KB * BAND_QB // CHUNK
    rows = [jnp.concatenate([_chunk_bias(blocks, far, r - c + BAND_LEFT_CHUNKS) for c in range(kc)], axis=2)
            for r in range(qc)]
    return jnp.concatenate(rows, axis=1)


def _band_step_bias(blocks, table, lb):
    far = jnp.broadcast_to(table[:, -1].astype(F32)[:, None, None], (table.shape[0], CHUNK, CHUNK))
    nc = lb // CHUNK
    return jnp.concatenate([blocks[:, nc - c] if nc - c < BIAS_NEAR else far for c in range(nc + 1)], axis=2)


def _softmax_pv_heads(scores, values):
    ps, dens = [], []
    for sc in scores:
        m = sc[0].max(axis=1, keepdims=True)
        for s in sc[1:]:
            m = jnp.maximum(m, s.max(axis=1, keepdims=True))
        p = [jnp.exp(s - m) for s in sc]
        den = p[0].sum(axis=1, keepdims=True)
        for pc in p[1:]:
            den = den + pc.sum(axis=1, keepdims=True)
        ps.append(p)
        dens.append(den)
    outs = []
    for p, den, vals in zip(ps, dens, values):
        num = _dot(p[0].astype(BF16), vals[0])
        for pc, v in zip(p[1:], vals[1:]):
            num = num + _dot(pc.astype(BF16), v)
        outs.append(num / den)
    return jnp.concatenate(outs, axis=1)


def _band_prompt_kernel(q_ref, k0_ref, k1_ref, k2_ref, v0_ref, v1_ref, v2_ref, bias_ref, o_ref):
    qi = pl.program_id(1)
    scores, values = [], []
    for h in range(N_HEADS):
        hs = _head_slice(h)
        q = q_ref[0, :, hs]
        sc = []
        for c, k_ref in enumerate((k0_ref, k1_ref, k2_ref)):
            s = _dot_nt(q, k_ref[0, :, hs]) + bias_ref[h, :, c * BAND_QB:(c + 1) * BAND_QB]
            if c < BAND_KB - 1:
                s = jnp.where(qi + c >= BAND_KB - 1, s, NEG_BIG)
            sc.append(s)
        scores.append(sc)
        values.append([v_ref[0, :, hs] for v_ref in (v0_ref, v1_ref, v2_ref)])
    o_ref[0] = _softmax_pv_heads(scores, values).astype(BF16)


def _band_prompt(proj3, bias):
    b, s, _ = proj3.shape
    qcol, kcol, vcol = 3, 4, 5

    def kv_spec(col, back):
        return pl.BlockSpec((1, BAND_QB, W_HEADS), lambda bi, qi: (bi, jnp.maximum(qi - back, 0), col))

    return pl.pallas_call(
        _band_prompt_kernel,
        grid=(b, s // BAND_QB),
        in_specs=[
            pl.BlockSpec((1, BAND_QB, W_HEADS), lambda bi, qi: (bi, qi, qcol)),
            kv_spec(kcol, 2), kv_spec(kcol, 1), kv_spec(kcol, 0),
            kv_spec(vcol, 2), kv_spec(vcol, 1), kv_spec(vcol, 0),
            pl.BlockSpec((N_HEADS, BAND_QB, BAND_KB * BAND_QB), lambda bi, qi: (0, 0, 0)),
        ],
        out_specs=pl.BlockSpec((1, BAND_QB, W_HEADS), lambda bi, qi: (bi, qi, 0)),
        out_shape=jax.ShapeDtypeStruct((b, s, W_HEADS), BF16),
        compiler_params=_params("arbitrary", "arbitrary"),
        name="band_prompt",
    )(proj3, proj3, proj3, proj3, proj3, proj3, proj3, bias)


def _band_step_kernel(q_ref, kn_ref, vn_ref, ck_ref, cv_ref, bias_ref, o_ref, *, lb):
    scores, values = [], []
    for h in range(N_HEADS):
        hs = _head_slice(h)
        q = q_ref[0, :, hs]
        rows = pl.ds(h, lb, stride=N_HEADS)
        scores.append([_dot_nt(q, ck_ref[0, rows, :].astype(BF16)) + bias_ref[h, :, :lb],
                       _dot_nt(q, kn_ref[0, :, hs]) + bias_ref[h, :, lb:]])
        values.append([cv_ref[0, rows, :].astype(BF16), vn_ref[0, :, hs]])
    o_ref[0] = _softmax_pv_heads(scores, values).astype(BF16)


def _band_step(proj3, cache_k, cache_v, bias):
    b, t, _ = proj3.shape
    lb = cache_k.shape[1] // N_HEADS
    cache_spec = pl.BlockSpec((1, lb * N_HEADS, HEAD_DIM), lambda bi: (bi, 0, 0))
    return pl.pallas_call(
        functools.partial(_band_step_kernel, lb=lb),
        grid=(b,),
        in_specs=[
            pl.BlockSpec((1, t, W_HEADS), lambda bi: (bi, 0, 3)),
            pl.BlockSpec((1, t, W_HEADS), lambda bi: (bi, 0, 4)),
            pl.BlockSpec((1, t, W_HEADS), lambda bi: (bi, 0, 5)),
            cache_spec, cache_spec,
            pl.BlockSpec((N_HEADS, t, lb + t), lambda bi: (0, 0, 0)),
        ],
        out_specs=pl.BlockSpec((1, t, W_HEADS), lambda bi: (bi, 0, 0)),
        out_shape=jax.ShapeDtypeStruct((b, t, W_HEADS), BF16),
        compiler_params=_params("arbitrary"),
        name="band_step",
    )(proj3, proj3, proj3, cache_k, cache_v, bias)


def _merge_kernel(osb_ref, obd_ref, wsb_ref, wbd_ref, ga_ref, gb_ref, o_ref):
    a = _dot(osb_ref[...], wsb_ref[...])
    b = _dot(obd_ref[...], wbd_ref[...])
    merged = _sigmoid(ga_ref[...].astype(F32)) * a + _sigmoid(gb_ref[...].astype(F32)) * b
    o_ref[...] = merged.astype(BF16)


def _merge(o_sb, o_band, w_sb_bf, w_band_bf, proj, tm):
    m = o_sb.shape[0]
    d = w_sb_bf.shape[1]
    tn = min(MATMUL_TN, d)
    ga_col = 6 * W_HEADS // tn
    gb_col = ga_col + d // tn
    return pl.pallas_call(
        _merge_kernel,
        grid=(m // tm, d // tn),
        in_specs=[
            pl.BlockSpec((tm, W_HEADS), lambda i, j: (i, 0)),
            pl.BlockSpec((tm, W_HEADS), lambda i, j: (i, 0)),
            pl.BlockSpec((W_HEADS, tn), lambda i, j: (0, j)),
            pl.BlockSpec((W_HEADS, tn), lambda i, j: (0, j)),
            pl.BlockSpec((tm, tn), lambda i, j: (i, ga_col + j)),
            pl.BlockSpec((tm, tn), lambda i, j: (i, gb_col + j)),
        ],
        out_specs=pl.BlockSpec((tm, tn), lambda i, j: (i, j)),
        out_shape=jax.ShapeDtypeStruct((m, d), BF16),
        compiler_params=_params("arbitrary", "arbitrary"),
        name="merge",
    )(o_sb, o_band, w_sb_bf, w_band_bf, proj, proj)


def _outproj_kernel(m_ref, w_ref, x_ref, gate_ref, o_ref, *, gb):
    acc = _dot(m_ref[...], w_ref[...])
    acc = acc.reshape(gb, ROW_GROUP, acc.shape[1])
    o_ref[...] = x_ref[...] + gate_ref[0] * acc


def _outproj(merged, w_out_bf, x3, mods, gb):
    ng, _, d = x3.shape
    tm = gb * ROW_GROUP
    tn = min(MATMUL_TN, d)
    return pl.pallas_call(
        functools.partial(_outproj_kernel, gb=gb),
        grid=(ng // gb, d // tn),
        in_specs=[
            pl.BlockSpec((tm, d), lambda i, j: (i, 0)),
            pl.BlockSpec((d, tn), lambda i, j: (0, j)),
            pl.BlockSpec((gb, ROW_GROUP, tn), lambda i, j: (i, 0, j)),
            pl.BlockSpec((1, gb, 1, tn), lambda i, j: (MOD_GATE_M, i, 0, j)),
        ],
        out_specs=pl.BlockSpec((gb, ROW_GROUP, tn), lambda i, j: (i, 0, j)),
        out_shape=jax.ShapeDtypeStruct(x3.shape, F32),
        compiler_params=_params("arbitrary", "arbitrary"),
        name="outproj",
    )(merged, w_out_bf, x3, mods)


GATHER_HALF = 128


def _tile_rows(d):
    assert d % V7X_LANES == 0
    return d // V7X_LANES


def _store_token_tiles(ref, first_token, x):
    n, d = x.shape
    s_per = _tile_rows(d)
    for c in range(s_per):
        ref[pl.ds(first_token * s_per + c, n, stride=s_per), :] = x[:, c * V7X_LANES:(c + 1) * V7X_LANES]


def _load_token_tiles(ref, n, d, first_token=0):
    s_per = _tile_rows(d)
    return jnp.concatenate([ref[pl.ds(first_token * s_per + c, n, stride=s_per), :] for c in range(s_per)], axis=1)


def _start_token_gather(table_hbm, idx_ref, first, buf, sem_of, n, s_per):
    for half in range(n // GATHER_HALF):
        def issue(i, carry, half=half):
            row = half * GATHER_HALF + i
            pltpu.make_async_copy(table_hbm.at[pl.ds(idx_ref[first + row] * s_per, s_per)],
                                  buf.at[pl.ds(row * s_per, s_per)], sem_of(half)).start()
            return carry
        lax.fori_loop(0, GATHER_HALF, issue, 0, unroll=8)


def _wait_token_gather(buf, sem_of, n, s_per):
    for half in range(n // GATHER_HALF):
        part = buf.at[pl.ds(half * GATHER_HALF * s_per, GATHER_HALF * s_per)]
        pltpu.make_async_copy(part, part, sem_of(half)).wait()


ROUTE_E1, ROUTE_E2, ROUTE_W1, ROUTE_W2 = 0, 1, 2, 3


def _router_kernel(xa_ref, sca_ref, sha_ref, xb_ref, scb_ref, shb_ref, g_ref, whi_ref, wlo_ref, b_ref,
                   h_ref, route_ref, hi_scr, lo_scr, *, gb, na):
    def fill(x_ref, sc_ref, sh_ref):
        def body(s, carry):
            h = _modulated_norm(x_ref[s], g_ref[...], sc_ref[0, s], sh_ref[0, s])
            hi = h.astype(BF16)
            first = pl.multiple_of(s * ROW_GROUP, ROW_GROUP)
            _store_token_tiles(h_ref, first, h)
            hi_scr[pl.ds(first, ROW_GROUP), :] = hi
            lo_scr[pl.ds(first, ROW_GROUP), :] = (h - hi.astype(F32)).astype(BF16)
            return carry
        lax.fori_loop(0, gb, body, 0)

    @pl.when(pl.program_id(0) < na)
    def _():
        fill(xa_ref, sca_ref, sha_ref)

    @pl.when(pl.program_id(0) >= na)
    def _():
        fill(xb_ref, scb_ref, shb_ref)

    hi = hi_scr[...]
    logits = _dot(hi, whi_ref[...]) + _dot(hi, wlo_ref[...]) + _dot(lo_scr[...], whi_ref[...]) + b_ref[...]
    lane = lax.broadcasted_iota(jnp.int32, logits.shape, 1)
    ninf = -jnp.inf

    lane_f = lane.astype(F32)

    def first_max(vals):
        mx = vals.max(axis=1, keepdims=True)
        idx = jnp.where(vals == mx, lane_f, float(V7X_LANES)).min(axis=1, keepdims=True)
        return mx, idx

    gl = jnp.where(lane < N_GROUPS, logits, ninf)
    gmax, gidx = first_max(gl)
    g_weight = 1.0 / jnp.exp(gl - gmax).sum(axis=1, keepdims=True)
    lo_lane = N_GROUPS + EXPERTS_PER_GROUP * gidx
    el = jnp.where(lane_f >= lo_lane, jnp.where(lane_f < lo_lane + EXPERTS_PER_GROUP, logits, ninf), ninf)
    m1, i1 = first_max(el)
    m2, i2 = first_max(jnp.where(lane_f == i1, ninf, el))
    e21 = jnp.exp(m2 - m1)
    p1 = 1.0 / (1.0 + e21)
    p2 = e21 / (1.0 + e21)
    route = jnp.where(lane == ROUTE_E1, i1 - N_GROUPS,
                      jnp.where(lane == ROUTE_E2, i2 - N_GROUPS,
                                jnp.where(lane == ROUTE_W1, g_weight * p1,
                                          jnp.where(lane == ROUTE_W2, g_weight * p2, 0.0))))
    route_ref[...] = route


def _router(xa3, mods_a, xb3, mods_b, g, w_hi, w_lo, bias, gb):
    na, nb = xa3.shape[0] // gb, xb3.shape[0] // gb
    d = xa3.shape[2]
    n_tokens = (na + nb) * gb * ROW_GROUP
    tm = gb * ROW_GROUP
    s_per = _tile_rows(d)

    def first(i):
        return jnp.minimum(i, na - 1)

    def second(i):
        return jnp.maximum(i - na, 0)

    def mod_spec(which, blk):
        return pl.BlockSpec((1, gb, 1, d), lambda i: (which, blk(i), 0, 0))

    w_spec = pl.BlockSpec((d, V7X_LANES), lambda i: (0, 0))
    return pl.pallas_call(
        functools.partial(_router_kernel, gb=gb, na=na),
        grid=(na + nb,),
        in_specs=[
            pl.BlockSpec((gb, ROW_GROUP, d), lambda i: (first(i), 0, 0)),
            mod_spec(MOD_SCALE_F, first), mod_spec(MOD_SHIFT_F, first),
            pl.BlockSpec((gb, ROW_GROUP, d), lambda i: (second(i), 0, 0)),
            mod_spec(MOD_SCALE_F, second), mod_spec(MOD_SHIFT_F, second),
            pl.BlockSpec((1, d), lambda i: (0, 0)),
            w_spec, w_spec,
            pl.BlockSpec((1, V7X_LANES), lambda i: (0, 0)),
        ],
        out_specs=[
            pl.BlockSpec((tm * s_per, V7X_LANES), lambda i: (i, 0)),
            pl.BlockSpec((tm, V7X_LANES), lambda i: (i, 0)),
        ],
        out_shape=[jax.ShapeDtypeStruct((n_tokens * s_per, V7X_LANES), F32),
                   jax.ShapeDtypeStruct((n_tokens, V7X_LANES), F32)],
        scratch_shapes=[pltpu.VMEM((tm, d), BF16), pltpu.VMEM((tm, d), BF16)],
        compiler_params=_params("arbitrary"),
        name="router",
    )(xa3, mods_a, mods_a, xb3, mods_b, mods_b, g, w_hi, w_lo, bias)


def _moe_kernel(te_ref, nv_ref, src_ref, h_hbm, wg_ref, wu_ref, wd_ref, o_ref, xbuf, sems, wg_scr, wu_scr, wd_scr):
    t = pl.program_id(0)
    n_valid = nv_ref[0]
    d = wg_scr.shape[0]
    s_per = _tile_rows(d)

    def start(tile, slot):
        _start_token_gather(h_hbm, src_ref, tile * MOE_TILE, xbuf.at[slot], lambda half: sems.at[slot, half],
                            MOE_TILE, s_per)

    @pl.when(t == 0)
    def _():
        start(0, 0)

    @pl.when(t + 1 < n_valid)
    def _():
        start(t + 1, (t + 1) % 2)

    changed = te_ref[t] != te_ref[jnp.maximum(t - 1, 0)]

    @pl.when((t == 0) | changed)
    def _():
        wg_scr[...] = wg_ref[0].astype(BF16)
        wu_scr[...] = wu_ref[0].astype(BF16)
        wd_scr[...] = wd_ref[0].astype(BF16)

    @pl.when(t < n_valid)
    def _():
        slot = t % 2
        _wait_token_gather(xbuf.at[slot], lambda half: sems.at[slot, half], MOE_TILE, s_per)

        x = _load_token_tiles(xbuf.at[slot], MOE_TILE, d).astype(BF16)
        g = _dot(x, wg_scr[...])
        u = _dot(x, wu_scr[...])
        hidden = (g * _sigmoid(g)) * u
        _store_token_tiles(o_ref, 0, _dot(hidden.astype(BF16), wd_scr[...]))

    @pl.when(t >= n_valid)
    def _():
        o_ref[...] = jnp.zeros(o_ref.shape, F32)


def _moe(tile_expert, n_valid, src_tok, h_tiles, w_gate, w_up, w_down):
    _, d, f = w_gate.shape
    s_per = _tile_rows(d)
    n_tiles = src_tok.shape[0] // MOE_TILE
    halves = MOE_TILE // GATHER_HALF
    grid_spec = pltpu.PrefetchScalarGridSpec(
        num_scalar_prefetch=3,
        grid=(n_tiles,),
        in_specs=[
            pl.BlockSpec(memory_space=pl.ANY),
            pl.BlockSpec((1, d, f), lambda t, te, nv, src: (te[t], 0, 0)),
            pl.BlockSpec((1, d, f), lambda t, te, nv, src: (te[t], 0, 0)),
            pl.BlockSpec((1, f, d), lambda t, te, nv, src: (te[t], 0, 0)),
        ],
        out_specs=pl.BlockSpec((MOE_TILE * s_per, V7X_LANES), lambda t, te, nv, src: (t, 0)),
        scratch_shapes=[pltpu.VMEM((2, MOE_TILE * s_per, V7X_LANES), F32), pltpu.SemaphoreType.DMA((2, halves)),
                        pltpu.VMEM((d, f), BF16), pltpu.VMEM((d, f), BF16), pltpu.VMEM((f, d), BF16)],
    )
    return pl.pallas_call(
        _moe_kernel,
        grid_spec=grid_spec,
        out_shape=jax.ShapeDtypeStruct((n_tiles * MOE_TILE * s_per, V7X_LANES), F32),
        compiler_params=_params("arbitrary"),
        name="moe",
    )(tile_expert, n_valid, src_tok, h_tiles, w_gate, w_up, w_down)


def _dispatch(route):
    n = route.shape[0]
    e = jnp.concatenate([route[:, ROUTE_E1], route[:, ROUTE_E2]]).astype(jnp.int32)
    tok = jnp.concatenate([jnp.arange(n, dtype=jnp.int32)] * 2)
    onehot = (e[:, None] == jnp.arange(N_EXPERTS, dtype=jnp.int32)[None, :]).astype(jnp.int32)
    before = jnp.cumsum(onehot, axis=0) - onehot
    rank = jnp.sum(before * onehot, axis=1)
    counts = jnp.sum(onehot, axis=0)
    padded = ((counts + MOE_TILE - 1) // MOE_TILE) * MOE_TILE
    ends = jnp.cumsum(padded)
    pos = (ends - padded)[e] + rank
    n_tiles = -(-(2 * n + N_EXPERTS * (MOE_TILE - 1)) // (4 * MOE_TILE)) * 4
    tile_start = jnp.arange(n_tiles, dtype=jnp.int32) * MOE_TILE
    n_valid = (ends[-1] // MOE_TILE).astype(jnp.int32)
    tile_expert = jnp.sum((ends[None, :] <= tile_start[:, None]).astype(jnp.int32), axis=1)
    last_expert = tile_expert[jnp.maximum(n_valid - 1, 0)]
    tile_expert = jnp.where(tile_start < ends[-1], tile_expert, last_expert)
    src_tok = jnp.zeros((n_tiles * MOE_TILE,), jnp.int32).at[pos].set(tok)
    return pos[:n], pos[n:], src_tok, tile_expert, n_valid.reshape(1)


def _final_kernel(p1_ref, p2_ref, x_ref, y_hbm, route_ref, gate_ref, o_ref, ybuf, sems, *, first_token):
    i = pl.program_id(0)
    gb, rg, d = x_ref.shape
    tm = gb * rg
    s_per = _tile_rows(d)

    def start(step, slot):
        for which, pos_ref in enumerate((p1_ref, p2_ref)):
            _start_token_gather(y_hbm, pos_ref, first_token + step * tm, ybuf.at[slot, which],
                                lambda half, which=which: sems.at[slot, which, half], tm, s_per)

    @pl.when(i == 0)
    def _():
        start(0, 0)

    @pl.when(i + 1 < pl.num_programs(0))
    def _():
        start(i + 1, (i + 1) % 2)

    slot = i % 2
    for which in range(2):
        _wait_token_gather(ybuf.at[slot, which], lambda half, which=which: sems.at[slot, which, half], tm, s_per)
    r = route_ref[...]
    y = (r[:, ROUTE_W1:ROUTE_W1 + 1] * _load_token_tiles(ybuf.at[slot, 0], tm, d)
         + r[:, ROUTE_W2:ROUTE_W2 + 1] * _load_token_tiles(ybuf.at[slot, 1], tm, d))
    o_ref[...] = x_ref[...] + gate_ref[0] * y.reshape(gb, rg, d)


def _final(x3, y_tiles, pos1, pos2, route, first_token, mods, gb):
    ng, _, d = x3.shape
    tm = gb * ROW_GROUP
    s_per = _tile_rows(d)
    assert first_token % tm == 0
    b0 = first_token // tm
    blk = pl.BlockSpec((gb, ROW_GROUP, d), lambda i, p1, p2: (i, 0, 0))
    grid_spec = pltpu.PrefetchScalarGridSpec(
        num_scalar_prefetch=2,
        grid=(ng // gb,),
        in_specs=[
            blk,
            pl.BlockSpec(memory_space=pl.ANY),
            pl.BlockSpec((tm, V7X_LANES), lambda i, p1, p2: (b0 + i, 0)),
            pl.BlockSpec((1, gb, 1, d), lambda i, p1, p2: (MOD_GATE_F, i, 0, 0)),
        ],
        out_specs=blk,
        scratch_shapes=[pltpu.VMEM((2, 2, tm * s_per, V7X_LANES), F32),
                        pltpu.SemaphoreType.DMA((2, 2, tm // GATHER_HALF))],
    )
    return pl.pallas_call(
        functools.partial(_final_kernel, first_token=first_token),
        grid_spec=grid_spec,
        out_shape=jax.ShapeDtypeStruct(x3.shape, F32),
        compiler_params=_params("arbitrary"),
        name="final",
    )(pos1, pos2, x3, y_tiles, route, mods)


def _layer(xp, xs, csk, csv, cbk, cbv, c_prompt, c_sample, norm_mix, norm_ffn, w_ada, b_ada, w_in, q_norm,
           k_norm, rel_table, w_proj_sb, w_proj_band, w_out, w_rg, b_rg, w_re, b_re, w_gate, w_up, w_down):
    bp, sp_len, d = xp.shape
    bs, ts, _ = xs.shape
    d_in = w_in.shape[1]
    assert sp_len % BAND_QB == 0 and ts == ROW_GROUP and sp_len % ROW_GROUP == 0

    n_c = bp + bs
    c_pad = jnp.concatenate([c_prompt, c_sample, jnp.zeros((-n_c % 8, d), F32)], axis=0)
    mod = _ada(c_pad, w_ada, b_ada)
    gp = sp_len // ROW_GROUP
    mod_p = jnp.repeat(mod[:, :bp], gp, axis=1)[:, :, None, :]
    mod_s = mod[:, bp:n_c][:, :, None, :]

    w_in_bf = w_in.astype(BF16)
    w_sb_bf = w_proj_sb.astype(BF16)
    w_band_bf = w_proj_band.astype(BF16)
    w_out_bf = w_out.astype(BF16)
    qn = q_norm.reshape(1, W_HEADS)
    kn = k_norm.reshape(1, W_HEADS)
    g_mix = norm_mix.reshape(1, d)
    g_ffn = norm_ffn.reshape(1, d)
    lb = cbk.shape[1]
    assert lb % CHUNK == 0 and ts == CHUNK
    bias_blocks = _bias_blocks(rel_table)
    bias = _band_bias(bias_blocks, rel_table)
    bias_step = _band_step_bias(bias_blocks, rel_table, lb)
    u = _strict_lower_twice(SB_T)

    w_router = jnp.zeros((d, V7X_LANES), F32).at[:, :N_GROUPS].set(w_rg)
    w_router = w_router.at[:, N_GROUPS:N_GROUPS + N_EXPERTS].set(w_re)
    w_router_hi = w_router.astype(BF16)
    w_router_lo = (w_router - w_router_hi.astype(F32)).astype(BF16)
    b_router = jnp.zeros((1, V7X_LANES), F32).at[0, :N_GROUPS].set(b_rg)
    b_router = b_router.at[0, N_GROUPS:N_GROUPS + N_EXPERTS].set(b_re.reshape(-1))

    xp3 = xp.reshape(bp * gp, ROW_GROUP, d)
    xs3 = xs.reshape(bs, ROW_GROUP, d)
    gb_p = min(ROWWISE_GROUPS, bp * gp)
    gb_s = min(ROWWISE_GROUPS, bs)

    def mixer(x3, mods, gb, attend):
        proj, ka, va, kb, vb = _proj(x3, g_mix, mods, w_in_bf, qn, kn, gb)
        o_sb, o_band = attend(proj)
        merged = _merge(o_sb, o_band, w_sb_bf, w_band_bf, proj, gb * ROW_GROUP)
        x1 = _outproj(merged, w_out_bf, x3, mods, gb)
        return x1, ka, va, kb, vb

    def attend_prompt(proj):
        p3 = proj.reshape(bp, sp_len, d_in)
        return (_sb_prompt(p3, u).reshape(bp * sp_len, W_HEADS),
                _band_prompt(p3, bias).reshape(bp * sp_len, W_HEADS))

    def attend_sample(proj):
        p3 = proj.reshape(bs, ts, d_in)
        def rows(cache):
            return cache.reshape(bs, -1, HEAD_DIM)

        o_sb = _sb_step(p3, rows(csk), rows(csv), u)
        o_band = _band_step(p3, rows(cbk), rows(cbv), bias_step)
        return o_sb.reshape(bs * ts, W_HEADS), o_band.reshape(bs * ts, W_HEADS)

    x1p, kap, vap, kbp, vbp = mixer(xp3, mod_p, min(MATMUL_GROUPS, bp * gp), attend_prompt)
    x1s, kas, vas, kbs, vbs = mixer(xs3, mod_s, min(MATMUL_GROUPS, bs), attend_sample)

    gb_r = min(gb_p, gb_s)
    h2, route = _router(x1p, mod_p, x1s, mod_s, g_ffn, w_router_hi, w_router_lo, b_router, gb_r)
    pos1, pos2, src_tok, tile_expert, n_valid = _dispatch(route)
    y_sorted = _moe(tile_expert, n_valid, src_tok, h2, w_gate, w_up, w_down)
    gb_f = min(FINAL_GROUPS, bs)
    yp = _final(x1p, y_sorted, pos1, pos2, route, 0, mod_p, gb_f).reshape(bp, sp_len, d)
    ys = _final(x1s, y_sorted, pos1, pos2, route, bp * sp_len, mod_s, gb_f).reshape(bs, ts, d)

    def heads(a, b):
        return a.reshape(b, -1, N_HEADS, HEAD_DIM)

    n_band = min(BAND_PAST, sp_len)
    return (yp, ys, heads(kap, bp), heads(vap, bp), heads(kbp, bp)[:, -n_band:], heads(vbp, bp)[:, -n_band:],
            heads(kas, bs), heads(vas, bs), heads(kbs, bs), heads(vbs, bs))


def kernel(x_prompt, x_sample, cache_sb_k, cache_sb_v, cache_band_k, cache_band_v, c_prompt, c_sample, norm_mix, norm_ffn, w_ada, b_ada, w_in, q_norm_band, k_norm_band, rel_bias_band, w_proj_sb, w_proj_band, w_out, w_router_group, b_router_group, w_router_expert, b_router_expert, w_gate, w_up, w_down):
    depth = w_in.shape[0]
    xp, xs = x_prompt, x_sample
    outs = [[] for _ in range(8)]
    for l in range(depth):
        res = _layer(xp, xs, cache_sb_k[l], cache_sb_v[l], cache_band_k[l], cache_band_v[l], c_prompt, c_sample,
                     norm_mix[l], norm_ffn[l], w_ada[l], b_ada[l], w_in[l], q_norm_band[l], k_norm_band[l],
                     rel_bias_band[l], w_proj_sb[l], w_proj_band[l], w_out[l], w_router_group[l],
                     b_router_group[l], w_router_expert[l], b_router_expert[l], w_gate[l], w_up[l], w_down[l])
        xp, xs = res[0], res[1]
        for acc, r in zip(outs, res[2:]):
            acc.append(r)
    return (xp, xs) + tuple(jnp.stack(o, axis=0) for o in outs)
```

```python
import functools

import jax
import jax.numpy as jnp
from jax import lax
from jax.experimental import pallas as pl
from jax.experimental.pallas import tpu as pltpu

F32 = jnp.float32
BF16 = jnp.bfloat16

EPS = 1e-6
HEAD_DIM = 128
N_HEADS = 8
W_HEADS = N_HEADS * HEAD_DIM
CHUNK = 64
BAND_LEFT_CHUNKS = 8
BAND_PAST = BAND_LEFT_CHUNKS * CHUNK
REL_CLIP = 128
N_GROUPS = 4
EXPERTS_PER_GROUP = 8
N_EXPERTS = N_GROUPS * EXPERTS_PER_GROUP
QK_SCALE = HEAD_DIM ** -0.5
NEG_BIG = -1e30

V7X_LANES = 128
V7X_VMEM_LIMIT = 56 * 1024 * 1024
ROW_GROUP = 64
MOE_TILE = 256
MATMUL_GROUPS = 16
MATMUL_TN = 1024
PROJ_TN = 1024
ROWWISE_GROUPS = 8
FINAL_GROUPS = 4


def _params(*sem):
    return pltpu.CompilerParams(dimension_semantics=sem, vmem_limit_bytes=V7X_VMEM_LIMIT)


def _sigmoid(x):
    return 1.0 / (1.0 + jnp.exp(-x))


def _dot(a, b):
    return jnp.dot(a, b, preferred_element_type=F32)


def _dot_nt(a, b):
    return lax.dot_general(a, b, (((1,), (1,)), ((), ())), preferred_element_type=F32)


def _ada_kernel(c_ref, w_ref, b_ref, o_ref):
    c = c_ref[...]
    a = (c * _sigmoid(c)).astype(BF16)
    o_ref[0] = _dot(a, w_ref[...].astype(BF16)) + b_ref[...]


def _ada(c_pad, w_ada, b_ada):
    r, d = c_pad.shape
    tn = min(1024, d)
    per = d // tn
    return pl.pallas_call(
        _ada_kernel,
        grid=(6 * per,),
        in_specs=[
            pl.BlockSpec((r, d), lambda j: (0, 0)),
            pl.BlockSpec((d, tn), lambda j: (0, j)),
            pl.BlockSpec((1, tn), lambda j: (0, j)),
        ],
        out_specs=pl.BlockSpec((1, r, tn), lambda j: (j // per, 0, j % per)),
        out_shape=jax.ShapeDtypeStruct((6, r, d), F32),
        compiler_params=_params("arbitrary"),
        name="ada",
    )(c_pad, w_ada, b_ada.reshape(1, 6 * d))


def _modulated_norm(x, g, scale, shift):
    ms = jnp.mean(x * x, axis=-1, keepdims=True)
    return (x * lax.rsqrt(ms + EPS) * g) * (1.0 + scale) + shift


def _head_norm(a, gain_ref):
    outs = []
    for hh in range(a.shape[1] // HEAD_DIM):
        blk = a[:, hh * HEAD_DIM:(hh + 1) * HEAD_DIM]
        ms = jnp.mean(blk * blk, axis=-1, keepdims=True)
        outs.append(blk * lax.rsqrt(ms + EPS) * gain_ref[:, hh * HEAD_DIM:(hh + 1) * HEAD_DIM])
    return jnp.concatenate(outs, axis=1)


def _proj_kernel(x_ref, g_ref, sc_ref, sh_ref, w_ref, qn_ref, kn_ref,
                 proj_ref, ka_ref, va_ref, kb_ref, vb_ref, h_scr, *, gb, per):
    j = pl.program_id(1)

    @pl.when(j == 0)
    def _():
        def body(s, carry):
            h = _modulated_norm(x_ref[s], g_ref[...], sc_ref[0, s], sh_ref[0, s])
            h_scr[pl.ds(pl.multiple_of(s * ROW_GROUP, ROW_GROUP), ROW_GROUP), :] = h.astype(BF16)
            return carry
        lax.fori_loop(0, gb, body, 0)

    acc = _dot(h_scr[...], w_ref[...])
    sec = j // per

    @pl.when(sec == 0)
    def _():
        proj_ref[...] = (acc * QK_SCALE).astype(BF16)

    @pl.when(sec == 1)
    def _():
        proj_ref[...] = acc.astype(BF16)
        ka_ref[...] = acc

    @pl.when(sec == 2)
    def _():
        proj_ref[...] = acc.astype(BF16)
        va_ref[...] = acc

    @pl.when(sec == 3)
    def _():
        proj_ref[...] = (_head_norm(acc, qn_ref) * QK_SCALE).astype(BF16)

    @pl.when(sec == 4)
    def _():
        n = _head_norm(acc, kn_ref)
        proj_ref[...] = n.astype(BF16)
        kb_ref[...] = n

    @pl.when(sec == 5)
    def _():
        proj_ref[...] = acc.astype(BF16)
        vb_ref[...] = acc

    @pl.when(sec >= 6)
    def _():
        proj_ref[...] = acc.astype(BF16)


MOD_SHIFT_M, MOD_SCALE_M, MOD_GATE_M, MOD_SHIFT_F, MOD_SCALE_F, MOD_GATE_F = range(6)


def _proj(x3, g, mods, w_in_bf, qn, kn, gb):
    ng, _, d = x3.shape
    d_in = w_in_bf.shape[1]
    m = ng * ROW_GROUP
    tm = gb * ROW_GROUP
    tn = PROJ_TN
    per = W_HEADS // tn

    def sect(s):
        return pl.BlockSpec((tm, tn), lambda i, j: (i, jnp.clip(j - s * per, 0, per - 1)),
                            pipeline_mode=pl.Buffered(1))

    def mod_spec(which):
        return pl.BlockSpec((1, gb, 1, d), lambda i, j: (which, i, 0, 0))

    kv_shape = jax.ShapeDtypeStruct((m, W_HEADS), F32)
    return pl.pallas_call(
        functools.partial(_proj_kernel, gb=gb, per=per),
        grid=(ng // gb, d_in // tn),
        in_specs=[
            pl.BlockSpec((gb, ROW_GROUP, d), lambda i, j: (i, 0, 0)),
            pl.BlockSpec((1, d), lambda i, j: (0, 0)),
            mod_spec(MOD_SCALE_M), mod_spec(MOD_SHIFT_M),
            pl.BlockSpec((d, tn), lambda i, j: (0, j)),
            pl.BlockSpec((1, tn), lambda i, j: (0, jnp.clip(j - 3 * per, 0, per - 1))),
            pl.BlockSpec((1, tn), lambda i, j: (0, jnp.clip(j - 4 * per, 0, per - 1))),
        ],
        out_specs=[
            pl.BlockSpec((tm, tn), lambda i, j: (i, j)),
            sect(1), sect(2), sect(4), sect(5),
        ],
        out_shape=[jax.ShapeDtypeStruct((m, d_in), BF16), kv_shape, kv_shape, kv_shape, kv_shape],
        scratch_shapes=[pltpu.VMEM((tm, d), BF16)],
        compiler_params=_params("arbitrary", "arbitrary"),
        name="proj",
    )(x3, g, mods, mods, w_in_bf, qn, kn)


def _lower_twice(n):
    row = lax.broadcasted_iota(jnp.int32, (2 * n, n), 0)
    col = lax.broadcasted_iota(jnp.int32, (2 * n, n), 1)
    row = jnp.where(row >= n, row - n, row)
    return jnp.where(row >= col, 1.0, 0.0).astype(BF16)


def _sb_scores(load_qk, n_heads):
    return jnp.concatenate([_dot_nt(*load_qk(h)) for h in range(n_heads)], axis=1)


def _sb_apply(z_all, load_v, n_heads, u, carries, diagonal):
    tq = z_all.shape[0]
    tk = z_all.shape[1] // n_heads
    zs = [z_all[:, h * tk:(h + 1) * tk] for h in range(n_heads)]
    if diagonal:
        row = lax.broadcasted_iota(jnp.int32, (tq, tk), 0)
        col = lax.broadcasted_iota(jnp.int32, (tq, tk), 1)
        valid = col < row
    tails = []
    for z in zs:
        sp = jnp.maximum(z, 0.0) + jnp.log(1.0 + jnp.exp(-jnp.abs(z)))
        if diagonal:
            sp = jnp.where(valid, sp, 0.0)
        hi = sp.astype(BF16)
        lo = (sp - hi.astype(F32)).astype(BF16)
        tails.append(_dot(jnp.concatenate([hi, lo], axis=1), u))
    pvs, new = [], []
    for h, (z, tail) in enumerate(zip(zs, tails)):
        carry = jnp.zeros((tq, 1), F32) if carries is None else carries[:, h * HEAD_DIM:h * HEAD_DIM + 1]
        loga = z - tail - carry
        if diagonal:
            loga = jnp.where(valid, loga, NEG_BIG)
        pvs.append(_dot(jnp.exp(loga).astype(BF16), load_v(h)))
        new.append(jnp.broadcast_to(carry + tail[:, 0:1], (tq, HEAD_DIM)))
    return jnp.concatenate(pvs, axis=1), jnp.concatenate(new, axis=1)


SB_T = 256
SB_HEADS_PER_STEP = 8
SB_CACHE_ROWS = 2048


def _head_slice(h):
    return slice(h * HEAD_DIM, (h + 1) * HEAD_DIM)


def _sb_prompt_kernel(q_ref, k_ref, v_ref, u_ref, o_ref, acc_scr, carry_scr):
    qi = pl.program_id(2)
    u = u_ref[...]
    nh = SB_HEADS_PER_STEP

    def block(blk, carries, diagonal):
        r = pl.multiple_of(blk * SB_T, SB_T)
        z_all = _sb_scores(lambda h: (q_ref[0, :, _head_slice(h)], k_ref[0, pl.ds(r, SB_T), _head_slice(h)]), nh)
        return _sb_apply(z_all, lambda h: v_ref[0, pl.ds(r, SB_T), _head_slice(h)], nh, u, carries, diagonal)

    acc_scr[...], carry_scr[...] = block(qi, None, True)

    def body(step, c):
        pv, carries = block(qi - 1 - step, carry_scr[...], False)
        acc_scr[...] += pv
        carry_scr[...] = carries
        return c

    lax.fori_loop(0, qi, body, 0)
    o_ref[0] = acc_scr[...].astype(BF16)


def _sb_prompt(proj3, u):
    b, s, _ = proj3.shape
    hp = SB_HEADS_PER_STEP
    w = hp * HEAD_DIM
    kcol, vcol = W_HEADS // w, 2 * W_HEADS // w
    return pl.pallas_call(
        _sb_prompt_kernel,
        grid=(b, N_HEADS // hp, s // SB_T),
        in_specs=[
            pl.BlockSpec((1, SB_T, w), lambda bi, h, qi: (bi, qi, h)),
            pl.BlockSpec((1, s, w), lambda bi, h, qi: (bi, 0, kcol + h)),
            pl.BlockSpec((1, s, w), lambda bi, h, qi: (bi, 0, vcol + h)),
            pl.BlockSpec((2 * SB_T, SB_T), lambda bi, h, qi: (0, 0)),
        ],
        out_specs=pl.BlockSpec((1, SB_T, w), lambda bi, h, qi: (bi, qi, h)),
        out_shape=jax.ShapeDtypeStruct((b, s, W_HEADS), BF16),
        scratch_shapes=[pltpu.VMEM((SB_T, w), F32), pltpu.VMEM((SB_T, w), F32)],
        compiler_params=_params("arbitrary", "arbitrary", "arbitrary"),
        name="sb_prompt",
    )(proj3, proj3, proj3, u)


def _sb_step_kernel(q_ref, kn_ref, vn_ref, ck_ref, cv_ref, u_ref, o_ref, acc_scr, carry_scr, *, tkb, t):
    j = pl.program_id(1)
    u = u_ref[...]

    @pl.when(j == 0)
    def _():
        z_new = _sb_scores(lambda h: (q_ref[0, :, _head_slice(h)], kn_ref[0, :, _head_slice(h)]), N_HEADS)
        acc_scr[...], carry_scr[...] = _sb_apply(z_new, lambda h: vn_ref[0, :, _head_slice(h)], N_HEADS,
                                                 _lower_twice(t), None, True)

    def rows(sub, h):
        first = pl.multiple_of((tkb - (sub + 1) * SB_T) * N_HEADS, SB_T * N_HEADS)
        return pl.ds(first + h, SB_T, stride=N_HEADS)

    def body(sub, c):
        z_all = _sb_scores(lambda h: (q_ref[0, :, _head_slice(h)], ck_ref[0, rows(sub, h), :].astype(BF16)), N_HEADS)
        pv, carries = _sb_apply(z_all, lambda h: cv_ref[0, rows(sub, h), :].astype(BF16), N_HEADS, u,
                                carry_scr[...], False)
        acc_scr[...] += pv
        carry_scr[...] = carries
        return c

    lax.fori_loop(0, tkb // SB_T, body, 0)

    @pl.when(j == pl.num_programs(1) - 1)
    def _():
        o_ref[0] = acc_scr[...].astype(BF16)


def _sb_step(proj3, cache_k, cache_v, u):
    b, t, _ = proj3.shape
    p = cache_k.shape[1] // N_HEADS
    tkb = min(SB_CACHE_ROWS, p)
    nj = p // tkb
    cache_spec = pl.BlockSpec((1, tkb * N_HEADS, HEAD_DIM), lambda bi, j: (bi, nj - 1 - j, 0))
    return pl.pallas_call(
        functools.partial(_sb_step_kernel, tkb=tkb, t=t),
        grid=(b, nj),
        in_specs=[
            pl.BlockSpec((1, t, W_HEADS), lambda bi, j: (bi, 0, 0)),
            pl.BlockSpec((1, t, W_HEADS), lambda bi, j: (bi, 0, 1)),
            pl.BlockSpec((1, t, W_HEADS), lambda bi, j: (bi, 0, 2)),
            cache_spec, cache_spec,
            pl.BlockSpec((2 * SB_T, SB_T), lambda bi, j: (0, 0)),
        ],
        out_specs=pl.BlockSpec((1, t, W_HEADS), lambda bi, j: (bi, 0, 0)),
        out_shape=jax.ShapeDtypeStruct((b, t, W_HEADS), BF16),
        scratch_shapes=[pltpu.VMEM((t, W_HEADS), F32), pltpu.VMEM((t, W_HEADS), F32)],
        compiler_params=_params("arbitrary", "arbitrary"),
        name="sb_step",
    )(proj3, proj3, proj3, cache_k, cache_v, u)


BAND_QB = 4 * CHUNK
BAND_KB = 3


BIAS_NEAR = REL_CLIP // CHUNK + 1


def _bias_blocks_kernel(table_ref, o_ref):
    h = pl.program_id(0)
    i = lax.broadcasted_iota(jnp.int32, (CHUNK, CHUNK), 0)
    j = lax.broadcasted_iota(jnp.int32, (CHUNK, CHUNK), 1)
    for d in range(BIAS_NEAR):
        idx = jnp.clip(i - j + d * CHUNK, -REL_CLIP, REL_CLIP) + REL_CLIP

        def body(r, acc, idx=idx):
            return jnp.where(idx == r, table_ref[h, r], acc)

        lo = max(d * CHUNK - (CHUNK - 1), -REL_CLIP) + REL_CLIP
        hi = min(d * CHUNK + (CHUNK - 1), REL_CLIP) + REL_CLIP
        o_ref[0, d] = lax.fori_loop(lo, hi + 1, body, jnp.zeros((CHUNK, CHUNK), F32))


def _bias_blocks(table):
    nh = table.shape[0]
    return pl.pallas_call(
        _bias_blocks_kernel,
        grid=(nh,),
        in_specs=[pl.BlockSpec(memory_space=pltpu.SMEM)],
        out_specs=pl.BlockSpec((1, BIAS_NEAR, CHUNK, CHUNK), lambda h: (h, 0, 0, 0)),
        out_shape=jax.ShapeDtypeStruct((nh, BIAS_NEAR, CHUNK, CHUNK), F32),
        compiler_params=_params("arbitrary"),
        name="bias_blocks",
    )(table.astype(F32))


def _chunk_bias(blocks, far, dist):
    if dist < 0 or dist > BAND_LEFT_CHUNKS:
        return jnp.full(far.shape, NEG_BIG, F32)
    return blocks[:, dist] if dist < BIAS_NEAR else far


def _band_bias(blocks, table):
    far = jnp.broadcast_to(table[:, -1].astype(F32)[:, None, None], (table.shape[0], CHUNK, CHUNK))
    qc, kc = BAND_QB // CHUNK, BAND_KB * BAND_QB // CHUNK
    rows = [jnp.concatenate([_chunk_bias(blocks, far, r - c + BAND_LEFT_CHUNKS) for c in range(kc)], axis=2)
            for r in range(qc)]
    return jnp.concatenate(rows, axis=1)


def _band_step_bias(blocks, table, lb):
    far = jnp.broadcast_to(table[:, -1].astype(F32)[:, None, None], (table.shape[0], CHUNK, CHUNK))
    nc = lb // CHUNK
    return jnp.concatenate([blocks[:, nc - c] if nc - c < BIAS_NEAR else far for c in range(nc + 1)], axis=2)


def _softmax_pv_heads(scores, values):
    ps, dens = [], []
    for sc in scores:
        m = sc[0].max(axis=1, keepdims=True)
        for s in sc[1:]:
            m = jnp.maximum(m, s.max(axis=1, keepdims=True))
        p = [jnp.exp(s - m) for s in sc]
        den = p[0].sum(axis=1, keepdims=True)
        for pc in p[1:]:
            den = den + pc.sum(axis=1, keepdims=True)
        ps.append(p)
        dens.append(den)
    outs = []
    for p, den, vals in zip(ps, dens, values):
        num = _dot(p[0].astype(BF16), vals[0])
        for pc, v in zip(p[1:], vals[1:]):
            num = num + _dot(pc.astype(BF16), v)
        outs.append(num / den)
    return jnp.concatenate(outs, axis=1)


def _band_prompt_kernel(q_ref, k0_ref, k1_ref, k2_ref, v0_ref, v1_ref, v2_ref, bias_ref, o_ref):
    qi = pl.program_id(1)
    scores, values = [], []
    for h in range(N_HEADS):
        hs = _head_slice(h)
        q = q_ref[0, :, hs]
        sc = []
        for c, k_ref in enumerate((k0_ref, k1_ref, k2_ref)):
            s = _dot_nt(q, k_ref[0, :, hs]) + bias_ref[h, :, c * BAND_QB:(c + 1) * BAND_QB]
            if c < BAND_KB - 1:
                s = jnp.where(qi + c >= BAND_KB - 1, s, NEG_BIG)
            sc.append(s)
        scores.append(sc)
        values.append([v_ref[0, :, hs] for v_ref in (v0_ref, v1_ref, v2_ref)])
    o_ref[0] = _softmax_pv_heads(scores, values).astype(BF16)


def _band_prompt(proj3, bias):
    b, s, _ = proj3.shape
    qcol, kcol, vcol = 3, 4, 5

    def kv_spec(col, back):
        return pl.BlockSpec((1, BAND_QB, W_HEADS), lambda bi, qi: (bi, jnp.maximum(qi - back, 0), col))

    return pl.pallas_call(
        _band_prompt_kernel,
        grid=(b, s // BAND_QB),
        in_specs=[
            pl.BlockSpec((1, BAND_QB, W_HEADS), lambda bi, qi: (bi, qi, qcol)),
            kv_spec(kcol, 2), kv_spec(kcol, 1), kv_spec(kcol, 0),
            kv_spec(vcol, 2), kv_spec(vcol, 1), kv_spec(vcol, 0),
            pl.BlockSpec((N_HEADS, BAND_QB, BAND_KB * BAND_QB), lambda bi, qi: (0, 0, 0)),
        ],
        out_specs=pl.BlockSpec((1, BAND_QB, W_HEADS), lambda bi, qi: (bi, qi, 0)),
        out_shape=jax.ShapeDtypeStruct((b, s, W_HEADS), BF16),
        compiler_params=_params("arbitrary", "arbitrary"),
        name="band_prompt",
    )(proj3, proj3, proj3, proj3, proj3, proj3, proj3, bias)


def _band_step_kernel(q_ref, kn_ref, vn_ref, ck_ref, cv_ref, bias_ref, o_ref, *, lb):
    scores, values = [], []
    for h in range(N_HEADS):
        hs = _head_slice(h)
        q = q_ref[0, :, hs]
        rows = pl.ds(h, lb, stride=N_HEADS)
        scores.append([_dot_nt(q, ck_ref[0, rows, :].astype(BF16)) + bias_ref[h, :, :lb],
                       _dot_nt(q, kn_ref[0, :, hs]) + bias_ref[h, :, lb:]])
        values.append([cv_ref[0, rows, :].astype(BF16), vn_ref[0, :, hs]])
    o_ref[0] = _softmax_pv_heads(scores, values).astype(BF16)


def _band_step(proj3, cache_k, cache_v, bias):
    b, t, _ = proj3.shape
    lb = cache_k.shape[1] // N_HEADS
    cache_spec = pl.BlockSpec((1, lb * N_HEADS, HEAD_DIM), lambda bi: (bi, 0, 0))
    return pl.pallas_call(
        functools.partial(_band_step_kernel, lb=lb),
        grid=(b,),
        in_specs=[
            pl.BlockSpec((1, t, W_HEADS), lambda bi: (bi, 0, 3)),
            pl.BlockSpec((1, t, W_HEADS), lambda bi: (bi, 0, 4)),
            pl.BlockSpec((1, t, W_HEADS), lambda bi: (bi, 0, 5)),
            cache_spec, cache_spec,
            pl.BlockSpec((N_HEADS, t, lb + t), lambda bi: (0, 0, 0)),
        ],
        out_specs=pl.BlockSpec((1, t, W_HEADS), lambda bi: (bi, 0, 0)),
        out_shape=jax.ShapeDtypeStruct((b, t, W_HEADS), BF16),
        compiler_params=_params("arbitrary"),
        name="band_step",
    )(proj3, proj3, proj3, cache_k, cache_v, bias)


def _merge_kernel(osb_ref, obd_ref, wsb_ref, wbd_ref, ga_ref, gb_ref, o_ref):
    a = _dot(osb_ref[...], wsb_ref[...])
    b = _dot(obd_ref[...], wbd_ref[...])
    merged = _sigmoid(ga_ref[...].astype(F32)) * a + _sigmoid(gb_ref[...].astype(F32)) * b
    o_ref[...] = merged.astype(BF16)


def _merge(o_sb, o_band, w_sb_bf, w_band_bf, proj, tm):
    m = o_sb.shape[0]
    d = w_sb_bf.shape[1]
    tn = min(MATMUL_TN, d)
    ga_col = 6 * W_HEADS // tn
    gb_col = ga_col + d // tn
    return pl.pallas_call(
        _merge_kernel,
        grid=(m // tm, d // tn),
        in_specs=[
            pl.BlockSpec((tm, W_HEADS), lambda i, j: (i, 0)),
            pl.BlockSpec((tm, W_HEADS), lambda i, j: (i, 0)),
            pl.BlockSpec((W_HEADS, tn), lambda i, j: (0, j)),
            pl.BlockSpec((W_HEADS, tn), lambda i, j: (0, j)),
            pl.BlockSpec((tm, tn), lambda i, j: (i, ga_col + j)),
            pl.BlockSpec((tm, tn), lambda i, j: (i, gb_col + j)),
        ],
        out_specs=pl.BlockSpec((tm, tn), lambda i, j: (i, j)),
        out_shape=jax.ShapeDtypeStruct((m, d), BF16),
        compiler_params=_params("arbitrary", "arbitrary"),
        name="merge",
    )(o_sb, o_band, w_sb_bf, w_band_bf, proj, proj)


def _outproj_kernel(m_ref, w_ref, x_ref, gate_ref, o_ref, *, gb):
    acc = _dot(m_ref[...], w_ref[...])
    acc = acc.reshape(gb, ROW_GROUP, acc.shape[1])
    o_ref[...] = x_ref[...] + gate_ref[0] * acc


def _outproj(merged, w_out_bf, x3, mods, gb):
    ng, _, d = x3.shape
    tm = gb * ROW_GROUP
    tn = min(MATMUL_TN, d)
    return pl.pallas_call(
        functools.partial(_outproj_kernel, gb=gb),
        grid=(ng // gb, d // tn),
        in_specs=[
            pl.BlockSpec((tm, d), lambda i, j: (i, 0)),
            pl.BlockSpec((d, tn), lambda i, j: (0, j)),
            pl.BlockSpec((gb, ROW_GROUP, tn), lambda i, j: (i, 0, j)),
            pl.BlockSpec((1, gb, 1, tn), lambda i, j: (MOD_GATE_M, i, 0, j)),
        ],
        out_specs=pl.BlockSpec((gb, ROW_GROUP, tn), lambda i, j: (i, 0, j)),
        out_shape=jax.ShapeDtypeStruct(x3.shape, F32),
        compiler_params=_params("arbitrary", "arbitrary"),
        name="outproj",
    )(merged, w_out_bf, x3, mods)


GATHER_HALF = 128


def _tile_rows(d):
    assert d % V7X_LANES == 0
    return d // V7X_LANES


def _store_token_tiles(ref, first_token, x):
    n, d = x.shape
    s_per = _tile_rows(d)
    for c in range(s_per):
        ref[pl.ds(first_token * s_per + c, n, stride=s_per), :] = x[:, c * V7X_LANES:(c + 1) * V7X_LANES]


def _load_token_tiles(ref, n, d, first_token=0):
    s_per = _tile_rows(d)
    return jnp.concatenate([ref[pl.ds(first_token * s_per + c, n, stride=s_per), :] for c in range(s_per)], axis=1)


def _start_token_gather(table_hbm, idx_ref, first, buf, sem_of, n, s_per):
    for half in range(n // GATHER_HALF):
        def issue(i, carry, half=half):
            row = half * GATHER_HALF + i
            pltpu.make_async_copy(table_hbm.at[pl.ds(idx_ref[first + row] * s_per, s_per)],
                                  buf.at[pl.ds(row * s_per, s_per)], sem_of(half)).start()
            return carry
        lax.fori_loop(0, GATHER_HALF, issue, 0, unroll=8)


def _wait_token_gather(buf, sem_of, n, s_per):
    for half in range(n // GATHER_HALF):
        part = buf.at[pl.ds(half * GATHER_HALF * s_per, GATHER_HALF * s_per)]
        pltpu.make_async_copy(part, part, sem_of(half)).wait()


ROUTE_E1, ROUTE_E2, ROUTE_W1, ROUTE_W2 = 0, 1, 2, 3


def _router_kernel(xa_ref, sca_ref, sha_ref, xb_ref, scb_ref, shb_ref, g_ref, whi_ref, wlo_ref, b_ref,
                   h_ref, route_ref, hi_scr, lo_scr, *, gb, na):
    def fill(x_ref, sc_ref, sh_ref):
        def body(s, carry):
            h = _modulated_norm(x_ref[s], g_ref[...], sc_ref[0, s], sh_ref[0, s])
            hi = h.astype(BF16)
            first = pl.multiple_of(s * ROW_GROUP, ROW_GROUP)
            _store_token_tiles(h_ref, first, h)
            hi_scr[pl.ds(first, ROW_GROUP), :] = hi
            lo_scr[pl.ds(first, ROW_GROUP), :] = (h - hi.astype(F32)).astype(BF16)
            return carry
        lax.fori_loop(0, gb, body, 0)

    @pl.when(pl.program_id(0) < na)
    def _():
        fill(xa_ref, sca_ref, sha_ref)

    @pl.when(pl.program_id(0) >= na)
    def _():
        fill(xb_ref, scb_ref, shb_ref)

    hi = hi_scr[...]
    logits = _dot(hi, whi_ref[...]) + _dot(hi, wlo_ref[...]) + _dot(lo_scr[...], whi_ref[...]) + b_ref[...]
    lane = lax.broadcasted_iota(jnp.int32, logits.shape, 1)
    ninf = -jnp.inf

    lane_f = lane.astype(F32)

    def first_max(vals):
        mx = vals.max(axis=1, keepdims=True)
        idx = jnp.where(vals == mx, lane_f, float(V7X_LANES)).min(axis=1, keepdims=True)
        return mx, idx

    gl = jnp.where(lane < N_GROUPS, logits, ninf)
    gmax, gidx = first_max(gl)
    g_weight = 1.0 / jnp.exp(gl - gmax).sum(axis=1, keepdims=True)
    lo_lane = N_GROUPS + EXPERTS_PER_GROUP * gidx
    el = jnp.where(lane_f >= lo_lane, jnp.where(lane_f < lo_lane + EXPERTS_PER_GROUP, logits, ninf), ninf)
    m1, i1 = first_max(el)
    m2, i2 = first_max(jnp.where(lane_f == i1, ninf, el))
    e21 = jnp.exp(m2 - m1)
    p1 = 1.0 / (1.0 + e21)
    p2 = e21 / (1.0 + e21)
    route = jnp.where(lane == ROUTE_E1, i1 - N_GROUPS,
                      jnp.where(lane == ROUTE_E2, i2 - N_GROUPS,
                                jnp.where(lane == ROUTE_W1, g_weight * p1,
                                          jnp.where(lane == ROUTE_W2, g_weight * p2, 0.0))))
    route_ref[...] = route


def _router(xa3, mods_a, xb3, mods_b, g, w_hi, w_lo, bias, gb):
    na, nb = xa3.shape[0] // gb, xb3.shape[0] // gb
    d = xa3.shape[2]
    n_tokens = (na + nb) * gb * ROW_GROUP
    tm = gb * ROW_GROUP
    s_per = _tile_rows(d)

    def first(i):
        return jnp.minimum(i, na - 1)

    def second(i):
        return jnp.maximum(i - na, 0)

    def mod_spec(which, blk):
        return pl.BlockSpec((1, gb, 1, d), lambda i: (which, blk(i), 0, 0))

    w_spec = pl.BlockSpec((d, V7X_LANES), lambda i: (0, 0))
    return pl.pallas_call(
        functools.partial(_router_kernel, gb=gb, na=na),
        grid=(na + nb,),
        in_specs=[
            pl.BlockSpec((gb, ROW_GROUP, d), lambda i: (first(i), 0, 0)),
            mod_spec(MOD_SCALE_F, first), mod_spec(MOD_SHIFT_F, first),
            pl.BlockSpec((gb, ROW_GROUP, d), lambda i: (second(i), 0, 0)),
            mod_spec(MOD_SCALE_F, second), mod_spec(MOD_SHIFT_F, second),
            pl.BlockSpec((1, d), lambda i: (0, 0)),
            w_spec, w_spec,
            pl.BlockSpec((1, V7X_LANES), lambda i: (0, 0)),
        ],
        out_specs=[
            pl.BlockSpec((tm * s_per, V7X_LANES), lambda i: (i, 0)),
            pl.BlockSpec((tm, V7X_LANES), lambda i: (i, 0)),
        ],
        out_shape=[jax.ShapeDtypeStruct((n_tokens * s_per, V7X_LANES), F32),
                   jax.ShapeDtypeStruct((n_tokens, V7X_LANES), F32)],
        scratch_shapes=[pltpu.VMEM((tm, d), BF16), pltpu.VMEM((tm, d), BF16)],
        compiler_params=_params("arbitrary"),
        name="router",
    )(xa3, mods_a, mods_a, xb3, mods_b, mods_b, g, w_hi, w_lo, bias)


def _moe_kernel(te_ref, nv_ref, src_ref, h_hbm, wg_ref, wu_ref, wd_ref, o_ref, xbuf, sems, wg_scr, wu_scr, wd_scr):
    t = pl.program_id(0)
    n_valid = nv_ref[0]
    d = wg_scr.shape[0]
    s_per = _tile_rows(d)

    def start(tile, slot):
        _start_token_gather(h_hbm, src_ref, tile * MOE_TILE, xbuf.at[slot], lambda half: sems.at[slot, half],
                            MOE_TILE, s_per)

    @pl.when(t == 0)
    def _():
        start(0, 0)

    @pl.when(t + 1 < n_valid)
    def _():
        start(t + 1, (t + 1) % 2)

    changed = te_ref[t] != te_ref[jnp.maximum(t - 1, 0)]

    @pl.when((t == 0) | changed)
    def _():
        wg_scr[...] = wg_ref[0].astype(BF16)
        wu_scr[...] = wu_ref[0].astype(BF16)
        wd_scr[...] = wd_ref[0].astype(BF16)

    @pl.when(t < n_valid)
    def _():
        slot = t % 2
        _wait_token_gather(xbuf.at[slot], lambda half: sems.at[slot, half], MOE_TILE, s_per)

        x = _load_token_tiles(xbuf.at[slot], MOE_TILE, d).astype(BF16)
        g = _dot(x, wg_scr[...])
        u = _dot(x, wu_scr[...])
        hidden = (g * _sigmoid(g)) * u
        _store_token_tiles(o_ref, 0, _dot(hidden.astype(BF16), wd_scr[...]))

    @pl.when(t >= n_valid)
    def _():
        o_ref[...] = jnp.zeros(o_ref.shape, F32)


def _moe(tile_expert, n_valid, src_tok, h_tiles, w_gate, w_up, w_down):
    _, d, f = w_gate.shape
    s_per = _tile_rows(d)
    n_tiles = src_tok.shape[0] // MOE_TILE
    halves = MOE_TILE // GATHER_HALF
    grid_spec = pltpu.PrefetchScalarGridSpec(
        num_scalar_prefetch=3,
        grid=(n_tiles,),
        in_specs=[
            pl.BlockSpec(memory_space=pl.ANY),
            pl.BlockSpec((1, d, f), lambda t, te, nv, src: (te[t], 0, 0)),
            pl.BlockSpec((1, d, f), lambda t, te, nv, src: (te[t], 0, 0)),
            pl.BlockSpec((1, f, d), lambda t, te, nv, src: (te[t], 0, 0)),
        ],
        out_specs=pl.BlockSpec((MOE_TILE * s_per, V7X_LANES), lambda t, te, nv, src: (t, 0)),
        scratch_shapes=[pltpu.VMEM((2, MOE_TILE * s_per, V7X_LANES), F32), pltpu.SemaphoreType.DMA((2, halves)),
                        pltpu.VMEM((d, f), BF16), pltpu.VMEM((d, f), BF16), pltpu.VMEM((f, d), BF16)],
    )
    return pl.pallas_call(
        _moe_kernel,
        grid_spec=grid_spec,
        out_shape=jax.ShapeDtypeStruct((n_tiles * MOE_TILE * s_per, V7X_LANES), F32),
        compiler_params=_params("arbitrary"),
        name="moe",
    )(tile_expert, n_valid, src_tok, h_tiles, w_gate, w_up, w_down)


def _dispatch(route):
    n = route.shape[0]
    e = jnp.concatenate([route[:, ROUTE_E1], route[:, ROUTE_E2]]).astype(jnp.int32)
    tok = jnp.concatenate([jnp.arange(n, dtype=jnp.int32)] * 2)
    onehot = (e[:, None] == jnp.arange(N_EXPERTS, dtype=jnp.int32)[None, :]).astype(jnp.int32)
    before = jnp.cumsum(onehot, axis=0) - onehot
    rank = jnp.sum(before * onehot, axis=1)
    counts = jnp.sum(onehot, axis=0)
    padded = ((counts + MOE_TILE - 1) // MOE_TILE) * MOE_TILE
    ends = jnp.cumsum(padded)
    pos = (ends - padded)[e] + rank
    n_tiles = -(-(2 * n + N_EXPERTS * (MOE_TILE - 1)) // (4 * MOE_TILE)) * 4
    tile_start = jnp.arange(n_tiles, dtype=jnp.int32) * MOE_TILE
    n_valid = (ends[-1] // MOE_TILE).astype(jnp.int32)
    tile_expert = jnp.sum((ends[None, :] <= tile_start[:, None]).astype(jnp.int32), axis=1)
    last_expert = tile_expert[jnp.maximum(n_valid - 1, 0)]
    tile_expert = jnp.where(tile_start < ends[-1], tile_expert, last_expert)
    src_tok = jnp.zeros((n_tiles * MOE_TILE,), jnp.int32).at[pos].set(tok)
    return pos[:n], pos[n:], src_tok, tile_expert, n_valid.reshape(1)


def _final_kernel(p1_ref, p2_ref, x_ref, y_hbm, route_ref, gate_ref, o_ref, ybuf, sems, *, first_token):
    i = pl.program_id(0)
    gb, rg, d = x_ref.shape
    tm = gb * rg
    s_per = _tile_rows(d)

    def start(step, slot):
        for which, pos_ref in enumerate((p1_ref, p2_ref)):
            _start_token_gather(y_hbm, pos_ref, first_token + step * tm, ybuf.at[slot, which],
                                lambda half, which=which: sems.at[slot, which, half], tm, s_per)

    @pl.when(i == 0)
    def _():
        start(0, 0)

    @pl.when(i + 1 < pl.num_programs(0))
    def _():
        start(i + 1, (i + 1) % 2)

    slot = i % 2
    for which in range(2):
        _wait_token_gather(ybuf.at[slot, which], lambda half, which=which: sems.at[slot, which, half], tm, s_per)
    r = route_ref[...]
    y = (r[:, ROUTE_W1:ROUTE_W1 + 1] * _load_token_tiles(ybuf.at[slot, 0], tm, d)
         + r[:, ROUTE_W2:ROUTE_W2 + 1] * _load_token_tiles(ybuf.at[slot, 1], tm, d))
    o_ref[...] = x_ref[...] + gate_ref[0] * y.reshape(gb, rg, d)


def _final(x3, y_tiles, pos1, pos2, route, first_token, mods, gb):
    ng, _, d = x3.shape
    tm = gb * ROW_GROUP
    s_per = _tile_rows(d)
    assert first_token % tm == 0
    b0 = first_token // tm
    blk = pl.BlockSpec((gb, ROW_GROUP, d), lambda i, p1, p2: (i, 0, 0))
    grid_spec = pltpu.PrefetchScalarGridSpec(
        num_scalar_prefetch=2,
        grid=(ng // gb,),
        in_specs=[
            blk,
            pl.BlockSpec(memory_space=pl.ANY),
            pl.BlockSpec((tm, V7X_LANES), lambda i, p1, p2: (b0 + i, 0)),
            pl.BlockSpec((1, gb, 1, d), lambda i, p1, p2: (MOD_GATE_F, i, 0, 0)),
        ],
        out_specs=blk,
        scratch_shapes=[pltpu.VMEM((2, 2, tm * s_per, V7X_LANES), F32),
                        pltpu.SemaphoreType.DMA((2, 2, tm // GATHER_HALF))],
    )
    return pl.pallas_call(
        functools.partial(_final_kernel, first_token=first_token),
        grid_spec=grid_spec,
        out_shape=jax.ShapeDtypeStruct(x3.shape, F32),
        compiler_params=_params("arbitrary"),
        name="final",
    )(pos1, pos2, x3, y_tiles, route, mods)


def _layer(xp, xs, csk, csv, cbk, cbv, c_prompt, c_sample, norm_mix, norm_ffn, w_ada, b_ada, w_in, q_norm,
           k_norm, rel_table, w_proj_sb, w_proj_band, w_out, w_rg, b_rg, w_re, b_re, w_gate, w_up, w_down):
    bp, sp_len, d = xp.shape
    bs, ts, _ = xs.shape
    d_in = w_in.shape[1]
    assert sp_len % BAND_QB == 0 and ts == ROW_GROUP and sp_len % ROW_GROUP == 0

    n_c = bp + bs
    c_pad = jnp.concatenate([c_prompt, c_sample, jnp.zeros((-n_c % 8, d), F32)], axis=0)
    mod = _ada(c_pad, w_ada, b_ada)
    gp = sp_len // ROW_GROUP
    mod_p = jnp.repeat(mod[:, :bp], gp, axis=1)[:, :, None, :]
    mod_s = mod[:, bp:n_c][:, :, None, :]

    w_in_bf = w_in.astype(BF16)
    w_sb_bf = w_proj_sb.astype(BF16)
    w_band_bf = w_proj_band.astype(BF16)
    w_out_bf = w_out.astype(BF16)
    qn = q_norm.reshape(1, W_HEADS)
    kn = k_norm.reshape(1, W_HEADS)
    g_mix = norm_mix.reshape(1, d)
    g_ffn = norm_ffn.reshape(1, d)
    lb = cbk.shape[1]
    assert lb % CHUNK == 0 and ts == CHUNK
    bias_blocks = _bias_blocks(rel_table)
    bias = _band_bias(bias_blocks, rel_table)
    bias_step = _band_step_bias(bias_blocks, rel_table, lb)
    u = _lower_twice(SB_T)

    w_router = jnp.zeros((d, V7X_LANES), F32).at[:, :N_GROUPS].set(w_rg)
    w_router = w_router.at[:, N_GROUPS:N_GROUPS + N_EXPERTS].set(w_re)
    w_router_hi = w_router.astype(BF16)
    w_router_lo = (w_router - w_router_hi.astype(F32)).astype(BF16)
    b_router = jnp.zeros((1, V7X_LANES), F32).at[0, :N_GROUPS].set(b_rg)
    b_router = b_router.at[0, N_GROUPS:N_GROUPS + N_EXPERTS].set(b_re.reshape(-1))

    xp3 = xp.reshape(bp * gp, ROW_GROUP, d)
    xs3 = xs.reshape(bs, ROW_GROUP, d)
    gb_p = min(ROWWISE_GROUPS, bp * gp)
    gb_s = min(ROWWISE_GROUPS, bs)

    def mixer(x3, mods, gb, attend):
        proj, ka, va, kb, vb = _proj(x3, g_mix, mods, w_in_bf, qn, kn, gb)
        o_sb, o_band = attend(proj)
        merged = _merge(o_sb, o_band, w_sb_bf, w_band_bf, proj, gb * ROW_GROUP)
        x1 = _outproj(merged, w_out_bf, x3, mods, gb)
        return x1, ka, va, kb, vb

    def attend_prompt(proj):
        p3 = proj.reshape(bp, sp_len, d_in)
        return (_sb_prompt(p3, u).reshape(bp * sp_len, W_HEADS),
                _band_prompt(p3, bias).reshape(bp * sp_len, W_HEADS))

    def attend_sample(proj):
        p3 = proj.reshape(bs, ts, d_in)
        def rows(cache):
            return cache.reshape(bs, -1, HEAD_DIM)

        o_sb = _sb_step(p3, rows(csk), rows(csv), u)
        o_band = _band_step(p3, rows(cbk), rows(cbv), bias_step)
        return o_sb.reshape(bs * ts, W_HEADS), o_band.reshape(bs * ts, W_HEADS)

    x1p, kap, vap, kbp, vbp = mixer(xp3, mod_p, min(MATMUL_GROUPS, bp * gp), attend_prompt)
    x1s, kas, vas, kbs, vbs = mixer(xs3, mod_s, min(MATMUL_GROUPS, bs), attend_sample)

    gb_r = min(gb_p, gb_s)
    h2, route = _router(x1p, mod_p, x1s, mod_s, g_ffn, w_router_hi, w_router_lo, b_router, gb_r)
    pos1, pos2, src_tok, tile_expert, n_valid = _dispatch(route)
    y_sorted = _moe(tile_expert, n_valid, src_tok, h2, w_gate, w_up, w_down)
    gb_f = min(FINAL_GROUPS, bs)
    yp = _final(x1p, y_sorted, pos1, pos2, route, 0, mod_p, gb_f).reshape(bp, sp_len, d)
    ys = _final(x1s, y_sorted, pos1, pos2, route, bp * sp_len, mod_s, gb_f).reshape(bs, ts, d)

    def heads(a, b):
        return a.reshape(b, -1, N_HEADS, HEAD_DIM)

    n_band = min(BAND_PAST, sp_len)
    return (yp, ys, heads(kap, bp), heads(vap, bp), heads(kbp, bp)[:, -n_band:], heads(vbp, bp)[:, -n_band:],
            heads(kas, bs), heads(vas, bs), heads(kbs, bs), heads(vbs, bs))


def kernel(x_prompt, x_sample, cache_sb_k, cache_sb_v, cache_band_k, cache_band_v, c_prompt, c_sample, norm_mix, norm_ffn, w_ada, b_ada, w_in, q_norm_band, k_norm_band, rel_bias_band, w_proj_sb, w_proj_band, w_out, w_router_group, b_router_group, w_router_expert, b_router_expert, w_gate, w_up, w_down):
    depth = w_in.shape[0]
    xp, xs = x_prompt, x_sample
    outs = [[] for _ in range(8)]
    for l in range(depth):
        res = _layer(xp, xs, cache_sb_k[l], cache_sb_v[l], cache_band_k[l], cache_band_v[l], c_prompt, c_sample,
                     norm_mix[l], norm_ffn[l], w_ada[l], b_ada[l], w_in[l], q_norm_band[l], k_norm_band[l],
                     rel_bias_band[l], w_proj_sb[l], w_proj_band[l], w_out[l], w_router_group[l],
                     b_router_group[l], w_router_expert[l], b_router_expert[l], w_gate[l], w_up[l], w_down[l])
        xp, xs = res[0], res[1]
        for acc, r in zip(outs, res[2:]):
            acc.append(r)
    return (xp, xs) + tuple(jnp.stack(o, axis=0) for o in outs)
```

```python
import functools

import jax
import jax.numpy as jnp
from jax import lax
from jax.experimental import pallas as pl
from jax.experimental.pallas import tpu as pltpu

F32 = jnp.float32
BF16 = jnp.bfloat16

EPS = 1e-6
HEAD_DIM = 128
N_HEADS = 8
W_HEADS = N_HEADS * HEAD_DIM
CHUNK = 64
BAND_LEFT_CHUNKS = 8
BAND_PAST = BAND_LEFT_CHUNKS * CHUNK
REL_CLIP = 128
N_GROUPS = 4
EXPERTS_PER_GROUP = 8
N_EXPERTS = N_GROUPS * EXPERTS_PER_GROUP
QK_SCALE = HEAD_DIM ** -0.5
NEG_BIG = -1e30

V7X_LANES = 128
V7X_VMEM_LIMIT = 56 * 1024 * 1024
ROW_GROUP = 64
MOE_TILE = 256
MATMUL_GROUPS = 16
MATMUL_TN = 1024
PROJ_TN = 1024
ROWWISE_GROUPS = 8
FINAL_GROUPS = 8


def _params(*sem):
    return pltpu.CompilerParams(dimension_semantics=sem, vmem_limit_bytes=V7X_VMEM_LIMIT)


def _sigmoid(x):
    return 1.0 / (1.0 + jnp.exp(-x))


def _dot(a, b):
    return jnp.dot(a, b, preferred_element_type=F32)


def _dot_nt(a, b):
    return lax.dot_general(a, b, (((1,), (1,)), ((), ())), preferred_element_type=F32)


def _ada_kernel(c_ref, w_ref, b_ref, o_ref):
    c = c_ref[...]
    a = (c * _sigmoid(c)).astype(BF16)
    o_ref[0] = _dot(a, w_ref[...].astype(BF16)) + b_ref[...]


def _ada(c_pad, w_ada, b_ada):
    r, d = c_pad.shape
    tn = min(1024, d)
    per = d // tn
    return pl.pallas_call(
        _ada_kernel,
        grid=(6 * per,),
        in_specs=[
            pl.BlockSpec((r, d), lambda j: (0, 0)),
            pl.BlockSpec((d, tn), lambda j: (0, j)),
            pl.BlockSpec((1, tn), lambda j: (0, j)),
        ],
        out_specs=pl.BlockSpec((1, r, tn), lambda j: (j // per, 0, j % per)),
        out_shape=jax.ShapeDtypeStruct((6, r, d), F32),
        compiler_params=_params("arbitrary"),
        name="ada",
    )(c_pad, w_ada, b_ada.reshape(1, 6 * d))


def _modulated_norm(x, g, scale, shift):
    ms = jnp.mean(x * x, axis=-1, keepdims=True)
    return (x * lax.rsqrt(ms + EPS) * g) * (1.0 + scale) + shift


def _head_norm(a, gain_ref):
    outs = []
    for hh in range(a.shape[1] // HEAD_DIM):
        blk = a[:, hh * HEAD_DIM:(hh + 1) * HEAD_DIM]
        ms = jnp.mean(blk * blk, axis=-1, keepdims=True)
        outs.append(blk * lax.rsqrt(ms + EPS) * gain_ref[:, hh * HEAD_DIM:(hh + 1) * HEAD_DIM])
    return jnp.concatenate(outs, axis=1)


def _proj_kernel(x_ref, g_ref, sc_ref, sh_ref, w_ref, qn_ref, kn_ref,
                 proj_ref, ka_ref, va_ref, kb_ref, vb_ref, h_scr, *, gb, per):
    j = pl.program_id(1)

    @pl.when(j == 0)
    def _():
        def body(s, carry):
            h = _modulated_norm(x_ref[s], g_ref[...], sc_ref[0, s], sh_ref[0, s])
            h_scr[pl.ds(pl.multiple_of(s * ROW_GROUP, ROW_GROUP), ROW_GROUP), :] = h.astype(BF16)
            return carry
        lax.fori_loop(0, gb, body, 0)

    acc = _dot(h_scr[...], w_ref[...])
    sec = j // per

    @pl.when(sec == 0)
    def _():
        proj_ref[...] = (acc * QK_SCALE).astype(BF16)

    @pl.when(sec == 1)
    def _():
        proj_ref[...] = acc.astype(BF16)
        ka_ref[...] = acc

    @pl.when(sec == 2)
    def _():
        proj_ref[...] = acc.astype(BF16)
        va_ref[...] = acc

    @pl.when(sec == 3)
    def _():
        proj_ref[...] = (_head_norm(acc, qn_ref) * QK_SCALE).astype(BF16)

    @pl.when(sec == 4)
    def _():
        n = _head_norm(acc, kn_ref)
        proj_ref[...] = n.astype(BF16)
        kb_ref[...] = n

    @pl.when(sec == 5)
    def _():
        proj_ref[...] = acc.astype(BF16)
        vb_ref[...] = acc

    @pl.when(sec >= 6)
    def _():
        proj_ref[...] = acc.astype(BF16)


MOD_SHIFT_M, MOD_SCALE_M, MOD_GATE_M, MOD_SHIFT_F, MOD_SCALE_F, MOD_GATE_F = range(6)


def _proj(x3, g, mods, w_in_bf, qn, kn, gb):
    ng, _, d = x3.shape
    d_in = w_in_bf.shape[1]
    m = ng * ROW_GROUP
    tm = gb * ROW_GROUP
    tn = PROJ_TN
    per = W_HEADS // tn

    def sect(s):
        return pl.BlockSpec((tm, tn), lambda i, j: (i, jnp.clip(j - s * per, 0, per - 1)),
                            pipeline_mode=pl.Buffered(1))

    def mod_spec(which):
        return pl.BlockSpec((1, gb, 1, d), lambda i, j: (which, i, 0, 0))

    kv_shape = jax.ShapeDtypeStruct((m, W_HEADS), F32)
    return pl.pallas_call(
        functools.partial(_proj_kernel, gb=gb, per=per),
        grid=(ng // gb, d_in // tn),
        in_specs=[
            pl.BlockSpec((gb, ROW_GROUP, d), lambda i, j: (i, 0, 0)),
            pl.BlockSpec((1, d), lambda i, j: (0, 0)),
            mod_spec(MOD_SCALE_M), mod_spec(MOD_SHIFT_M),
            pl.BlockSpec((d, tn), lambda i, j: (0, j)),
            pl.BlockSpec((1, tn), lambda i, j: (0, jnp.clip(j - 3 * per, 0, per - 1))),
            pl.BlockSpec((1, tn), lambda i, j: (0, jnp.clip(j - 4 * per, 0, per - 1))),
        ],
        out_specs=[
            pl.BlockSpec((tm, tn), lambda i, j: (i, j)),
            sect(1), sect(2), sect(4), sect(5),
        ],
        out_shape=[jax.ShapeDtypeStruct((m, d_in), BF16), kv_shape, kv_shape, kv_shape, kv_shape],
        scratch_shapes=[pltpu.VMEM((tm, d), BF16)],
        compiler_params=_params("arbitrary", "arbitrary"),
        name="proj",
    )(x3, g, mods, mods, w_in_bf, qn, kn)


def _lower_twice(n):
    row = lax.broadcasted_iota(jnp.int32, (2 * n, n), 0)
    col = lax.broadcasted_iota(jnp.int32, (2 * n, n), 1)
    row = jnp.where(row >= n, row - n, row)
    return jnp.where(row >= col, 1.0, 0.0).astype(BF16)


def _sb_scores(load_qk, n_heads):
    return jnp.concatenate([_dot_nt(*load_qk(h)) for h in range(n_heads)], axis=1)


def _sb_apply(z_all, load_v, n_heads, u, carries, diagonal):
    tq = z_all.shape[0]
    tk = z_all.shape[1] // n_heads
    zs = [z_all[:, h * tk:(h + 1) * tk] for h in range(n_heads)]
    if diagonal:
        row = lax.broadcasted_iota(jnp.int32, (tq, tk), 0)
        col = lax.broadcasted_iota(jnp.int32, (tq, tk), 1)
        valid = col < row
    tails = []
    for z in zs:
        sp = jnp.maximum(z, 0.0) + jnp.log(1.0 + jnp.exp(-jnp.abs(z)))
        if diagonal:
            sp = jnp.where(valid, sp, 0.0)
        hi = sp.astype(BF16)
        lo = (sp - hi.astype(F32)).astype(BF16)
        tails.append(_dot(jnp.concatenate([hi, lo], axis=1), u))
    pvs, new = [], []
    for h, (z, tail) in enumerate(zip(zs, tails)):
        carry = jnp.zeros((tq, 1), F32) if carries is None else carries[:, h * HEAD_DIM:h * HEAD_DIM + 1]
        loga = z - tail - carry
        if diagonal:
            loga = jnp.where(valid, loga, NEG_BIG)
        pvs.append(_dot(jnp.exp(loga).astype(BF16), load_v(h)))
        new.append(jnp.broadcast_to(carry + tail[:, 0:1], (tq, HEAD_DIM)))
    return jnp.concatenate(pvs, axis=1), jnp.concatenate(new, axis=1)


SB_T = 256
SB_HEADS_PER_STEP = 8
SB_CACHE_ROWS = 2048


def _head_slice(h):
    return slice(h * HEAD_DIM, (h + 1) * HEAD_DIM)


def _sb_prompt_kernel(q_ref, k_ref, v_ref, u_ref, o_ref, acc_scr, carry_scr):
    qi = pl.program_id(2)
    u = u_ref[...]
    nh = SB_HEADS_PER_STEP

    def block(blk, carries, diagonal):
        r = pl.multiple_of(blk * SB_T, SB_T)
        z_all = _sb_scores(lambda h: (q_ref[0, :, _head_slice(h)], k_ref[0, pl.ds(r, SB_T), _head_slice(h)]), nh)
        return _sb_apply(z_all, lambda h: v_ref[0, pl.ds(r, SB_T), _head_slice(h)], nh, u, carries, diagonal)

    acc_scr[...], carry_scr[...] = block(qi, None, True)

    def body(step, c):
        pv, carries = block(qi - 1 - step, carry_scr[...], False)
        acc_scr[...] += pv
        carry_scr[...] = carries
        return c

    lax.fori_loop(0, qi, body, 0)
    o_ref[0] = acc_scr[...].astype(BF16)


def _sb_prompt(proj3, u):
    b, s, _ = proj3.shape
    hp = SB_HEADS_PER_STEP
    w = hp * HEAD_DIM
    kcol, vcol = W_HEADS // w, 2 * W_HEADS // w
    return pl.pallas_call(
        _sb_prompt_kernel,
        grid=(b, N_HEADS // hp, s // SB_T),
        in_specs=[
            pl.BlockSpec((1, SB_T, w), lambda bi, h, qi: (bi, qi, h)),
            pl.BlockSpec((1, s, w), lambda bi, h, qi: (bi, 0, kcol + h)),
            pl.BlockSpec((1, s, w), lambda bi, h, qi: (bi, 0, vcol + h)),
            pl.BlockSpec((2 * SB_T, SB_T), lambda bi, h, qi: (0, 0)),
        ],
        out_specs=pl.BlockSpec((1, SB_T, w), lambda bi, h, qi: (bi, qi, h)),
        out_shape=jax.ShapeDtypeStruct((b, s, W_HEADS), BF16),
        scratch_shapes=[pltpu.VMEM((SB_T, w), F32), pltpu.VMEM((SB_T, w), F32)],
        compiler_params=_params("arbitrary", "arbitrary", "arbitrary"),
        name="sb_prompt",
    )(proj3, proj3, proj3, u)


def _sb_step_kernel(q_ref, kn_ref, vn_ref, ck_ref, cv_ref, u_ref, o_ref, acc_scr, carry_scr, *, tkb, t):
    j = pl.program_id(1)
    u = u_ref[...]

    @pl.when(j == 0)
    def _():
        z_new = _sb_scores(lambda h: (q_ref[0, :, _head_slice(h)], kn_ref[0, :, _head_slice(h)]), N_HEADS)
        acc_scr[...], carry_scr[...] = _sb_apply(z_new, lambda h: vn_ref[0, :, _head_slice(h)], N_HEADS,
                                                 _lower_twice(t), None, True)

    def rows(sub, h):
        first = pl.multiple_of((tkb - (sub + 1) * SB_T) * N_HEADS, SB_T * N_HEADS)
        return pl.ds(first + h, SB_T, stride=N_HEADS)

    def body(sub, c):
        z_all = _sb_scores(lambda h: (q_ref[0, :, _head_slice(h)], ck_ref[0, rows(sub, h), :].astype(BF16)), N_HEADS)
        pv, carries = _sb_apply(z_all, lambda h: cv_ref[0, rows(sub, h), :].astype(BF16), N_HEADS, u,
                                carry_scr[...], False)
        acc_scr[...] += pv
        carry_scr[...] = carries
        return c

    lax.fori_loop(0, tkb // SB_T, body, 0)

    @pl.when(j == pl.num_programs(1) - 1)
    def _():
        o_ref[0] = acc_scr[...].astype(BF16)


def _sb_step(proj3, cache_k, cache_v, u):
    b, t, _ = proj3.shape
    p = cache_k.shape[1] // N_HEADS
    tkb = min(SB_CACHE_ROWS, p)
    nj = p // tkb
    cache_spec = pl.BlockSpec((1, tkb * N_HEADS, HEAD_DIM), lambda bi, j: (bi, nj - 1 - j, 0))
    return pl.pallas_call(
        functools.partial(_sb_step_kernel, tkb=tkb, t=t),
        grid=(b, nj),
        in_specs=[
            pl.BlockSpec((1, t, W_HEADS), lambda bi, j: (bi, 0, 0)),
            pl.BlockSpec((1, t, W_HEADS), lambda bi, j: (bi, 0, 1)),
            pl.BlockSpec((1, t, W_HEADS), lambda bi, j: (bi, 0, 2)),
            cache_spec, cache_spec,
            pl.BlockSpec((2 * SB_T, SB_T), lambda bi, j: (0, 0)),
        ],
        out_specs=pl.BlockSpec((1, t, W_HEADS), lambda bi, j: (bi, 0, 0)),
        out_shape=jax.ShapeDtypeStruct((b, t, W_HEADS), BF16),
        scratch_shapes=[pltpu.VMEM((t, W_HEADS), F32), pltpu.VMEM((t, W_HEADS), F32)],
        compiler_params=_params("arbitrary", "arbitrary"),
        name="sb_step",
    )(proj3, proj3, proj3, cache_k, cache_v, u)


BAND_QB = 4 * CHUNK
BAND_KB = 3


BIAS_NEAR = REL_CLIP // CHUNK + 1


def _bias_blocks_kernel(table_ref, o_ref):
    h = pl.program_id(0)
    i = lax.broadcasted_iota(jnp.int32, (CHUNK, CHUNK), 0)
    j = lax.broadcasted_iota(jnp.int32, (CHUNK, CHUNK), 1)
    for d in range(BIAS_NEAR):
        idx = jnp.clip(i - j + d * CHUNK, -REL_CLIP, REL_CLIP) + REL_CLIP

        def body(r, acc, idx=idx):
            return jnp.where(idx == r, table_ref[h, r], acc)

        lo = max(d * CHUNK - (CHUNK - 1), -REL_CLIP) + REL_CLIP
        hi = min(d * CHUNK + (CHUNK - 1), REL_CLIP) + REL_CLIP
        o_ref[0, d] = lax.fori_loop(lo, hi + 1, body, jnp.zeros((CHUNK, CHUNK), F32))


def _bias_blocks(table):
    nh = table.shape[0]
    return pl.pallas_call(
        _bias_blocks_kernel,
        grid=(nh,),
        in_specs=[pl.BlockSpec(memory_space=pltpu.SMEM)],
        out_specs=pl.BlockSpec((1, BIAS_NEAR, CHUNK, CHUNK), lambda h: (h, 0, 0, 0)),
        out_shape=jax.ShapeDtypeStruct((nh, BIAS_NEAR, CHUNK, CHUNK), F32),
        compiler_params=_params("arbitrary"),
        name="bias_blocks",
    )(table.astype(F32))


def _chunk_bias(blocks, far, dist):
    if dist < 0 or dist > BAND_LEFT_CHUNKS:
        return jnp.full(far.shape, NEG_BIG, F32)
    return blocks[:, dist] if dist < BIAS_NEAR else far


def _band_bias(blocks, table):
    far = jnp.broadcast_to(table[:, -1].astype(F32)[:, None, None], (table.shape[0], CHUNK, CHUNK))
    qc, kc = BAND_QB // CHUNK, BAND_KB * BAND_QB // CHUNK
    rows = [jnp.concatenate([_chunk_bias(blocks, far, r - c + BAND_LEFT_CHUNKS) for c in range(kc)], axis=2)
            for r in range(qc)]
    return jnp.concatenate(rows, axis=1)


def _band_step_bias(blocks, table, lb):
    far = jnp.broadcast_to(table[:, -1].astype(F32)[:, None, None], (table.shape[0], CHUNK, CHUNK))
    nc = lb // CHUNK
    return jnp.concatenate([blocks[:, nc - c] if nc - c < BIAS_NEAR else far for c in range(nc + 1)], axis=2)


def _softmax_pv_heads(scores, values):
    ps, dens = [], []
    for sc in scores:
        m = sc[0].max(axis=1, keepdims=True)
        for s in sc[1:]:
            m = jnp.maximum(m, s.max(axis=1, keepdims=True))
        p = [jnp.exp(s - m) for s in sc]
        den = p[0].sum(axis=1, keepdims=True)
        for pc in p[1:]:
            den = den + pc.sum(axis=1, keepdims=True)
        ps.append(p)
        dens.append(den)
    outs = []
    for p, den, vals in zip(ps, dens, values):
        num = _dot(p[0].astype(BF16), vals[0])
        for pc, v in zip(p[1:], vals[1:]):
            num = num + _dot(pc.astype(BF16), v)
        outs.append(num / den)
    return jnp.concatenate(outs, axis=1)


def _band_prompt_kernel(q_ref, k0_ref, k1_ref, k2_ref, v0_ref, v1_ref, v2_ref, bias_ref, o_ref):
    qi = pl.program_id(1)
    scores, values = [], []
    for h in range(N_HEADS):
        hs = _head_slice(h)
        q = q_ref[0, :, hs]
        sc = []
        for c, k_ref in enumerate((k0_ref, k1_ref, k2_ref)):
            s = _dot_nt(q, k_ref[0, :, hs]) + bias_ref[h, :, c * BAND_QB:(c + 1) * BAND_QB]
            if c < BAND_KB - 1:
                s = jnp.where(qi + c >= BAND_KB - 1, s, NEG_BIG)
            sc.append(s)
        scores.append(sc)
        values.append([v_ref[0, :, hs] for v_ref in (v0_ref, v1_ref, v2_ref)])
    o_ref[0] = _softmax_pv_heads(scores, values).astype(BF16)


def _band_prompt(proj3, bias):
    b, s, _ = proj3.shape
    qcol, kcol, vcol = 3, 4, 5

    def kv_spec(col, back):
        return pl.BlockSpec((1, BAND_QB, W_HEADS), lambda bi, qi: (bi, jnp.maximum(qi - back, 0), col))

    return pl.pallas_call(
        _band_prompt_kernel,
        grid=(b, s // BAND_QB),
        in_specs=[
            pl.BlockSpec((1, BAND_QB, W_HEADS), lambda bi, qi: (bi, qi, qcol)),
            kv_spec(kcol, 2), kv_spec(kcol, 1), kv_spec(kcol, 0),
            kv_spec(vcol, 2), kv_spec(vcol, 1), kv_spec(vcol, 0),
            pl.BlockSpec((N_HEADS, BAND_QB, BAND_KB * BAND_QB), lambda bi, qi: (0, 0, 0)),
        ],
        out_specs=pl.BlockSpec((1, BAND_QB, W_HEADS), lambda bi, qi: (bi, qi, 0)),
        out_shape=jax.ShapeDtypeStruct((b, s, W_HEADS), BF16),
        compiler_params=_params("arbitrary", "arbitrary"),
        name="band_prompt",
    )(proj3, proj3, proj3, proj3, proj3, proj3, proj3, bias)


def _band_step_kernel(q_ref, kn_ref, vn_ref, ck_ref, cv_ref, bias_ref, o_ref, *, lb):
    scores, values = [], []
    for h in range(N_HEADS):
        hs = _head_slice(h)
        q = q_ref[0, :, hs]
        rows = pl.ds(h, lb, stride=N_HEADS)
        scores.append([_dot_nt(q, ck_ref[0, rows, :].astype(BF16)) + bias_ref[h, :, :lb],
                       _dot_nt(q, kn_ref[0, :, hs]) + bias_ref[h, :, lb:]])
        values.append([cv_ref[0, rows, :].astype(BF16), vn_ref[0, :, hs]])
    o_ref[0] = _softmax_pv_heads(scores, values).astype(BF16)


def _band_step(proj3, cache_k, cache_v, bias):
    b, t, _ = proj3.shape
    lb = cache_k.shape[1] // N_HEADS
    cache_spec = pl.BlockSpec((1, lb * N_HEADS, HEAD_DIM), lambda bi: (bi, 0, 0))
    return pl.pallas_call(
        functools.partial(_band_step_kernel, lb=lb),
        grid=(b,),
        in_specs=[
            pl.BlockSpec((1, t, W_HEADS), lambda bi: (bi, 0, 3)),
            pl.BlockSpec((1, t, W_HEADS), lambda bi: (bi, 0, 4)),
            pl.BlockSpec((1, t, W_HEADS), lambda bi: (bi, 0, 5)),
            cache_spec, cache_spec,
            pl.BlockSpec((N_HEADS, t, lb + t), lambda bi: (0, 0, 0)),
        ],
        out_specs=pl.BlockSpec((1, t, W_HEADS), lambda bi: (bi, 0, 0)),
        out_shape=jax.ShapeDtypeStruct((b, t, W_HEADS), BF16),
        compiler_params=_params("arbitrary"),
        name="band_step",
    )(proj3, proj3, proj3, cache_k, cache_v, bias)


def _merge_kernel(osb_ref, obd_ref, wsb_ref, wbd_ref, ga_ref, gb_ref, o_ref):
    a = _dot(osb_ref[...], wsb_ref[...])
    b = _dot(obd_ref[...], wbd_ref[...])
    merged = _sigmoid(ga_ref[...].astype(F32)) * a + _sigmoid(gb_ref[...].astype(F32)) * b
    o_ref[...] = merged.astype(BF16)


def _merge(o_sb, o_band, w_sb_bf, w_band_bf, proj, tm):
    m = o_sb.shape[0]
    d = w_sb_bf.shape[1]
    tn = min(MATMUL_TN, d)
    ga_col = 6 * W_HEADS // tn
    gb_col = ga_col + d // tn
    return pl.pallas_call(
        _merge_kernel,
        grid=(m // tm, d // tn),
        in_specs=[
            pl.BlockSpec((tm, W_HEADS), lambda i, j: (i, 0)),
            pl.BlockSpec((tm, W_HEADS), lambda i, j: (i, 0)),
            pl.BlockSpec((W_HEADS, tn), lambda i, j: (0, j)),
            pl.BlockSpec((W_HEADS, tn), lambda i, j: (0, j)),
            pl.BlockSpec((tm, tn), lambda i, j: (i, ga_col + j)),
            pl.BlockSpec((tm, tn), lambda i, j: (i, gb_col + j)),
        ],
        out_specs=pl.BlockSpec((tm, tn), lambda i, j: (i, j)),
        out_shape=jax.ShapeDtypeStruct((m, d), BF16),
        compiler_params=_params("arbitrary", "arbitrary"),
        name="merge",
    )(o_sb, o_band, w_sb_bf, w_band_bf, proj, proj)


def _outproj_kernel(m_ref, w_ref, x_ref, gate_ref, o_ref, *, gb):
    acc = _dot(m_ref[...], w_ref[...])
    acc = acc.reshape(gb, ROW_GROUP, acc.shape[1])
    o_ref[...] = x_ref[...] + gate_ref[0] * acc


def _outproj(merged, w_out_bf, x3, mods, gb):
    ng, _, d = x3.shape
    tm = gb * ROW_GROUP
    tn = min(MATMUL_TN, d)
    return pl.pallas_call(
        functools.partial(_outproj_kernel, gb=gb),
        grid=(ng // gb, d // tn),
        in_specs=[
            pl.BlockSpec((tm, d), lambda i, j: (i, 0)),
            pl.BlockSpec((d, tn), lambda i, j: (0, j)),
            pl.BlockSpec((gb, ROW_GROUP, tn), lambda i, j: (i, 0, j)),
            pl.BlockSpec((1, gb, 1, tn), lambda i, j: (MOD_GATE_M, i, 0, j)),
        ],
        out_specs=pl.BlockSpec((gb, ROW_GROUP, tn), lambda i, j: (i, 0, j)),
        out_shape=jax.ShapeDtypeStruct(x3.shape, F32),
        compiler_params=_params("arbitrary", "arbitrary"),
        name="outproj",
    )(merged, w_out_bf, x3, mods)


GATHER_HALF = 128


def _tile_rows(d):
    assert d % V7X_LANES == 0
    return d // V7X_LANES


def _store_token_tiles(ref, first_token, x):
    n, d = x.shape
    s_per = _tile_rows(d)
    for c in range(s_per):
        ref[pl.ds(first_token * s_per + c, n, stride=s_per), :] = x[:, c * V7X_LANES:(c + 1) * V7X_LANES]


def _load_token_tiles(ref, n, d, first_token=0):
    s_per = _tile_rows(d)
    return jnp.concatenate([ref[pl.ds(first_token * s_per + c, n, stride=s_per), :] for c in range(s_per)], axis=1)


def _start_token_gather(table_hbm, idx_ref, first, buf, sem_of, n, s_per):
    for half in range(n // GATHER_HALF):
        def issue(i, carry, half=half):
            row = half * GATHER_HALF + i
            pltpu.make_async_copy(table_hbm.at[pl.ds(idx_ref[first + row] * s_per, s_per)],
                                  buf.at[pl.ds(row * s_per, s_per)], sem_of(half)).start()
            return carry
        lax.fori_loop(0, GATHER_HALF, issue, 0, unroll=8)


def _wait_token_gather(buf, sem_of, n, s_per):
    for half in range(n // GATHER_HALF):
        part = buf.at[pl.ds(half * GATHER_HALF * s_per, GATHER_HALF * s_per)]
        pltpu.make_async_copy(part, part, sem_of(half)).wait()


ROUTE_E1, ROUTE_E2, ROUTE_W1, ROUTE_W2 = 0, 1, 2, 3


def _router_kernel(xa_ref, sca_ref, sha_ref, xb_ref, scb_ref, shb_ref, g_ref, whi_ref, wlo_ref, b_ref,
                   h_ref, route_ref, hi_scr, lo_scr, *, gb, na):
    def fill(x_ref, sc_ref, sh_ref):
        def body(s, carry):
            h = _modulated_norm(x_ref[s], g_ref[...], sc_ref[0, s], sh_ref[0, s])
            hi = h.astype(BF16)
            first = pl.multiple_of(s * ROW_GROUP, ROW_GROUP)
            _store_token_tiles(h_ref, first, h)
            hi_scr[pl.ds(first, ROW_GROUP), :] = hi
            lo_scr[pl.ds(first, ROW_GROUP), :] = (h - hi.astype(F32)).astype(BF16)
            return carry
        lax.fori_loop(0, gb, body, 0)

    @pl.when(pl.program_id(0) < na)
    def _():
        fill(xa_ref, sca_ref, sha_ref)

    @pl.when(pl.program_id(0) >= na)
    def _():
        fill(xb_ref, scb_ref, shb_ref)

    hi = hi_scr[...]
    logits = _dot(hi, whi_ref[...]) + _dot(hi, wlo_ref[...]) + _dot(lo_scr[...], whi_ref[...]) + b_ref[...]
    lane = lax.broadcasted_iota(jnp.int32, logits.shape, 1)
    ninf = -jnp.inf

    lane_f = lane.astype(F32)

    def first_max(vals):
        mx = vals.max(axis=1, keepdims=True)
        idx = jnp.where(vals == mx, lane_f, float(V7X_LANES)).min(axis=1, keepdims=True)
        return mx, idx

    gl = jnp.where(lane < N_GROUPS, logits, ninf)
    gmax, gidx = first_max(gl)
    g_weight = 1.0 / jnp.exp(gl - gmax).sum(axis=1, keepdims=True)
    lo_lane = N_GROUPS + EXPERTS_PER_GROUP * gidx
    el = jnp.where(lane_f >= lo_lane, jnp.where(lane_f < lo_lane + EXPERTS_PER_GROUP, logits, ninf), ninf)
    m1, i1 = first_max(el)
    m2, i2 = first_max(jnp.where(lane_f == i1, ninf, el))
    e21 = jnp.exp(m2 - m1)
    p1 = 1.0 / (1.0 + e21)
    p2 = e21 / (1.0 + e21)
    route = jnp.where(lane == ROUTE_E1, i1 - N_GROUPS,
                      jnp.where(lane == ROUTE_E2, i2 - N_GROUPS,
                                jnp.where(lane == ROUTE_W1, g_weight * p1,
                                          jnp.where(lane == ROUTE_W2, g_weight * p2, 0.0))))
    route_ref[...] = route


def _router(xa3, mods_a, xb3, mods_b, g, w_hi, w_lo, bias, gb):
    na, nb = xa3.shape[0] // gb, xb3.shape[0] // gb
    d = xa3.shape[2]
    n_tokens = (na + nb) * gb * ROW_GROUP
    tm = gb * ROW_GROUP
    s_per = _tile_rows(d)

    def first(i):
        return jnp.minimum(i, na - 1)

    def second(i):
        return jnp.maximum(i - na, 0)

    def mod_spec(which, blk):
        return pl.BlockSpec((1, gb, 1, d), lambda i: (which, blk(i), 0, 0))

    w_spec = pl.BlockSpec((d, V7X_LANES), lambda i: (0, 0))
    return pl.pallas_call(
        functools.partial(_router_kernel, gb=gb, na=na),
        grid=(na + nb,),
        in_specs=[
            pl.BlockSpec((gb, ROW_GROUP, d), lambda i: (first(i), 0, 0)),
            mod_spec(MOD_SCALE_F, first), mod_spec(MOD_SHIFT_F, first),
            pl.BlockSpec((gb, ROW_GROUP, d), lambda i: (second(i), 0, 0)),
            mod_spec(MOD_SCALE_F, second), mod_spec(MOD_SHIFT_F, second),
            pl.BlockSpec((1, d), lambda i: (0, 0)),
            w_spec, w_spec,
            pl.BlockSpec((1, V7X_LANES), lambda i: (0, 0)),
        ],
        out_specs=[
            pl.BlockSpec((tm * s_per, V7X_LANES), lambda i: (i, 0)),
            pl.BlockSpec((tm, V7X_LANES), lambda i: (i, 0)),
        ],
        out_shape=[jax.ShapeDtypeStruct((n_tokens * s_per, V7X_LANES), F32),
                   jax.ShapeDtypeStruct((n_tokens, V7X_LANES), F32)],
        scratch_shapes=[pltpu.VMEM((tm, d), BF16), pltpu.VMEM((tm, d), BF16)],
        compiler_params=_params("arbitrary"),
        name="router",
    )(xa3, mods_a, mods_a, xb3, mods_b, mods_b, g, w_hi, w_lo, bias)


def _moe_kernel(te_ref, nv_ref, src_ref, h_hbm, wg_ref, wu_ref, wd_ref, o_ref, xbuf, sems, wg_scr, wu_scr, wd_scr):
    t = pl.program_id(0)
    n_valid = nv_ref[0]
    d = wg_scr.shape[0]
    s_per = _tile_rows(d)

    def start(tile, slot):
        _start_token_gather(h_hbm, src_ref, tile * MOE_TILE, xbuf.at[slot], lambda half: sems.at[slot, half],
                            MOE_TILE, s_per)

    @pl.when(t == 0)
    def _():
        start(0, 0)

    @pl.when(t + 1 < n_valid)
    def _():
        start(t + 1, (t + 1) % 2)

    changed = te_ref[t] != te_ref[jnp.maximum(t - 1, 0)]

    @pl.when((t == 0) | changed)
    def _():
        wg_scr[...] = wg_ref[0].astype(BF16)
        wu_scr[...] = wu_ref[0].astype(BF16)
        wd_scr[...] = wd_ref[0].astype(BF16)

    @pl.when(t < n_valid)
    def _():
        slot = t % 2
        _wait_token_gather(xbuf.at[slot], lambda half: sems.at[slot, half], MOE_TILE, s_per)

        x = _load_token_tiles(xbuf.at[slot], MOE_TILE, d).astype(BF16)
        g = _dot(x, wg_scr[...])
        u = _dot(x, wu_scr[...])
        hidden = (g * _sigmoid(g)) * u
        _store_token_tiles(o_ref, 0, _dot(hidden.astype(BF16), wd_scr[...]))

    @pl.when(t >= n_valid)
    def _():
        o_ref[...] = jnp.zeros(o_ref.shape, F32)


def _moe(tile_expert, n_valid, src_tok, h_tiles, w_gate, w_up, w_down):
    _, d, f = w_gate.shape
    s_per = _tile_rows(d)
    n_tiles = src_tok.shape[0] // MOE_TILE
    halves = MOE_TILE // GATHER_HALF
    grid_spec = pltpu.PrefetchScalarGridSpec(
        num_scalar_prefetch=3,
        grid=(n_tiles,),
        in_specs=[
            pl.BlockSpec(memory_space=pl.ANY),
            pl.BlockSpec((1, d, f), lambda t, te, nv, src: (te[t], 0, 0)),
            pl.BlockSpec((1, d, f), lambda t, te, nv, src: (te[t], 0, 0)),
            pl.BlockSpec((1, f, d), lambda t, te, nv, src: (te[t], 0, 0)),
        ],
        out_specs=pl.BlockSpec((MOE_TILE * s_per, V7X_LANES), lambda t, te, nv, src: (t, 0)),
        scratch_shapes=[pltpu.VMEM((2, MOE_TILE * s_per, V7X_LANES), F32), pltpu.SemaphoreType.DMA((2, halves)),
                        pltpu.VMEM((d, f), BF16), pltpu.VMEM((d, f), BF16), pltpu.VMEM((f, d), BF16)],
    )
    return pl.pallas_call(
        _moe_kernel,
        grid_spec=grid_spec,
        out_shape=jax.ShapeDtypeStruct((n_tiles * MOE_TILE * s_per, V7X_LANES), F32),
        compiler_params=_params("arbitrary"),
        name="moe",
    )(tile_expert, n_valid, src_tok, h_tiles, w_gate, w_up, w_down)


def _dispatch(route):
    n = route.shape[0]
    e = jnp.concatenate([route[:, ROUTE_E1], route[:, ROUTE_E2]]).astype(jnp.int32)
    tok = jnp.concatenate([jnp.arange(n, dtype=jnp.int32)] * 2)
    onehot = (e[:, None] == jnp.arange(N_EXPERTS, dtype=jnp.int32)[None, :]).astype(jnp.int32)
    before = jnp.cumsum(onehot, axis=0) - onehot
    rank = jnp.sum(before * onehot, axis=1)
    counts = jnp.sum(onehot, axis=0)
    padded = ((counts + MOE_TILE - 1) // MOE_TILE) * MOE_TILE
    ends = jnp.cumsum(padded)
    pos = (ends - padded)[e] + rank
    n_tiles = -(-(2 * n + N_EXPERTS * (MOE_TILE - 1)) // MOE_TILE)
    tile_start = jnp.arange(n_tiles, dtype=jnp.int32) * MOE_TILE
    n_valid = (ends[-1] // MOE_TILE).astype(jnp.int32)
    tile_expert = jnp.sum((ends[None, :] <= tile_start[:, None]).astype(jnp.int32), axis=1)
    last_expert = tile_expert[jnp.maximum(n_valid - 1, 0)]
    tile_expert = jnp.where(tile_start < ends[-1], tile_expert, last_expert)
    src_tok = jnp.zeros((n_tiles * MOE_TILE,), jnp.int32).at[pos].set(tok)
    return pos[:n], pos[n:], src_tok, tile_expert, n_valid.reshape(1)


def _final_kernel(p1_ref, p2_ref, x_ref, y_hbm, route_ref, gate_ref, o_ref, ybuf, sems, *, first_token):
    i = pl.program_id(0)
    gb, rg, d = x_ref.shape
    tm = gb * rg
    s_per = _tile_rows(d)

    def start(step, slot):
        for which, pos_ref in enumerate((p1_ref, p2_ref)):
            _start_token_gather(y_hbm, pos_ref, first_token + step * tm, ybuf.at[slot, which],
                                lambda half, which=which: sems.at[slot, which, half], tm, s_per)

    @pl.when(i == 0)
    def _():
        start(0, 0)

    @pl.when(i + 1 < pl.num_programs(0))
    def _():
        start(i + 1, (i + 1) % 2)

    slot = i % 2
    for which in range(2):
        _wait_token_gather(ybuf.at[slot, which], lambda half, which=which: sems.at[slot, which, half], tm, s_per)
    r = route_ref[...]
    y = (r[:, ROUTE_W1:ROUTE_W1 + 1] * _load_token_tiles(ybuf.at[slot, 0], tm, d)
         + r[:, ROUTE_W2:ROUTE_W2 + 1] * _load_token_tiles(ybuf.at[slot, 1], tm, d))
    o_ref[...] = x_ref[...] + gate_ref[0] * y.reshape(gb, rg, d)


def _final(x3, y_tiles, pos1, pos2, route, first_token, mods, gb):
    ng, _, d = x3.shape
    tm = gb * ROW_GROUP
    s_per = _tile_rows(d)
    assert first_token % tm == 0
    b0 = first_token // tm
    blk = pl.BlockSpec((gb, ROW_GROUP, d), lambda i, p1, p2: (i, 0, 0))
    grid_spec = pltpu.PrefetchScalarGridSpec(
        num_scalar_prefetch=2,
        grid=(ng // gb,),
        in_specs=[
            blk,
            pl.BlockSpec(memory_space=pl.ANY),
            pl.BlockSpec((tm, V7X_LANES), lambda i, p1, p2: (b0 + i, 0)),
            pl.BlockSpec((1, gb, 1, d), lambda i, p1, p2: (MOD_GATE_F, i, 0, 0)),
        ],
        out_specs=blk,
        scratch_shapes=[pltpu.VMEM((2, 2, tm * s_per, V7X_LANES), F32),
                        pltpu.SemaphoreType.DMA((2, 2, tm // GATHER_HALF))],
    )
    return pl.pallas_call(
        functools.partial(_final_kernel, first_token=first_token),
        grid_spec=grid_spec,
        out_shape=jax.ShapeDtypeStruct(x3.shape, F32),
        compiler_params=_params("arbitrary"),
        name="final",
    )(pos1, pos2, x3, y_tiles, route, mods)


def _layer(xp, xs, csk, csv, cbk, cbv, c_prompt, c_sample, norm_mix, norm_ffn, w_ada, b_ada, w_in, q_norm,
           k_norm, rel_table, w_proj_sb, w_proj_band, w_out, w_rg, b_rg, w_re, b_re, w_gate, w_up, w_down):
    bp, sp_len, d = xp.shape
    bs, ts, _ = xs.shape
    d_in = w_in.shape[1]
    assert sp_len % BAND_QB == 0 and ts == ROW_GROUP and sp_len % ROW_GROUP == 0

    n_c = bp + bs
    c_pad = jnp.concatenate([c_prompt, c_sample, jnp.zeros((-n_c % 8, d), F32)], axis=0)
    mod = _ada(c_pad, w_ada, b_ada)
    gp = sp_len // ROW_GROUP
    mod_p = jnp.repeat(mod[:, :bp], gp, axis=1)[:, :, None, :]
    mod_s = mod[:, bp:n_c][:, :, None, :]

    w_in_bf = w_in.astype(BF16)
    w_sb_bf = w_proj_sb.astype(BF16)
    w_band_bf = w_proj_band.astype(BF16)
    w_out_bf = w_out.astype(BF16)
    qn = q_norm.reshape(1, W_HEADS)
    kn = k_norm.reshape(1, W_HEADS)
    g_mix = norm_mix.reshape(1, d)
    g_ffn = norm_ffn.reshape(1, d)
    lb = cbk.shape[1]
    assert lb % CHUNK == 0 and ts == CHUNK
    bias_blocks = _bias_blocks(rel_table)
    bias = _band_bias(bias_blocks, rel_table)
    bias_step = _band_step_bias(bias_blocks, rel_table, lb)
    u = _lower_twice(SB_T)

    w_router = jnp.zeros((d, V7X_LANES), F32).at[:, :N_GROUPS].set(w_rg)
    w_router = w_router.at[:, N_GROUPS:N_GROUPS + N_EXPERTS].set(w_re)
    w_router_hi = w_router.astype(BF16)
    w_router_lo = (w_router - w_router_hi.astype(F32)).astype(BF16)
    b_router = jnp.zeros((1, V7X_LANES), F32).at[0, :N_GROUPS].set(b_rg)
    b_router = b_router.at[0, N_GROUPS:N_GROUPS + N_EXPERTS].set(b_re.reshape(-1))

    xp3 = xp.reshape(bp * gp, ROW_GROUP, d)
    xs3 = xs.reshape(bs, ROW_GROUP, d)
    gb_p = min(ROWWISE_GROUPS, bp * gp)
    gb_s = min(ROWWISE_GROUPS, bs)

    def mixer(x3, mods, gb, attend):
        proj, ka, va, kb, vb = _proj(x3, g_mix, mods, w_in_bf, qn, kn, gb)
        o_sb, o_band = attend(proj)
        merged = _merge(o_sb, o_band, w_sb_bf, w_band_bf, proj, gb * ROW_GROUP)
        x1 = _outproj(merged, w_out_bf, x3, mods, gb)
        return x1, ka, va, kb, vb

    def attend_prompt(proj):
        p3 = proj.reshape(bp, sp_len, d_in)
        return (_sb_prompt(p3, u).reshape(bp * sp_len, W_HEADS),
                _band_prompt(p3, bias).reshape(bp * sp_len, W_HEADS))

    def attend_sample(proj):
        p3 = proj.reshape(bs, ts, d_in)
        def rows(cache):
            return cache.reshape(bs, -1, HEAD_DIM)

        o_sb = _sb_step(p3, rows(csk), rows(csv), u)
        o_band = _band_step(p3, rows(cbk), rows(cbv), bias_step)
        return o_sb.reshape(bs * ts, W_HEADS), o_band.reshape(bs * ts, W_HEADS)

    x1p, kap, vap, kbp, vbp = mixer(xp3, mod_p, min(MATMUL_GROUPS, bp * gp), attend_prompt)
    x1s, kas, vas, kbs, vbs = mixer(xs3, mod_s, min(MATMUL_GROUPS, bs), attend_sample)

    gb_r = min(gb_p, gb_s)
    h2, route = _router(x1p, mod_p, x1s, mod_s, g_ffn, w_router_hi, w_router_lo, b_router, gb_r)
    pos1, pos2, src_tok, tile_expert, n_valid = _dispatch(route)
    y_sorted = _moe(tile_expert, n_valid, src_tok, h2, w_gate, w_up, w_down)
    gb_f = min(FINAL_GROUPS, bs)
    yp = _final(x1p, y_sorted, pos1, pos2, route, 0, mod_p, gb_f).reshape(bp, sp_len, d)
    ys = _final(x1s, y_sorted, pos1, pos2, route, bp * sp_len, mod_s, gb_f).reshape(bs, ts, d)

    def heads(a, b):
        return a.reshape(b, -1, N_HEADS, HEAD_DIM)

    n_band = min(BAND_PAST, sp_len)
    return (yp, ys, heads(kap, bp), heads(vap, bp), heads(kbp, bp)[:, -n_band:], heads(vbp, bp)[:, -n_band:],
            heads(kas, bs), heads(vas, bs), heads(kbs, bs), heads(vbs, bs))


def kernel(x_prompt, x_sample, cache_sb_k, cache_sb_v, cache_band_k, cache_band_v, c_prompt, c_sample, norm_mix, norm_ffn, w_ada, b_ada, w_in, q_norm_band, k_norm_band, rel_bias_band, w_proj_sb, w_proj_band, w_out, w_router_group, b_router_group, w_router_expert, b_router_expert, w_gate, w_up, w_down):
    depth = w_in.shape[0]
    xp, xs = x_prompt, x_sample
    outs = [[] for _ in range(8)]
    for l in range(depth):
        res = _layer(xp, xs, cache_sb_k[l], cache_sb_v[l], cache_band_k[l], cache_band_v[l], c_prompt, c_sample,
                     norm_mix[l], norm_ffn[l], w_ada[l], b_ada[l], w_in[l], q_norm_band[l], k_norm_band[l],
                     rel_bias_band[l], w_proj_sb[l], w_proj_band[l], w_out[l], w_router_group[l],
                     b_router_group[l], w_router_expert[l], b_router_expert[l], w_gate[l], w_up[l], w_down[l])
        xp, xs = res[0], res[1]
        for acc, r in zip(outs, res[2:]):
            acc.append(r)
    return (xp, xs) + tuple(jnp.stack(o, axis=0) for o in outs)
```

```python
import functools

import jax
import jax.numpy as jnp
from jax import lax
from jax.experimental import pallas as pl
from jax.experimental.pallas import tpu as pltpu

F32 = jnp.float32
BF16 = jnp.bfloat16

EPS = 1e-6
HEAD_DIM = 128
N_HEADS = 8
W_HEADS = N_HEADS * HEAD_DIM
CHUNK = 64
BAND_LEFT_CHUNKS = 8
BAND_PAST = BAND_LEFT_CHUNKS * CHUNK
REL_CLIP = 128
N_GROUPS = 4
EXPERTS_PER_GROUP = 8
N_EXPERTS = N_GROUPS * EXPERTS_PER_GROUP
QK_SCALE = HEAD_DIM ** -0.5
NEG_BIG = -1e30

V7X_LANES = 128
V7X_VMEM_LIMIT = 56 * 1024 * 1024
ROW_GROUP = 64
MOE_TILE = 256
MATMUL_GROUPS = 16
MATMUL_TN = 1024
PROJ_TN = 1024
ROWWISE_GROUPS = 8
FINAL_GROUPS = 2


def _params(*sem):
    return pltpu.CompilerParams(dimension_semantics=sem, vmem_limit_bytes=V7X_VMEM_LIMIT)


def _sigmoid(x):
    return 1.0 / (1.0 + jnp.exp(-x))


def _dot(a, b):
    return jnp.dot(a, b, preferred_element_type=F32)


def _dot_nt(a, b):
    return lax.dot_general(a, b, (((1,), (1,)), ((), ())), preferred_element_type=F32)


def _ada_kernel(c_ref, w_ref, b_ref, o_ref):
    c = c_ref[...]
    a = (c * _sigmoid(c)).astype(BF16)
    o_ref[0] = _dot(a, w_ref[...].astype(BF16)) + b_ref[...]


def _ada(c_pad, w_ada, b_ada):
    r, d = c_pad.shape
    tn = min(1024, d)
    per = d // tn
    return pl.pallas_call(
        _ada_kernel,
        grid=(6 * per,),
        in_specs=[
            pl.BlockSpec((r, d), lambda j: (0, 0)),
            pl.BlockSpec((d, tn), lambda j: (0, j)),
            pl.BlockSpec((1, tn), lambda j: (0, j)),
        ],
        out_specs=pl.BlockSpec((1, r, tn), lambda j: (j // per, 0, j % per)),
        out_shape=jax.ShapeDtypeStruct((6, r, d), F32),
        compiler_params=_params("arbitrary"),
        name="ada",
    )(c_pad, w_ada, b_ada.reshape(1, 6 * d))


def _modulated_norm(x, g, scale, shift):
    ms = jnp.mean(x * x, axis=-1, keepdims=True)
    return (x * lax.rsqrt(ms + EPS) * g) * (1.0 + scale) + shift


def _head_norm(a, gain_ref):
    outs = []
    for hh in range(a.shape[1] // HEAD_DIM):
        blk = a[:, hh * HEAD_DIM:(hh + 1) * HEAD_DIM]
        ms = jnp.mean(blk * blk, axis=-1, keepdims=True)
        outs.append(blk * lax.rsqrt(ms + EPS) * gain_ref[:, hh * HEAD_DIM:(hh + 1) * HEAD_DIM])
    return jnp.concatenate(outs, axis=1)


def _proj_kernel(x_ref, g_ref, sc_ref, sh_ref, w_ref, qn_ref, kn_ref,
                 proj_ref, ka_ref, va_ref, kb_ref, vb_ref, h_scr, *, gb, per):
    j = pl.program_id(1)

    @pl.when(j == 0)
    def _():
        def body(s, carry):
            h = _modulated_norm(x_ref[s], g_ref[...], sc_ref[0, s], sh_ref[0, s])
            h_scr[pl.ds(pl.multiple_of(s * ROW_GROUP, ROW_GROUP), ROW_GROUP), :] = h.astype(BF16)
            return carry
        lax.fori_loop(0, gb, body, 0)

    acc = _dot(h_scr[...], w_ref[...])
    sec = j // per

    @pl.when(sec == 0)
    def _():
        proj_ref[...] = (acc * QK_SCALE).astype(BF16)

    @pl.when(sec == 1)
    def _():
        proj_ref[...] = acc.astype(BF16)
        ka_ref[...] = acc

    @pl.when(sec == 2)
    def _():
        proj_ref[...] = acc.astype(BF16)
        va_ref[...] = acc

    @pl.when(sec == 3)
    def _():
        proj_ref[...] = (_head_norm(acc, qn_ref) * QK_SCALE).astype(BF16)

    @pl.when(sec == 4)
    def _():
        n = _head_norm(acc, kn_ref)
        proj_ref[...] = n.astype(BF16)
        kb_ref[...] = n

    @pl.when(sec == 5)
    def _():
        proj_ref[...] = acc.astype(BF16)
        vb_ref[...] = acc

    @pl.when(sec >= 6)
    def _():
        proj_ref[...] = acc.astype(BF16)


MOD_SHIFT_M, MOD_SCALE_M, MOD_GATE_M, MOD_SHIFT_F, MOD_SCALE_F, MOD_GATE_F = range(6)


def _proj(x3, g, mods, w_in_bf, qn, kn, gb):
    ng, _, d = x3.shape
    d_in = w_in_bf.shape[1]
    m = ng * ROW_GROUP
    tm = gb * ROW_GROUP
    tn = PROJ_TN
    per = W_HEADS // tn

    def sect(s):
        return pl.BlockSpec((tm, tn), lambda i, j: (i, jnp.clip(j - s * per, 0, per - 1)),
                            pipeline_mode=pl.Buffered(1))

    def mod_spec(which):
        return pl.BlockSpec((1, gb, 1, d), lambda i, j: (which, i, 0, 0))

    kv_shape = jax.ShapeDtypeStruct((m, W_HEADS), F32)
    return pl.pallas_call(
        functools.partial(_proj_kernel, gb=gb, per=per),
        grid=(ng // gb, d_in // tn),
        in_specs=[
            pl.BlockSpec((gb, ROW_GROUP, d), lambda i, j: (i, 0, 0)),
            pl.BlockSpec((1, d), lambda i, j: (0, 0)),
            mod_spec(MOD_SCALE_M), mod_spec(MOD_SHIFT_M),
            pl.BlockSpec((d, tn), lambda i, j: (0, j)),
            pl.BlockSpec((1, tn), lambda i, j: (0, jnp.clip(j - 3 * per, 0, per - 1))),
            pl.BlockSpec((1, tn), lambda i, j: (0, jnp.clip(j - 4 * per, 0, per - 1))),
        ],
        out_specs=[
            pl.BlockSpec((tm, tn), lambda i, j: (i, j)),
            sect(1), sect(2), sect(4), sect(5),
        ],
        out_shape=[jax.ShapeDtypeStruct((m, d_in), BF16), kv_shape, kv_shape, kv_shape, kv_shape],
        scratch_shapes=[pltpu.VMEM((tm, d), BF16)],
        compiler_params=_params("arbitrary", "arbitrary"),
        name="proj",
    )(x3, g, mods, mods, w_in_bf, qn, kn)


def _lower_twice(n):
    row = lax.broadcasted_iota(jnp.int32, (2 * n, n), 0)
    col = lax.broadcasted_iota(jnp.int32, (2 * n, n), 1)
    row = jnp.where(row >= n, row - n, row)
    return jnp.where(row >= col, 1.0, 0.0).astype(BF16)


def _sb_scores(load_qk, n_heads):
    return jnp.concatenate([_dot_nt(*load_qk(h)) for h in range(n_heads)], axis=1)


def _sb_apply(z_all, load_v, n_heads, u, carries, diagonal):
    tq = z_all.shape[0]
    tk = z_all.shape[1] // n_heads
    zs = [z_all[:, h * tk:(h + 1) * tk] for h in range(n_heads)]
    if diagonal:
        row = lax.broadcasted_iota(jnp.int32, (tq, tk), 0)
        col = lax.broadcasted_iota(jnp.int32, (tq, tk), 1)
        valid = col < row
    tails = []
    for z in zs:
        sp = jnp.maximum(z, 0.0) + jnp.log(1.0 + jnp.exp(-jnp.abs(z)))
        if diagonal:
            sp = jnp.where(valid, sp, 0.0)
        hi = sp.astype(BF16)
        lo = (sp - hi.astype(F32)).astype(BF16)
        tails.append(_dot(jnp.concatenate([hi, lo], axis=1), u))
    pvs, new = [], []
    for h, (z, tail) in enumerate(zip(zs, tails)):
        carry = jnp.zeros((tq, 1), F32) if carries is None else carries[:, h * HEAD_DIM:h * HEAD_DIM + 1]
        loga = z - tail - carry
        if diagonal:
            loga = jnp.where(valid, loga, NEG_BIG)
        pvs.append(_dot(jnp.exp(loga).astype(BF16), load_v(h)))
        new.append(jnp.broadcast_to(carry + tail[:, 0:1], (tq, HEAD_DIM)))
    return jnp.concatenate(pvs, axis=1), jnp.concatenate(new, axis=1)


SB_T = 256
SB_HEADS_PER_STEP = 8
SB_CACHE_ROWS = 2048


def _head_slice(h):
    return slice(h * HEAD_DIM, (h + 1) * HEAD_DIM)


def _sb_prompt_kernel(q_ref, k_ref, v_ref, u_ref, o_ref, acc_scr, carry_scr):
    qi = pl.program_id(2)
    u = u_ref[...]
    nh = SB_HEADS_PER_STEP

    def block(blk, carries, diagonal):
        r = pl.multiple_of(blk * SB_T, SB_T)
        z_all = _sb_scores(lambda h: (q_ref[0, :, _head_slice(h)], k_ref[0, pl.ds(r, SB_T), _head_slice(h)]), nh)
        return _sb_apply(z_all, lambda h: v_ref[0, pl.ds(r, SB_T), _head_slice(h)], nh, u, carries, diagonal)

    acc_scr[...], carry_scr[...] = block(qi, None, True)

    def body(step, c):
        pv, carries = block(qi - 1 - step, carry_scr[...], False)
        acc_scr[...] += pv
        carry_scr[...] = carries
        return c

    lax.fori_loop(0, qi, body, 0)
    o_ref[0] = acc_scr[...].astype(BF16)


def _sb_prompt(proj3, u):
    b, s, _ = proj3.shape
    hp = SB_HEADS_PER_STEP
    w = hp * HEAD_DIM
    kcol, vcol = W_HEADS // w, 2 * W_HEADS // w
    return pl.pallas_call(
        _sb_prompt_kernel,
        grid=(b, N_HEADS // hp, s // SB_T),
        in_specs=[
            pl.BlockSpec((1, SB_T, w), lambda bi, h, qi: (bi, qi, h)),
            pl.BlockSpec((1, s, w), lambda bi, h, qi: (bi, 0, kcol + h)),
            pl.BlockSpec((1, s, w), lambda bi, h, qi: (bi, 0, vcol + h)),
            pl.BlockSpec((2 * SB_T, SB_T), lambda bi, h, qi: (0, 0)),
        ],
        out_specs=pl.BlockSpec((1, SB_T, w), lambda bi, h, qi: (bi, qi, h)),
        out_shape=jax.ShapeDtypeStruct((b, s, W_HEADS), BF16),
        scratch_shapes=[pltpu.VMEM((SB_T, w), F32), pltpu.VMEM((SB_T, w), F32)],
        compiler_params=_params("arbitrary", "arbitrary", "arbitrary"),
        name="sb_prompt",
    )(proj3, proj3, proj3, u)


def _sb_step_kernel(q_ref, kn_ref, vn_ref, ck_ref, cv_ref, u_ref, o_ref, acc_scr, carry_scr, *, tkb, t):
    j = pl.program_id(1)
    u = u_ref[...]

    @pl.when(j == 0)
    def _():
        z_new = _sb_scores(lambda h: (q_ref[0, :, _head_slice(h)], kn_ref[0, :, _head_slice(h)]), N_HEADS)
        acc_scr[...], carry_scr[...] = _sb_apply(z_new, lambda h: vn_ref[0, :, _head_slice(h)], N_HEADS,
                                                 _lower_twice(t), None, True)

    def rows(sub, h):
        first = pl.multiple_of((tkb - (sub + 1) * SB_T) * N_HEADS, SB_T * N_HEADS)
        return pl.ds(first + h, SB_T, stride=N_HEADS)

    def body(sub, c):
        z_all = _sb_scores(lambda h: (q_ref[0, :, _head_slice(h)], ck_ref[0, rows(sub, h), :].astype(BF16)), N_HEADS)
        pv, carries = _sb_apply(z_all, lambda h: cv_ref[0, rows(sub, h), :].astype(BF16), N_HEADS, u,
                                carry_scr[...], False)
        acc_scr[...] += pv
        carry_scr[...] = carries
        return c

    lax.fori_loop(0, tkb // SB_T, body, 0)

    @pl.when(j == pl.num_programs(1) - 1)
    def _():
        o_ref[0] = acc_scr[...].astype(BF16)


def _sb_step(proj3, cache_k, cache_v, u):
    b, t, _ = proj3.shape
    p = cache_k.shape[1] // N_HEADS
    tkb = min(SB_CACHE_ROWS, p)
    nj = p // tkb
    cache_spec = pl.BlockSpec((1, tkb * N_HEADS, HEAD_DIM), lambda bi, j: (bi, nj - 1 - j, 0))
    return pl.pallas_call(
        functools.partial(_sb_step_kernel, tkb=tkb, t=t),
        grid=(b, nj),
        in_specs=[
            pl.BlockSpec((1, t, W_HEADS), lambda bi, j: (bi, 0, 0)),
            pl.BlockSpec((1, t, W_HEADS), lambda bi, j: (bi, 0, 1)),
            pl.BlockSpec((1, t, W_HEADS), lambda bi, j: (bi, 0, 2)),
            cache_spec, cache_spec,
            pl.BlockSpec((2 * SB_T, SB_T), lambda bi, j: (0, 0)),
        ],
        out_specs=pl.BlockSpec((1, t, W_HEADS), lambda bi, j: (bi, 0, 0)),
        out_shape=jax.ShapeDtypeStruct((b, t, W_HEADS), BF16),
        scratch_shapes=[pltpu.VMEM((t, W_HEADS), F32), pltpu.VMEM((t, W_HEADS), F32)],
        compiler_params=_params("arbitrary", "arbitrary"),
        name="sb_step",
    )(proj3, proj3, proj3, cache_k, cache_v, u)


BAND_QB = 4 * CHUNK
BAND_KB = 3


BIAS_NEAR = REL_CLIP // CHUNK + 1


def _bias_blocks_kernel(table_ref, o_ref):
    h = pl.program_id(0)
    i = lax.broadcasted_iota(jnp.int32, (CHUNK, CHUNK), 0)
    j = lax.broadcasted_iota(jnp.int32, (CHUNK, CHUNK), 1)
    for d in range(BIAS_NEAR):
        idx = jnp.clip(i - j + d * CHUNK, -REL_CLIP, REL_CLIP) + REL_CLIP

        def body(r, acc, idx=idx):
            return jnp.where(idx == r, table_ref[h, r], acc)

        lo = max(d * CHUNK - (CHUNK - 1), -REL_CLIP) + REL_CLIP
        hi = min(d * CHUNK + (CHUNK - 1), REL_CLIP) + REL_CLIP
        o_ref[0, d] = lax.fori_loop(lo, hi + 1, body, jnp.zeros((CHUNK, CHUNK), F32))


def _bias_blocks(table):
    nh = table.shape[0]
    return pl.pallas_call(
        _bias_blocks_kernel,
        grid=(nh,),
        in_specs=[pl.BlockSpec(memory_space=pltpu.SMEM)],
        out_specs=pl.BlockSpec((1, BIAS_NEAR, CHUNK, CHUNK), lambda h: (h, 0, 0, 0)),
        out_shape=jax.ShapeDtypeStruct((nh, BIAS_NEAR, CHUNK, CHUNK), F32),
        compiler_params=_params("arbitrary"),
        name="bias_blocks",
    )(table.astype(F32))


def _chunk_bias(blocks, far, dist):
    if dist < 0 or dist > BAND_LEFT_CHUNKS:
        return jnp.full(far.shape, NEG_BIG, F32)
    return blocks[:, dist] if dist < BIAS_NEAR else far


def _band_bias(blocks, table):
    far = jnp.broadcast_to(table[:, -1].astype(F32)[:, None, None], (table.shape[0], CHUNK, CHUNK))
    qc, kc = BAND_QB // CHUNK, BAND_KB * BAND_QB // CHUNK
    rows = [jnp.concatenate([_chunk_bias(blocks, far, r - c + BAND_LEFT_CHUNKS) for c in range(kc)], axis=2)
            for r in range(qc)]
    return jnp.concatenate(rows, axis=1)


def _band_step_bias(blocks, table, lb):
    far = jnp.broadcast_to(table[:, -1].astype(F32)[:, None, None], (table.shape[0], CHUNK, CHUNK))
    nc = lb // CHUNK
    return jnp.concatenate([blocks[:, nc - c] if nc - c < BIAS_NEAR else far for c in range(nc + 1)], axis=2)


def _softmax_pv_heads(scores, values):
    ps, dens = [], []
    for sc in scores:
        m = sc[0].max(axis=1, keepdims=True)
        for s in sc[1:]:
            m = jnp.maximum(m, s.max(axis=1, keepdims=True))
        p = [jnp.exp(s - m) for s in sc]
        den = p[0].sum(axis=1, keepdims=True)
        for pc in p[1:]:
            den = den + pc.sum(axis=1, keepdims=True)
        ps.append(p)
        dens.append(den)
    outs = []
    for p, den, vals in zip(ps, dens, values):
        num = _dot(p[0].astype(BF16), vals[0])
        for pc, v in zip(p[1:], vals[1:]):
            num = num + _dot(pc.astype(BF16), v)
        outs.append(num / den)
    return jnp.concatenate(outs, axis=1)


def _band_prompt_kernel(q_ref, k0_ref, k1_ref, k2_ref, v0_ref, v1_ref, v2_ref, bias_ref, o_ref):
    qi = pl.program_id(1)
    scores, values = [], []
    for h in range(N_HEADS):
        hs = _head_slice(h)
        q = q_ref[0, :, hs]
        sc = []
        for c, k_ref in enumerate((k0_ref, k1_ref, k2_ref)):
            s = _dot_nt(q, k_ref[0, :, hs]) + bias_ref[h, :, c * BAND_QB:(c + 1) * BAND_QB]
            if c < BAND_KB - 1:
                s = jnp.where(qi + c >= BAND_KB - 1, s, NEG_BIG)
            sc.append(s)
        scores.append(sc)
        values.append([v_ref[0, :, hs] for v_ref in (v0_ref, v1_ref, v2_ref)])
    o_ref[0] = _softmax_pv_heads(scores, values).astype(BF16)


def _band_prompt(proj3, bias):
    b, s, _ = proj3.shape
    qcol, kcol, vcol = 3, 4, 5

    def kv_spec(col, back):
        return pl.BlockSpec((1, BAND_QB, W_HEADS), lambda bi, qi: (bi, jnp.maximum(qi - back, 0), col))

    return pl.pallas_call(
        _band_prompt_kernel,
        grid=(b, s // BAND_QB),
        in_specs=[
            pl.BlockSpec((1, BAND_QB, W_HEADS), lambda bi, qi: (bi, qi, qcol)),
            kv_spec(kcol, 2), kv_spec(kcol, 1), kv_spec(kcol, 0),
            kv_spec(vcol, 2), kv_spec(vcol, 1), kv_spec(vcol, 0),
            pl.BlockSpec((N_HEADS, BAND_QB, BAND_KB * BAND_QB), lambda bi, qi: (0, 0, 0)),
        ],
        out_specs=pl.BlockSpec((1, BAND_QB, W_HEADS), lambda bi, qi: (bi, qi, 0)),
        out_shape=jax.ShapeDtypeStruct((b, s, W_HEADS), BF16),
        compiler_params=_params("arbitrary", "arbitrary"),
        name="band_prompt",
    )(proj3, proj3, proj3, proj3, proj3, proj3, proj3, bias)


def _band_step_kernel(q_ref, kn_ref, vn_ref, ck_ref, cv_ref, bias_ref, o_ref, *, lb):
    scores, values = [], []
    for h in range(N_HEADS):
        hs = _head_slice(h)
        q = q_ref[0, :, hs]
        rows = pl.ds(h, lb, stride=N_HEADS)
        scores.append([_dot_nt(q, ck_ref[0, rows, :].astype(BF16)) + bias_ref[h, :, :lb],
                       _dot_nt(q, kn_ref[0, :, hs]) + bias_ref[h, :, lb:]])
        values.append([cv_ref[0, rows, :].astype(BF16), vn_ref[0, :, hs]])
    o_ref[0] = _softmax_pv_heads(scores, values).astype(BF16)


def _band_step(proj3, cache_k, cache_v, bias):
    b, t, _ = proj3.shape
    lb = cache_k.shape[1] // N_HEADS
    cache_spec = pl.BlockSpec((1, lb * N_HEADS, HEAD_DIM), lambda bi: (bi, 0, 0))
    return pl.pallas_call(
        functools.partial(_band_step_kernel, lb=lb),
        grid=(b,),
        in_specs=[
            pl.BlockSpec((1, t, W_HEADS), lambda bi: (bi, 0, 3)),
            pl.BlockSpec((1, t, W_HEADS), lambda bi: (bi, 0, 4)),
            pl.BlockSpec((1, t, W_HEADS), lambda bi: (bi, 0, 5)),
            cache_spec, cache_spec,
            pl.BlockSpec((N_HEADS, t, lb + t), lambda bi: (0, 0, 0)),
        ],
        out_specs=pl.BlockSpec((1, t, W_HEADS), lambda bi: (bi, 0, 0)),
        out_shape=jax.ShapeDtypeStruct((b, t, W_HEADS), BF16),
        compiler_params=_params("arbitrary"),
        name="band_step",
    )(proj3, proj3, proj3, cache_k, cache_v, bias)


def _merge_kernel(osb_ref, obd_ref, wsb_ref, wbd_ref, ga_ref, gb_ref, o_ref):
    a = _dot(osb_ref[...], wsb_ref[...])
    b = _dot(obd_ref[...], wbd_ref[...])
    merged = _sigmoid(ga_ref[...].astype(F32)) * a + _sigmoid(gb_ref[...].astype(F32)) * b
    o_ref[...] = merged.astype(BF16)


def _merge(o_sb, o_band, w_sb_bf, w_band_bf, proj, tm):
    m = o_sb.shape[0]
    d = w_sb_bf.shape[1]
    tn = min(MATMUL_TN, d)
    ga_col = 6 * W_HEADS // tn
    gb_col = ga_col + d // tn
    return pl.pallas_call(
        _merge_kernel,
        grid=(m // tm, d // tn),
        in_specs=[
            pl.BlockSpec((tm, W_HEADS), lambda i, j: (i, 0)),
            pl.BlockSpec((tm, W_HEADS), lambda i, j: (i, 0)),
            pl.BlockSpec((W_HEADS, tn), lambda i, j: (0, j)),
            pl.BlockSpec((W_HEADS, tn), lambda i, j: (0, j)),
            pl.BlockSpec((tm, tn), lambda i, j: (i, ga_col + j)),
            pl.BlockSpec((tm, tn), lambda i, j: (i, gb_col + j)),
        ],
        out_specs=pl.BlockSpec((tm, tn), lambda i, j: (i, j)),
        out_shape=jax.ShapeDtypeStruct((m, d), BF16),
        compiler_params=_params("arbitrary", "arbitrary"),
        name="merge",
    )(o_sb, o_band, w_sb_bf, w_band_bf, proj, proj)


def _outproj_kernel(m_ref, w_ref, x_ref, gate_ref, o_ref, *, gb):
    acc = _dot(m_ref[...], w_ref[...])
    acc = acc.reshape(gb, ROW_GROUP, acc.shape[1])
    o_ref[...] = x_ref[...] + gate_ref[0] * acc


def _outproj(merged, w_out_bf, x3, mods, gb):
    ng, _, d = x3.shape
    tm = gb * ROW_GROUP
    tn = min(MATMUL_TN, d)
    return pl.pallas_call(
        functools.partial(_outproj_kernel, gb=gb),
        grid=(ng // gb, d // tn),
        in_specs=[
            pl.BlockSpec((tm, d), lambda i, j: (i, 0)),
            pl.BlockSpec((d, tn), lambda i, j: (0, j)),
            pl.BlockSpec((gb, ROW_GROUP, tn), lambda i, j: (i, 0, j)),
            pl.BlockSpec((1, gb, 1, tn), lambda i, j: (MOD_GATE_M, i, 0, j)),
        ],
        out_specs=pl.BlockSpec((gb, ROW_GROUP, tn), lambda i, j: (i, 0, j)),
        out_shape=jax.ShapeDtypeStruct(x3.shape, F32),
        compiler_params=_params("arbitrary", "arbitrary"),
        name="outproj",
    )(merged, w_out_bf, x3, mods)


GATHER_HALF = 128


def _tile_rows(d):
    assert d % V7X_LANES == 0
    return d // V7X_LANES


def _store_token_tiles(ref, first_token, x):
    n, d = x.shape
    s_per = _tile_rows(d)
    for c in range(s_per):
        ref[pl.ds(first_token * s_per + c, n, stride=s_per), :] = x[:, c * V7X_LANES:(c + 1) * V7X_LANES]


def _load_token_tiles(ref, n, d, first_token=0):
    s_per = _tile_rows(d)
    return jnp.concatenate([ref[pl.ds(first_token * s_per + c, n, stride=s_per), :] for c in range(s_per)], axis=1)


def _start_token_gather(table_hbm, idx_ref, first, buf, sem_of, n, s_per):
    for half in range(n // GATHER_HALF):
        def issue(i, carry, half=half):
            row = half * GATHER_HALF + i
            pltpu.make_async_copy(table_hbm.at[pl.ds(idx_ref[first + row] * s_per, s_per)],
                                  buf.at[pl.ds(row * s_per, s_per)], sem_of(half)).start()
            return carry
        lax.fori_loop(0, GATHER_HALF, issue, 0, unroll=8)


def _wait_token_gather(buf, sem_of, n, s_per):
    for half in range(n // GATHER_HALF):
        part = buf.at[pl.ds(half * GATHER_HALF * s_per, GATHER_HALF * s_per)]
        pltpu.make_async_copy(part, part, sem_of(half)).wait()


ROUTE_E1, ROUTE_E2, ROUTE_W1, ROUTE_W2 = 0, 1, 2, 3


def _router_kernel(xa_ref, sca_ref, sha_ref, xb_ref, scb_ref, shb_ref, g_ref, whi_ref, wlo_ref, b_ref,
                   h_ref, route_ref, hi_scr, lo_scr, *, gb, na):
    def fill(x_ref, sc_ref, sh_ref):
        def body(s, carry):
            h = _modulated_norm(x_ref[s], g_ref[...], sc_ref[0, s], sh_ref[0, s])
            hi = h.astype(BF16)
            first = pl.multiple_of(s * ROW_GROUP, ROW_GROUP)
            _store_token_tiles(h_ref, first, h)
            hi_scr[pl.ds(first, ROW_GROUP), :] = hi
            lo_scr[pl.ds(first, ROW_GROUP), :] = (h - hi.astype(F32)).astype(BF16)
            return carry
        lax.fori_loop(0, gb, body, 0)

    @pl.when(pl.program_id(0) < na)
    def _():
        fill(xa_ref, sca_ref, sha_ref)

    @pl.when(pl.program_id(0) >= na)
    def _():
        fill(xb_ref, scb_ref, shb_ref)

    hi = hi_scr[...]
    logits = _dot(hi, whi_ref[...]) + _dot(hi, wlo_ref[...]) + _dot(lo_scr[...], whi_ref[...]) + b_ref[...]
    lane = lax.broadcasted_iota(jnp.int32, logits.shape, 1)
    ninf = -jnp.inf

    lane_f = lane.astype(F32)

    def first_max(vals):
        mx = vals.max(axis=1, keepdims=True)
        idx = jnp.where(vals == mx, lane_f, float(V7X_LANES)).min(axis=1, keepdims=True)
        return mx, idx

    gl = jnp.where(lane < N_GROUPS, logits, ninf)
    gmax, gidx = first_max(gl)
    g_weight = 1.0 / jnp.exp(gl - gmax).sum(axis=1, keepdims=True)
    lo_lane = N_GROUPS + EXPERTS_PER_GROUP * gidx
    el = jnp.where(lane_f >= lo_lane, jnp.where(lane_f < lo_lane + EXPERTS_PER_GROUP, logits, ninf), ninf)
    m1, i1 = first_max(el)
    m2, i2 = first_max(jnp.where(lane_f == i1, ninf, el))
    e21 = jnp.exp(m2 - m1)
    p1 = 1.0 / (1.0 + e21)
    p2 = e21 / (1.0 + e21)
    route = jnp.where(lane == ROUTE_E1, i1 - N_GROUPS,
                      jnp.where(lane == ROUTE_E2, i2 - N_GROUPS,
                                jnp.where(lane == ROUTE_W1, g_weight * p1,
                                          jnp.where(lane == ROUTE_W2, g_weight * p2, 0.0))))
    route_ref[...] = route


def _router(xa3, mods_a, xb3, mods_b, g, w_hi, w_lo, bias, gb):
    na, nb = xa3.shape[0] // gb, xb3.shape[0] // gb
    d = xa3.shape[2]
    n_tokens = (na + nb) * gb * ROW_GROUP
    tm = gb * ROW_GROUP
    s_per = _tile_rows(d)

    def first(i):
        return jnp.minimum(i, na - 1)

    def second(i):
        return jnp.maximum(i - na, 0)

    def mod_spec(which, blk):
        return pl.BlockSpec((1, gb, 1, d), lambda i: (which, blk(i), 0, 0))

    w_spec = pl.BlockSpec((d, V7X_LANES), lambda i: (0, 0))
    return pl.pallas_call(
        functools.partial(_router_kernel, gb=gb, na=na),
        grid=(na + nb,),
        in_specs=[
            pl.BlockSpec((gb, ROW_GROUP, d), lambda i: (first(i), 0, 0)),
            mod_spec(MOD_SCALE_F, first), mod_spec(MOD_SHIFT_F, first),
            pl.BlockSpec((gb, ROW_GROUP, d), lambda i: (second(i), 0, 0)),
            mod_spec(MOD_SCALE_F, second), mod_spec(MOD_SHIFT_F, second),
            pl.BlockSpec((1, d), lambda i: (0, 0)),
            w_spec, w_spec,
            pl.BlockSpec((1, V7X_LANES), lambda i: (0, 0)),
        ],
        out_specs=[
            pl.BlockSpec((tm * s_per, V7X_LANES), lambda i: (i, 0)),
            pl.BlockSpec((tm, V7X_LANES), lambda i: (i, 0)),
        ],
        out_shape=[jax.ShapeDtypeStruct((n_tokens * s_per, V7X_LANES), F32),
                   jax.ShapeDtypeStruct((n_tokens, V7X_LANES), F32)],
        scratch_shapes=[pltpu.VMEM((tm, d), BF16), pltpu.VMEM((tm, d), BF16)],
        compiler_params=_params("arbitrary"),
        name="router",
    )(xa3, mods_a, mods_a, xb3, mods_b, mods_b, g, w_hi, w_lo, bias)


def _moe_kernel(te_ref, nv_ref, src_ref, h_hbm, wg_ref, wu_ref, wd_ref, o_ref, xbuf, sems, wg_scr, wu_scr, wd_scr):
    t = pl.program_id(0)
    n_valid = nv_ref[0]
    d = wg_scr.shape[0]
    s_per = _tile_rows(d)

    def start(tile, slot):
        _start_token_gather(h_hbm, src_ref, tile * MOE_TILE, xbuf.at[slot], lambda half: sems.at[slot, half],
                            MOE_TILE, s_per)

    @pl.when(t == 0)
    def _():
        start(0, 0)

    @pl.when(t + 1 < n_valid)
    def _():
        start(t + 1, (t + 1) % 2)

    changed = te_ref[t] != te_ref[jnp.maximum(t - 1, 0)]

    @pl.when((t == 0) | changed)
    def _():
        wg_scr[...] = wg_ref[0].astype(BF16)
        wu_scr[...] = wu_ref[0].astype(BF16)
        wd_scr[...] = wd_ref[0].astype(BF16)

    @pl.when(t < n_valid)
    def _():
        slot = t % 2
        _wait_token_gather(xbuf.at[slot], lambda half: sems.at[slot, half], MOE_TILE, s_per)

        x = _load_token_tiles(xbuf.at[slot], MOE_TILE, d).astype(BF16)
        g = _dot(x, wg_scr[...])
        u = _dot(x, wu_scr[...])
        hidden = (g * _sigmoid(g)) * u
        _store_token_tiles(o_ref, 0, _dot(hidden.astype(BF16), wd_scr[...]))

    @pl.when(t >= n_valid)
    def _():
        o_ref[...] = jnp.zeros(o_ref.shape, F32)


def _moe(tile_expert, n_valid, src_tok, h_tiles, w_gate, w_up, w_down):
    _, d, f = w_gate.shape
    s_per = _tile_rows(d)
    n_tiles = src_tok.shape[0] // MOE_TILE
    halves = MOE_TILE // GATHER_HALF
    grid_spec = pltpu.PrefetchScalarGridSpec(
        num_scalar_prefetch=3,
        grid=(n_tiles,),
        in_specs=[
            pl.BlockSpec(memory_space=pl.ANY),
            pl.BlockSpec((1, d, f), lambda t, te, nv, src: (te[t], 0, 0)),
            pl.BlockSpec((1, d, f), lambda t, te, nv, src: (te[t], 0, 0)),
            pl.BlockSpec((1, f, d), lambda t, te, nv, src: (te[t], 0, 0)),
        ],
        out_specs=pl.BlockSpec((MOE_TILE * s_per, V7X_LANES), lambda t, te, nv, src: (t, 0)),
        scratch_shapes=[pltpu.VMEM((2, MOE_TILE * s_per, V7X_LANES), F32), pltpu.SemaphoreType.DMA((2, halves)),
                        pltpu.VMEM((d, f), BF16), pltpu.VMEM((d, f), BF16), pltpu.VMEM((f, d), BF16)],
    )
    return pl.pallas_call(
        _moe_kernel,
        grid_spec=grid_spec,
        out_shape=jax.ShapeDtypeStruct((n_tiles * MOE_TILE * s_per, V7X_LANES), F32),
        compiler_params=_params("arbitrary"),
        name="moe",
    )(tile_expert, n_valid, src_tok, h_tiles, w_gate, w_up, w_down)


def _dispatch(route):
    n = route.shape[0]
    e = jnp.concatenate([route[:, ROUTE_E1], route[:, ROUTE_E2]]).astype(jnp.int32)
    tok = jnp.concatenate([jnp.arange(n, dtype=jnp.int32)] * 2)
    onehot = (e[:, None] == jnp.arange(N_EXPERTS, dtype=jnp.int32)[None, :]).astype(jnp.int32)
    before = jnp.cumsum(onehot, axis=0) - onehot
    rank = jnp.sum(before * onehot, axis=1)
    counts = jnp.sum(onehot, axis=0)
    padded = ((counts + MOE_TILE - 1) // MOE_TILE) * MOE_TILE
    ends = jnp.cumsum(padded)
    pos = (ends - padded)[e] + rank
    n_tiles = -(-(2 * n + N_EXPERTS * (MOE_TILE - 1)) // MOE_TILE)
    tile_start = jnp.arange(n_tiles, dtype=jnp.int32) * MOE_TILE
    n_valid = (ends[-1] // MOE_TILE).astype(jnp.int32)
    tile_expert = jnp.sum((ends[None, :] <= tile_start[:, None]).astype(jnp.int32), axis=1)
    last_expert = tile_expert[jnp.maximum(n_valid - 1, 0)]
    tile_expert = jnp.where(tile_start < ends[-1], tile_expert, last_expert)
    src_tok = jnp.zeros((n_tiles * MOE_TILE,), jnp.int32).at[pos].set(tok)
    return pos[:n], pos[n:], src_tok, tile_expert, n_valid.reshape(1)


def _final_kernel(p1_ref, p2_ref, x_ref, y_hbm, route_ref, gate_ref, o_ref, ybuf, sems, *, first_token):
    i = pl.program_id(0)
    gb, rg, d = x_ref.shape
    tm = gb * rg
    s_per = _tile_rows(d)

    def start(step, slot):
        for which, pos_ref in enumerate((p1_ref, p2_ref)):
            _start_token_gather(y_hbm, pos_ref, first_token + step * tm, ybuf.at[slot, which],
                                lambda half, which=which: sems.at[slot, which, half], tm, s_per)

    @pl.when(i == 0)
    def _():
        start(0, 0)

    @pl.when(i + 1 < pl.num_programs(0))
    def _():
        start(i + 1, (i + 1) % 2)

    slot = i % 2
    for which in range(2):
        _wait_token_gather(ybuf.at[slot, which], lambda half, which=which: sems.at[slot, which, half], tm, s_per)
    r = route_ref[...]
    y = (r[:, ROUTE_W1:ROUTE_W1 + 1] * _load_token_tiles(ybuf.at[slot, 0], tm, d)
         + r[:, ROUTE_W2:ROUTE_W2 + 1] * _load_token_tiles(ybuf.at[slot, 1], tm, d))
    o_ref[...] = x_ref[...] + gate_ref[0] * y.reshape(gb, rg, d)


def _final(x3, y_tiles, pos1, pos2, route, first_token, mods, gb):
    ng, _, d = x3.shape
    tm = gb * ROW_GROUP
    s_per = _tile_rows(d)
    assert first_token % tm == 0
    b0 = first_token // tm
    blk = pl.BlockSpec((gb, ROW_GROUP, d), lambda i, p1, p2: (i, 0, 0))
    grid_spec = pltpu.PrefetchScalarGridSpec(
        num_scalar_prefetch=2,
        grid=(ng // gb,),
        in_specs=[
            blk,
            pl.BlockSpec(memory_space=pl.ANY),
            pl.BlockSpec((tm, V7X_LANES), lambda i, p1, p2: (b0 + i, 0)),
            pl.BlockSpec((1, gb, 1, d), lambda i, p1, p2: (MOD_GATE_F, i, 0, 0)),
        ],
        out_specs=blk,
        scratch_shapes=[pltpu.VMEM((2, 2, tm * s_per, V7X_LANES), F32),
                        pltpu.SemaphoreType.DMA((2, 2, tm // GATHER_HALF))],
    )
    return pl.pallas_call(
        functools.partial(_final_kernel, first_token=first_token),
        grid_spec=grid_spec,
        out_shape=jax.ShapeDtypeStruct(x3.shape, F32),
        compiler_params=_params("arbitrary"),
        name="final",
    )(pos1, pos2, x3, y_tiles, route, mods)


def _layer(xp, xs, csk, csv, cbk, cbv, c_prompt, c_sample, norm_mix, norm_ffn, w_ada, b_ada, w_in, q_norm,
           k_norm, rel_table, w_proj_sb, w_proj_band, w_out, w_rg, b_rg, w_re, b_re, w_gate, w_up, w_down):
    bp, sp_len, d = xp.shape
    bs, ts, _ = xs.shape
    d_in = w_in.shape[1]
    assert sp_len % BAND_QB == 0 and ts == ROW_GROUP and sp_len % ROW_GROUP == 0

    n_c = bp + bs
    c_pad = jnp.concatenate([c_prompt, c_sample, jnp.zeros((-n_c % 8, d), F32)], axis=0)
    mod = _ada(c_pad, w_ada, b_ada)
    gp = sp_len // ROW_GROUP
    mod_p = jnp.repeat(mod[:, :bp], gp, axis=1)[:, :, None, :]
    mod_s = mod[:, bp:n_c][:, :, None, :]

    w_in_bf = w_in.astype(BF16)
    w_sb_bf = w_proj_sb.astype(BF16)
    w_band_bf = w_proj_band.astype(BF16)
    w_out_bf = w_out.astype(BF16)
    qn = q_norm.reshape(1, W_HEADS)
    kn = k_norm.reshape(1, W_HEADS)
    g_mix = norm_mix.reshape(1, d)
    g_ffn = norm_ffn.reshape(1, d)
    lb = cbk.shape[1]
    assert lb % CHUNK == 0 and ts == CHUNK
    bias_blocks = _bias_blocks(rel_table)
    bias = _band_bias(bias_blocks, rel_table)
    bias_step = _band_step_bias(bias_blocks, rel_table, lb)
    u = _lower_twice(SB_T)

    w_router = jnp.zeros((d, V7X_LANES), F32).at[:, :N_GROUPS].set(w_rg)
    w_router = w_router.at[:, N_GROUPS:N_GROUPS + N_EXPERTS].set(w_re)
    w_router_hi = w_router.astype(BF16)
    w_router_lo = (w_router - w_router_hi.astype(F32)).astype(BF16)
    b_router = jnp.zeros((1, V7X_LANES), F32).at[0, :N_GROUPS].set(b_rg)
    b_router = b_router.at[0, N_GROUPS:N_GROUPS + N_EXPERTS].set(b_re.reshape(-1))

    xp3 = xp.reshape(bp * gp, ROW_GROUP, d)
    xs3 = xs.reshape(bs, ROW_GROUP, d)
    gb_p = min(ROWWISE_GROUPS, bp * gp)
    gb_s = min(ROWWISE_GROUPS, bs)

    def mixer(x3, mods, gb, attend):
        proj, ka, va, kb, vb = _proj(x3, g_mix, mods, w_in_bf, qn, kn, gb)
        o_sb, o_band = attend(proj)
        merged = _merge(o_sb, o_band, w_sb_bf, w_band_bf, proj, gb * ROW_GROUP)
        x1 = _outproj(merged, w_out_bf, x3, mods, gb)
        return x1, ka, va, kb, vb

    def attend_prompt(proj):
        p3 = proj.reshape(bp, sp_len, d_in)
        return (_sb_prompt(p3, u).reshape(bp * sp_len, W_HEADS),
                _band_prompt(p3, bias).reshape(bp * sp_len, W_HEADS))

    def attend_sample(proj):
        p3 = proj.reshape(bs, ts, d_in)
        def rows(cache):
            return cache.reshape(bs, -1, HEAD_DIM)

        o_sb = _sb_step(p3, rows(csk), rows(csv), u)
        o_band = _band_step(p3, rows(cbk), rows(cbv), bias_step)
        return o_sb.reshape(bs * ts, W_HEADS), o_band.reshape(bs * ts, W_HEADS)

    x1p, kap, vap, kbp, vbp = mixer(xp3, mod_p, min(MATMUL_GROUPS, bp * gp), attend_prompt)
    x1s, kas, vas, kbs, vbs = mixer(xs3, mod_s, min(MATMUL_GROUPS, bs), attend_sample)

    gb_r = min(gb_p, gb_s)
    h2, route = _router(x1p, mod_p, x1s, mod_s, g_ffn, w_router_hi, w_router_lo, b_router, gb_r)
    pos1, pos2, src_tok, tile_expert, n_valid = _dispatch(route)
    y_sorted = _moe(tile_expert, n_valid, src_tok, h2, w_gate, w_up, w_down)
    gb_f = min(FINAL_GROUPS, bs)
    yp = _final(x1p, y_sorted, pos1, pos2, route, 0, mod_p, gb_f).reshape(bp, sp_len, d)
    ys = _final(x1s, y_sorted, pos1, pos2, route, bp * sp_len, mod_s, gb_f).reshape(bs, ts, d)

    def heads(a, b):
        return a.reshape(b, -1, N_HEADS, HEAD_DIM)

    n_band = min(BAND_PAST, sp_len)
    return (yp, ys, heads(kap, bp), heads(vap, bp), heads(kbp, bp)[:, -n_band:], heads(vbp, bp)[:, -n_band:],
            heads(kas, bs), heads(vas, bs), heads(kbs, bs), heads(vbs, bs))


def kernel(x_prompt, x_sample, cache_sb_k, cache_sb_v, cache_band_k, cache_band_v, c_prompt, c_sample, norm_mix, norm_ffn, w_ada, b_ada, w_in, q_norm_band, k_norm_band, rel_bias_band, w_proj_sb, w_proj_band, w_out, w_router_group, b_router_group, w_router_expert, b_router_expert, w_gate, w_up, w_down):
    depth = w_in.shape[0]
    xp, xs = x_prompt, x_sample
    outs = [[] for _ in range(8)]
    for l in range(depth):
        res = _layer(xp, xs, cache_sb_k[l], cache_sb_v[l], cache_band_k[l], cache_band_v[l], c_prompt, c_sample,
                     norm_mix[l], norm_ffn[l], w_ada[l], b_ada[l], w_in[l], q_norm_band[l], k_norm_band[l],
                     rel_bias_band[l], w_proj_sb[l], w_proj_band[l], w_out[l], w_router_group[l],
                     b_router_group[l], w_router_expert[l], b_router_expert[l], w_gate[l], w_up[l], w_down[l])
        xp, xs = res[0], res[1]
        for acc, r in zip(outs, res[2:]):
            acc.append(r)
    return (xp, xs) + tuple(jnp.stack(o, axis=0) for o in outs)
```

```python
import functools

import jax
import jax.numpy as jnp
from jax import lax
from jax.experimental import pallas as pl
from jax.experimental.pallas import tpu as pltpu

F32 = jnp.float32
BF16 = jnp.bfloat16

EPS = 1e-6
HEAD_DIM = 128
N_HEADS = 8
W_HEADS = N_HEADS * HEAD_DIM
CHUNK = 64
BAND_LEFT_CHUNKS = 8
BAND_PAST = BAND_LEFT_CHUNKS * CHUNK
REL_CLIP = 128
N_GROUPS = 4
EXPERTS_PER_GROUP = 8
N_EXPERTS = N_GROUPS * EXPERTS_PER_GROUP
QK_SCALE = HEAD_DIM ** -0.5
NEG_BIG = -1e30
NEG_LOG2_E = -1.4426950408889634

V7X_LANES = 128
V7X_VMEM_LIMIT = 56 * 1024 * 1024
ROW_GROUP = 64
MOE_TILE = 256
MATMUL_GROUPS = 16
MATMUL_TN = 1024
PROJ_TN = 1024
ROWWISE_GROUPS = 8
FINAL_GROUPS = 2


def _params(*sem):
    return pltpu.CompilerParams(dimension_semantics=sem, vmem_limit_bytes=V7X_VMEM_LIMIT)


def _sigmoid(x):
    return 1.0 / (1.0 + jnp.exp(-x))


def _dot(a, b):
    return jnp.dot(a, b, preferred_element_type=F32)


def _dot_nt(a, b):
    return lax.dot_general(a, b, (((1,), (1,)), ((), ())), preferred_element_type=F32)


def _ada_kernel(c_ref, w_ref, b_ref, o_ref):
    c = c_ref[...]
    a = (c * _sigmoid(c)).astype(BF16)
    o_ref[0] = _dot(a, w_ref[...].astype(BF16)) + b_ref[...]


def _ada(c_pad, w_ada, b_ada):
    r, d = c_pad.shape
    tn = min(1024, d)
    per = d // tn
    return pl.pallas_call(
        _ada_kernel,
        grid=(6 * per,),
        in_specs=[
            pl.BlockSpec((r, d), lambda j: (0, 0)),
            pl.BlockSpec((d, tn), lambda j: (0, j)),
            pl.BlockSpec((1, tn), lambda j: (0, j)),
        ],
        out_specs=pl.BlockSpec((1, r, tn), lambda j: (j // per, 0, j % per)),
        out_shape=jax.ShapeDtypeStruct((6, r, d), F32),
        compiler_params=_params("arbitrary"),
        name="ada",
    )(c_pad, w_ada, b_ada.reshape(1, 6 * d))


def _modulated_norm(x, g, scale, shift):
    ms = jnp.mean(x * x, axis=-1, keepdims=True)
    return (x * lax.rsqrt(ms + EPS) * g) * (1.0 + scale) + shift


def _head_norm(a, gain_ref):
    outs = []
    for hh in range(a.shape[1] // HEAD_DIM):
        blk = a[:, hh * HEAD_DIM:(hh + 1) * HEAD_DIM]
        ms = jnp.mean(blk * blk, axis=-1, keepdims=True)
        outs.append(blk * lax.rsqrt(ms + EPS) * gain_ref[:, hh * HEAD_DIM:(hh + 1) * HEAD_DIM])
    return jnp.concatenate(outs, axis=1)


def _proj_kernel(x_ref, g_ref, sc_ref, sh_ref, w_ref, qn_ref, kn_ref,
                 proj_ref, ka_ref, va_ref, kb_ref, vb_ref, h_scr, *, gb, per):
    j = pl.program_id(1)

    @pl.when(j == 0)
    def _():
        def body(s, carry):
            h = _modulated_norm(x_ref[s], g_ref[...], sc_ref[0, s], sh_ref[0, s])
            h_scr[pl.ds(pl.multiple_of(s * ROW_GROUP, ROW_GROUP), ROW_GROUP), :] = h.astype(BF16)
            return carry
        lax.fori_loop(0, gb, body, 0)

    acc = _dot(h_scr[...], w_ref[...])
    sec = j // per

    @pl.when(sec == 0)
    def _():
        proj_ref[...] = (acc * QK_SCALE).astype(BF16)

    @pl.when(sec == 1)
    def _():
        proj_ref[...] = acc.astype(BF16)
        ka_ref[...] = acc

    @pl.when(sec == 2)
    def _():
        proj_ref[...] = acc.astype(BF16)
        va_ref[...] = acc

    @pl.when(sec == 3)
    def _():
        proj_ref[...] = (_head_norm(acc, qn_ref) * QK_SCALE).astype(BF16)

    @pl.when(sec == 4)
    def _():
        n = _head_norm(acc, kn_ref)
        proj_ref[...] = n.astype(BF16)
        kb_ref[...] = n

    @pl.when(sec == 5)
    def _():
        proj_ref[...] = acc.astype(BF16)
        vb_ref[...] = acc

    @pl.when(sec >= 6)
    def _():
        proj_ref[...] = acc.astype(BF16)


MOD_SHIFT_M, MOD_SCALE_M, MOD_GATE_M, MOD_SHIFT_F, MOD_SCALE_F, MOD_GATE_F = range(6)


def _proj(x3, g, mods, w_in_bf, qn, kn, gb):
    ng, _, d = x3.shape
    d_in = w_in_bf.shape[1]
    m = ng * ROW_GROUP
    tm = gb * ROW_GROUP
    tn = PROJ_TN
    per = W_HEADS // tn

    def sect(s):
        return pl.BlockSpec((tm, tn), lambda i, j: (i, jnp.clip(j - s * per, 0, per - 1)),
                            pipeline_mode=pl.Buffered(1))

    def mod_spec(which):
        return pl.BlockSpec((1, gb, 1, d), lambda i, j: (which, i, 0, 0))

    kv_shape = jax.ShapeDtypeStruct((m, W_HEADS), F32)
    return pl.pallas_call(
        functools.partial(_proj_kernel, gb=gb, per=per),
        grid=(ng // gb, d_in // tn),
        in_specs=[
            pl.BlockSpec((gb, ROW_GROUP, d), lambda i, j: (i, 0, 0)),
            pl.BlockSpec((1, d), lambda i, j: (0, 0)),
            mod_spec(MOD_SCALE_M), mod_spec(MOD_SHIFT_M),
            pl.BlockSpec((d, tn), lambda i, j: (0, j)),
            pl.BlockSpec((1, tn), lambda i, j: (0, jnp.clip(j - 3 * per, 0, per - 1))),
            pl.BlockSpec((1, tn), lambda i, j: (0, jnp.clip(j - 4 * per, 0, per - 1))),
        ],
        out_specs=[
            pl.BlockSpec((tm, tn), lambda i, j: (i, j)),
            sect(1), sect(2), sect(4), sect(5),
        ],
        out_shape=[jax.ShapeDtypeStruct((m, d_in), BF16), kv_shape, kv_shape, kv_shape, kv_shape],
        scratch_shapes=[pltpu.VMEM((tm, d), BF16)],
        compiler_params=_params("arbitrary", "arbitrary"),
        name="proj",
    )(x3, g, mods, mods, w_in_bf, qn, kn)


def _lower_twice(n):
    row = lax.broadcasted_iota(jnp.int32, (2 * n, n), 0)
    col = lax.broadcasted_iota(jnp.int32, (2 * n, n), 1)
    row = jnp.where(row >= n, row - n, row)
    return jnp.where(row >= col, 1.0, 0.0).astype(BF16)


def _sb_scores(load_qk, n_heads):
    return jnp.concatenate([_dot_nt(*load_qk(h)) for h in range(n_heads)], axis=1)


def _sb_apply(z_all, load_v, n_heads, u, carries, diagonal):
    tq = z_all.shape[0]
    tk = z_all.shape[1] // n_heads
    zs = [z_all[:, h * tk:(h + 1) * tk] for h in range(n_heads)]
    if diagonal:
        row = lax.broadcasted_iota(jnp.int32, (tq, tk), 0)
        col = lax.broadcasted_iota(jnp.int32, (tq, tk), 1)
        valid = col < row
    tails = []
    for z in zs:
        sp = jnp.maximum(z, 0.0) + jnp.log(1.0 + jnp.exp2(jnp.abs(z) * NEG_LOG2_E))
        if diagonal:
            sp = jnp.where(valid, sp, 0.0)
        hi = sp.astype(BF16)
        lo = (sp - hi.astype(F32)).astype(BF16)
        tails.append(_dot(jnp.concatenate([hi, lo], axis=1), u))
    pvs, new = [], []
    for h, (z, tail) in enumerate(zip(zs, tails)):
        carry = jnp.zeros((tq, 1), F32) if carries is None else carries[:, h * HEAD_DIM:h * HEAD_DIM + 1]
        loga = z - tail - carry
        if diagonal:
            loga = jnp.where(valid, loga, NEG_BIG)
        pvs.append(_dot(jnp.exp(loga).astype(BF16), load_v(h)))
        new.append(jnp.broadcast_to(carry + tail[:, 0:1], (tq, HEAD_DIM)))
    return jnp.concatenate(pvs, axis=1), jnp.concatenate(new, axis=1)


SB_T = 256
SB_HEADS_PER_STEP = 8
SB_CACHE_ROWS = 2048


def _head_slice(h):
    return slice(h * HEAD_DIM, (h + 1) * HEAD_DIM)


def _sb_prompt_kernel(q_ref, k_ref, v_ref, u_ref, o_ref, acc_scr, carry_scr):
    qi = pl.program_id(2)
    u = u_ref[...]
    nh = SB_HEADS_PER_STEP

    def block(blk, carries, diagonal):
        r = pl.multiple_of(blk * SB_T, SB_T)
        z_all = _sb_scores(lambda h: (q_ref[0, :, _head_slice(h)], k_ref[0, pl.ds(r, SB_T), _head_slice(h)]), nh)
        return _sb_apply(z_all, lambda h: v_ref[0, pl.ds(r, SB_T), _head_slice(h)], nh, u, carries, diagonal)

    acc_scr[...], carry_scr[...] = block(qi, None, True)

    def body(step, c):
        pv, carries = block(qi - 1 - step, carry_scr[...], False)
        acc_scr[...] += pv
        carry_scr[...] = carries
        return c

    lax.fori_loop(0, qi, body, 0)
    o_ref[0] = acc_scr[...].astype(BF16)


def _sb_prompt(proj3, u):
    b, s, _ = proj3.shape
    hp = SB_HEADS_PER_STEP
    w = hp * HEAD_DIM
    kcol, vcol = W_HEADS // w, 2 * W_HEADS // w
    return pl.pallas_call(
        _sb_prompt_kernel,
        grid=(b, N_HEADS // hp, s // SB_T),
        in_specs=[
            pl.BlockSpec((1, SB_T, w), lambda bi, h, qi: (bi, qi, h)),
            pl.BlockSpec((1, s, w), lambda bi, h, qi: (bi, 0, kcol + h)),
            pl.BlockSpec((1, s, w), lambda bi, h, qi: (bi, 0, vcol + h)),
            pl.BlockSpec((2 * SB_T, SB_T), lambda bi, h, qi: (0, 0)),
        ],
        out_specs=pl.BlockSpec((1, SB_T, w), lambda bi, h, qi: (bi, qi, h)),
        out_shape=jax.ShapeDtypeStruct((b, s, W_HEADS), BF16),
        scratch_shapes=[pltpu.VMEM((SB_T, w), F32), pltpu.VMEM((SB_T, w), F32)],
        compiler_params=_params("arbitrary", "arbitrary", "arbitrary"),
        name="sb_prompt",
    )(proj3, proj3, proj3, u)


def _sb_step_kernel(q_ref, kn_ref, vn_ref, ck_ref, cv_ref, u_ref, o_ref, acc_scr, carry_scr, *, tkb, t):
    j = pl.program_id(1)
    u = u_ref[...]

    @pl.when(j == 0)
    def _():
        z_new = _sb_scores(lambda h: (q_ref[0, :, _head_slice(h)], kn_ref[0, :, _head_slice(h)]), N_HEADS)
        acc_scr[...], carry_scr[...] = _sb_apply(z_new, lambda h: vn_ref[0, :, _head_slice(h)], N_HEADS,
                                                 _lower_twice(t), None, True)

    def rows(sub, h):
        first = pl.multiple_of((tkb - (sub + 1) * SB_T) * N_HEADS, SB_T * N_HEADS)
        return pl.ds(first + h, SB_T, stride=N_HEADS)

    def body(sub, c):
        z_all = _sb_scores(lambda h: (q_ref[0, :, _head_slice(h)], ck_ref[0, rows(sub, h), :].astype(BF16)), N_HEADS)
        pv, carries = _sb_apply(z_all, lambda h: cv_ref[0, rows(sub, h), :].astype(BF16), N_HEADS, u,
                                carry_scr[...], False)
        acc_scr[...] += pv
        carry_scr[...] = carries
        return c

    lax.fori_loop(0, tkb // SB_T, body, 0)

    @pl.when(j == pl.num_programs(1) - 1)
    def _():
        o_ref[0] = acc_scr[...].astype(BF16)


def _sb_step(proj3, cache_k, cache_v, u):
    b, t, _ = proj3.shape
    p = cache_k.shape[1] // N_HEADS
    tkb = min(SB_CACHE_ROWS, p)
    nj = p // tkb
    cache_spec = pl.BlockSpec((1, tkb * N_HEADS, HEAD_DIM), lambda bi, j: (bi, nj - 1 - j, 0))
    return pl.pallas_call(
        functools.partial(_sb_step_kernel, tkb=tkb, t=t),
        grid=(b, nj),
        in_specs=[
            pl.BlockSpec((1, t, W_HEADS), lambda bi, j: (bi, 0, 0)),
            pl.BlockSpec((1, t, W_HEADS), lambda bi, j: (bi, 0, 1)),
            pl.BlockSpec((1, t, W_HEADS), lambda bi, j: (bi, 0, 2)),
            cache_spec, cache_spec,
            pl.BlockSpec((2 * SB_T, SB_T), lambda bi, j: (0, 0)),
        ],
        out_specs=pl.BlockSpec((1, t, W_HEADS), lambda bi, j: (bi, 0, 0)),
        out_shape=jax.ShapeDtypeStruct((b, t, W_HEADS), BF16),
        scratch_shapes=[pltpu.VMEM((t, W_HEADS), F32), pltpu.VMEM((t, W_HEADS), F32)],
        compiler_params=_params("arbitrary", "arbitrary"),
        name="sb_step",
    )(proj3, proj3, proj3, cache_k, cache_v, u)


BAND_QB = 4 * CHUNK
BAND_KB = 3


BIAS_NEAR = REL_CLIP // CHUNK + 1


def _bias_blocks_kernel(table_ref, o_ref):
    h = pl.program_id(0)
    i = lax.broadcasted_iota(jnp.int32, (CHUNK, CHUNK), 0)
    j = lax.broadcasted_iota(jnp.int32, (CHUNK, CHUNK), 1)
    for d in range(BIAS_NEAR):
        idx = jnp.clip(i - j + d * CHUNK, -REL_CLIP, REL_CLIP) + REL_CLIP

        def body(r, acc, idx=idx):
            return jnp.where(idx == r, table_ref[h, r], acc)

        lo = max(d * CHUNK - (CHUNK - 1), -REL_CLIP) + REL_CLIP
        hi = min(d * CHUNK + (CHUNK - 1), REL_CLIP) + REL_CLIP
        o_ref[0, d] = lax.fori_loop(lo, hi + 1, body, jnp.zeros((CHUNK, CHUNK), F32))


def _bias_blocks(table):
    nh = table.shape[0]
    return pl.pallas_call(
        _bias_blocks_kernel,
        grid=(nh,),
        in_specs=[pl.BlockSpec(memory_space=pltpu.SMEM)],
        out_specs=pl.BlockSpec((1, BIAS_NEAR, CHUNK, CHUNK), lambda h: (h, 0, 0, 0)),
        out_shape=jax.ShapeDtypeStruct((nh, BIAS_NEAR, CHUNK, CHUNK), F32),
        compiler_params=_params("arbitrary"),
        name="bias_blocks",
    )(table.astype(F32))


def _chunk_bias(blocks, far, dist):
    if dist < 0 or dist > BAND_LEFT_CHUNKS:
        return jnp.full(far.shape, NEG_BIG, F32)
    return blocks[:, dist] if dist < BIAS_NEAR else far


def _band_bias(blocks, table):
    far = jnp.broadcast_to(table[:, -1].astype(F32)[:, None, None], (table.shape[0], CHUNK, CHUNK))
    qc, kc = BAND_QB // CHUNK, BAND_KB * BAND_QB // CHUNK
    rows = [jnp.concatenate([_chunk_bias(blocks, far, r - c + BAND_LEFT_CHUNKS) for c in range(kc)], axis=2)
            for r in range(qc)]
    return jnp.concatenate(rows, axis=1)


def _band_step_bias(blocks, table, lb):
    far = jnp.broadcast_to(table[:, -1].astype(F32)[:, None, None], (table.shape[0], CHUNK, CHUNK))
    nc = lb // CHUNK
    return jnp.concatenate([blocks[:, nc - c] if nc - c < BIAS_NEAR else far for c in range(nc + 1)], axis=2)


def _softmax_pv_heads(scores, values):
    ps, dens = [], []
    for sc in scores:
        m = sc[0].max(axis=1, keepdims=True)
        for s in sc[1:]:
            m = jnp.maximum(m, s.max(axis=1, keepdims=True))
        p = [jnp.exp(s - m) for s in sc]
        den = p[0].sum(axis=1, keepdims=True)
        for pc in p[1:]:
            den = den + pc.sum(axis=1, keepdims=True)
        ps.append(p)
        dens.append(den)
    outs = []
    for p, den, vals in zip(ps, dens, values):
        num = _dot(p[0].astype(BF16), vals[0])
        for pc, v in zip(p[1:], vals[1:]):
            num = num + _dot(pc.astype(BF16), v)
        outs.append(num / den)
    return jnp.concatenate(outs, axis=1)


def _band_prompt_kernel(q_ref, k0_ref, k1_ref, k2_ref, v0_ref, v1_ref, v2_ref, bias_ref, o_ref):
    qi = pl.program_id(1)
    scores, values = [], []
    for h in range(N_HEADS):
        hs = _head_slice(h)
        q = q_ref[0, :, hs]
        sc = []
        for c, k_ref in enumerate((k0_ref, k1_ref, k2_ref)):
            s = _dot_nt(q, k_ref[0, :, hs]) + bias_ref[h, :, c * BAND_QB:(c + 1) * BAND_QB]
            if c < BAND_KB - 1:
                s = jnp.where(qi + c >= BAND_KB - 1, s, NEG_BIG)
            sc.append(s)
        scores.append(sc)
        values.append([v_ref[0, :, hs] for v_ref in (v0_ref, v1_ref, v2_ref)])
    o_ref[0] = _softmax_pv_heads(scores, values).astype(BF16)


def _band_prompt(proj3, bias):
    b, s, _ = proj3.shape
    qcol, kcol, vcol = 3, 4, 5

    def kv_spec(col, back):
        return pl.BlockSpec((1, BAND_QB, W_HEADS), lambda bi, qi: (bi, jnp.maximum(qi - back, 0), col))

    return pl.pallas_call(
        _band_prompt_kernel,
        grid=(b, s // BAND_QB),
        in_specs=[
            pl.BlockSpec((1, BAND_QB, W_HEADS), lambda bi, qi: (bi, qi, qcol)),
            kv_spec(kcol, 2), kv_spec(kcol, 1), kv_spec(kcol, 0),
            kv_spec(vcol, 2), kv_spec(vcol, 1), kv_spec(vcol, 0),
            pl.BlockSpec((N_HEADS, BAND_QB, BAND_KB * BAND_QB), lambda bi, qi: (0, 0, 0)),
        ],
        out_specs=pl.BlockSpec((1, BAND_QB, W_HEADS), lambda bi, qi: (bi, qi, 0)),
        out_shape=jax.ShapeDtypeStruct((b, s, W_HEADS), BF16),
        compiler_params=_params("arbitrary", "arbitrary"),
        name="band_prompt",
    )(proj3, proj3, proj3, proj3, proj3, proj3, proj3, bias)


def _band_step_kernel(q_ref, kn_ref, vn_ref, ck_ref, cv_ref, bias_ref, o_ref, *, lb):
    scores, values = [], []
    for h in range(N_HEADS):
        hs = _head_slice(h)
        q = q_ref[0, :, hs]
        rows = pl.ds(h, lb, stride=N_HEADS)
        scores.append([_dot_nt(q, ck_ref[0, rows, :].astype(BF16)) + bias_ref[h, :, :lb],
                       _dot_nt(q, kn_ref[0, :, hs]) + bias_ref[h, :, lb:]])
        values.append([cv_ref[0, rows, :].astype(BF16), vn_ref[0, :, hs]])
    o_ref[0] = _softmax_pv_heads(scores, values).astype(BF16)


def _band_step(proj3, cache_k, cache_v, bias):
    b, t, _ = proj3.shape
    lb = cache_k.shape[1] // N_HEADS
    cache_spec = pl.BlockSpec((1, lb * N_HEADS, HEAD_DIM), lambda bi: (bi, 0, 0))
    return pl.pallas_call(
        functools.partial(_band_step_kernel, lb=lb),
        grid=(b,),
        in_specs=[
            pl.BlockSpec((1, t, W_HEADS), lambda bi: (bi, 0, 3)),
            pl.BlockSpec((1, t, W_HEADS), lambda bi: (bi, 0, 4)),
            pl.BlockSpec((1, t, W_HEADS), lambda bi: (bi, 0, 5)),
            cache_spec, cache_spec,
            pl.BlockSpec((N_HEADS, t, lb + t), lambda bi: (0, 0, 0)),
        ],
        out_specs=pl.BlockSpec((1, t, W_HEADS), lambda bi: (bi, 0, 0)),
        out_shape=jax.ShapeDtypeStruct((b, t, W_HEADS), BF16),
        compiler_params=_params("arbitrary"),
        name="band_step",
    )(proj3, proj3, proj3, cache_k, cache_v, bias)


def _merge_kernel(osb_ref, obd_ref, wsb_ref, wbd_ref, ga_ref, gb_ref, o_ref):
    a = _dot(osb_ref[...], wsb_ref[...])
    b = _dot(obd_ref[...], wbd_ref[...])
    merged = _sigmoid(ga_ref[...].astype(F32)) * a + _sigmoid(gb_ref[...].astype(F32)) * b
    o_ref[...] = merged.astype(BF16)


def _merge(o_sb, o_band, w_sb_bf, w_band_bf, proj, tm):
    m = o_sb.shape[0]
    d = w_sb_bf.shape[1]
    tn = min(MATMUL_TN, d)
    ga_col = 6 * W_HEADS // tn
    gb_col = ga_col + d // tn
    return pl.pallas_call(
        _merge_kernel,
        grid=(m // tm, d // tn),
        in_specs=[
            pl.BlockSpec((tm, W_HEADS), lambda i, j: (i, 0)),
            pl.BlockSpec((tm, W_HEADS), lambda i, j: (i, 0)),
            pl.BlockSpec((W_HEADS, tn), lambda i, j: (0, j)),
            pl.BlockSpec((W_HEADS, tn), lambda i, j: (0, j)),
            pl.BlockSpec((tm, tn), lambda i, j: (i, ga_col + j)),
            pl.BlockSpec((tm, tn), lambda i, j: (i, gb_col + j)),
        ],
        out_specs=pl.BlockSpec((tm, tn), lambda i, j: (i, j)),
        out_shape=jax.ShapeDtypeStruct((m, d), BF16),
        compiler_params=_params("arbitrary", "arbitrary"),
        name="merge",
    )(o_sb, o_band, w_sb_bf, w_band_bf, proj, proj)


def _outproj_kernel(m_ref, w_ref, x_ref, gate_ref, o_ref, *, gb):
    acc = _dot(m_ref[...], w_ref[...])
    acc = acc.reshape(gb, ROW_GROUP, acc.shape[1])
    o_ref[...] = x_ref[...] + gate_ref[0] * acc


def _outproj(merged, w_out_bf, x3, mods, gb):
    ng, _, d = x3.shape
    tm = gb * ROW_GROUP
    tn = min(MATMUL_TN, d)
    return pl.pallas_call(
        functools.partial(_outproj_kernel, gb=gb),
        grid=(ng // gb, d // tn),
        in_specs=[
            pl.BlockSpec((tm, d), lambda i, j: (i, 0)),
            pl.BlockSpec((d, tn), lambda i, j: (0, j)),
            pl.BlockSpec((gb, ROW_GROUP, tn), lambda i, j: (i, 0, j)),
            pl.BlockSpec((1, gb, 1, tn), lambda i, j: (MOD_GATE_M, i, 0, j)),
        ],
        out_specs=pl.BlockSpec((gb, ROW_GROUP, tn), lambda i, j: (i, 0, j)),
        out_shape=jax.ShapeDtypeStruct(x3.shape, F32),
        compiler_params=_params("arbitrary", "arbitrary"),
        name="outproj",
    )(merged, w_out_bf, x3, mods)


GATHER_HALF = 128


def _tile_rows(d):
    assert d % V7X_LANES == 0
    return d // V7X_LANES


def _store_token_tiles(ref, first_token, x):
    n, d = x.shape
    s_per = _tile_rows(d)
    for c in range(s_per):
        ref[pl.ds(first_token * s_per + c, n, stride=s_per), :] = x[:, c * V7X_LANES:(c + 1) * V7X_LANES]


def _load_token_tiles(ref, n, d, first_token=0):
    s_per = _tile_rows(d)
    return jnp.concatenate([ref[pl.ds(first_token * s_per + c, n, stride=s_per), :] for c in range(s_per)], axis=1)


def _start_token_gather(table_hbm, idx_ref, first, buf, sem_of, n, s_per):
    for half in range(n // GATHER_HALF):
        def issue(i, carry, half=half):
            row = half * GATHER_HALF + i
            pltpu.make_async_copy(table_hbm.at[pl.ds(idx_ref[first + row] * s_per, s_per)],
                                  buf.at[pl.ds(row * s_per, s_per)], sem_of(half)).start()
            return carry
        lax.fori_loop(0, GATHER_HALF, issue, 0, unroll=8)


def _wait_token_gather(buf, sem_of, n, s_per):
    for half in range(n // GATHER_HALF):
        part = buf.at[pl.ds(half * GATHER_HALF * s_per, GATHER_HALF * s_per)]
        pltpu.make_async_copy(part, part, sem_of(half)).wait()


ROUTE_E1, ROUTE_E2, ROUTE_W1, ROUTE_W2 = 0, 1, 2, 3


def _router_kernel(xa_ref, sca_ref, sha_ref, xb_ref, scb_ref, shb_ref, g_ref, whi_ref, wlo_ref, b_ref,
                   h_ref, route_ref, hi_scr, lo_scr, *, gb, na):
    def fill(x_ref, sc_ref, sh_ref):
        def body(s, carry):
            h = _modulated_norm(x_ref[s], g_ref[...], sc_ref[0, s], sh_ref[0, s])
            hi = h.astype(BF16)
            first = pl.multiple_of(s * ROW_GROUP, ROW_GROUP)
            _store_token_tiles(h_ref, first, h)
            hi_scr[pl.ds(first, ROW_GROUP), :] = hi
            lo_scr[pl.ds(first, ROW_GROUP), :] = (h - hi.astype(F32)).astype(BF16)
            return carry
        lax.fori_loop(0, gb, body, 0)

    @pl.when(pl.program_id(0) < na)
    def _():
        fill(xa_ref, sca_ref, sha_ref)

    @pl.when(pl.program_id(0) >= na)
    def _():
        fill(xb_ref, scb_ref, shb_ref)

    hi = hi_scr[...]
    logits = _dot(hi, whi_ref[...]) + _dot(hi, wlo_ref[...]) + _dot(lo_scr[...], whi_ref[...]) + b_ref[...]
    lane = lax.broadcasted_iota(jnp.int32, logits.shape, 1)
    ninf = -jnp.inf

    lane_f = lane.astype(F32)

    def first_max(vals):
        mx = vals.max(axis=1, keepdims=True)
        idx = jnp.where(vals == mx, lane_f, float(V7X_LANES)).min(axis=1, keepdims=True)
        return mx, idx

    gl = jnp.where(lane < N_GROUPS, logits, ninf)
    gmax, gidx = first_max(gl)
    g_weight = 1.0 / jnp.exp(gl - gmax).sum(axis=1, keepdims=True)
    lo_lane = N_GROUPS + EXPERTS_PER_GROUP * gidx
    el = jnp.where(lane_f >= lo_lane, jnp.where(lane_f < lo_lane + EXPERTS_PER_GROUP, logits, ninf), ninf)
    m1, i1 = first_max(el)
    m2, i2 = first_max(jnp.where(lane_f == i1, ninf, el))
    e21 = jnp.exp(m2 - m1)
    p1 = 1.0 / (1.0 + e21)
    p2 = e21 / (1.0 + e21)
    route = jnp.where(lane == ROUTE_E1, i1 - N_GROUPS,
                      jnp.where(lane == ROUTE_E2, i2 - N_GROUPS,
                                jnp.where(lane == ROUTE_W1, g_weight * p1,
                                          jnp.where(lane == ROUTE_W2, g_weight * p2, 0.0))))
    route_ref[...] = route


def _router(xa3, mods_a, xb3, mods_b, g, w_hi, w_lo, bias, gb):
    na, nb = xa3.shape[0] // gb, xb3.shape[0] // gb
    d = xa3.shape[2]
    n_tokens = (na + nb) * gb * ROW_GROUP
    tm = gb * ROW_GROUP
    s_per = _tile_rows(d)

    def first(i):
        return jnp.minimum(i, na - 1)

    def second(i):
        return jnp.maximum(i - na, 0)

    def mod_spec(which, blk):
        return pl.BlockSpec((1, gb, 1, d), lambda i: (which, blk(i), 0, 0))

    w_spec = pl.BlockSpec((d, V7X_LANES), lambda i: (0, 0))
    return pl.pallas_call(
        functools.partial(_router_kernel, gb=gb, na=na),
        grid=(na + nb,),
        in_specs=[
            pl.BlockSpec((gb, ROW_GROUP, d), lambda i: (first(i), 0, 0)),
            mod_spec(MOD_SCALE_F, first), mod_spec(MOD_SHIFT_F, first),
            pl.BlockSpec((gb, ROW_GROUP, d), lambda i: (second(i), 0, 0)),
            mod_spec(MOD_SCALE_F, second), mod_spec(MOD_SHIFT_F, second),
            pl.BlockSpec((1, d), lambda i: (0, 0)),
            w_spec, w_spec,
            pl.BlockSpec((1, V7X_LANES), lambda i: (0, 0)),
        ],
        out_specs=[
            pl.BlockSpec((tm * s_per, V7X_LANES), lambda i: (i, 0)),
            pl.BlockSpec((tm, V7X_LANES), lambda i: (i, 0)),
        ],
        out_shape=[jax.ShapeDtypeStruct((n_tokens * s_per, V7X_LANES), F32),
                   jax.ShapeDtypeStruct((n_tokens, V7X_LANES), F32)],
        scratch_shapes=[pltpu.VMEM((tm, d), BF16), pltpu.VMEM((tm, d), BF16)],
        compiler_params=_params("arbitrary"),
        name="router",
    )(xa3, mods_a, mods_a, xb3, mods_b, mods_b, g, w_hi, w_lo, bias)


def _moe_kernel(te_ref, nv_ref, src_ref, h_hbm, wg_ref, wu_ref, wd_ref, o_ref, xbuf, sems, wg_scr, wu_scr, wd_scr):
    t = pl.program_id(0)
    n_valid = nv_ref[0]
    d = wg_scr.shape[0]
    s_per = _tile_rows(d)

    def start(tile, slot):
        _start_token_gather(h_hbm, src_ref, tile * MOE_TILE, xbuf.at[slot], lambda half: sems.at[slot, half],
                            MOE_TILE, s_per)

    @pl.when(t == 0)
    def _():
        start(0, 0)

    @pl.when(t + 1 < n_valid)
    def _():
        start(t + 1, (t + 1) % 2)

    changed = te_ref[t] != te_ref[jnp.maximum(t - 1, 0)]

    @pl.when((t == 0) | changed)
    def _():
        wg_scr[...] = wg_ref[0].astype(BF16)
        wu_scr[...] = wu_ref[0].astype(BF16)
        wd_scr[...] = wd_ref[0].astype(BF16)

    @pl.when(t < n_valid)
    def _():
        slot = t % 2
        _wait_token_gather(xbuf.at[slot], lambda half: sems.at[slot, half], MOE_TILE, s_per)

        x = _load_token_tiles(xbuf.at[slot], MOE_TILE, d).astype(BF16)
        g = _dot(x, wg_scr[...])
        u = _dot(x, wu_scr[...])
        hidden = (g * _sigmoid(g)) * u
        _store_token_tiles(o_ref, 0, _dot(hidden.astype(BF16), wd_scr[...]))

    @pl.when(t >= n_valid)
    def _():
        o_ref[...] = jnp.zeros(o_ref.shape, F32)


def _moe(tile_expert, n_valid, src_tok, h_tiles, w_gate, w_up, w_down):
    _, d, f = w_gate.shape
    s_per = _tile_rows(d)
    n_tiles = src_tok.shape[0] // MOE_TILE
    halves = MOE_TILE // GATHER_HALF
    grid_spec = pltpu.PrefetchScalarGridSpec(
        num_scalar_prefetch=3,
        grid=(n_tiles,),
        in_specs=[
            pl.BlockSpec(memory_space=pl.ANY),
            pl.BlockSpec((1, d, f), lambda t, te, nv, src: (te[t], 0, 0)),
            pl.BlockSpec((1, d, f), lambda t, te, nv, src: (te[t], 0, 0)),
            pl.BlockSpec((1, f, d), lambda t, te, nv, src: (te[t], 0, 0)),
        ],
        out_specs=pl.BlockSpec((MOE_TILE * s_per, V7X_LANES), lambda t, te, nv, src: (t, 0)),
        scratch_shapes=[pltpu.VMEM((2, MOE_TILE * s_per, V7X_LANES), F32), pltpu.SemaphoreType.DMA((2, halves)),
                        pltpu.VMEM((d, f), BF16), pltpu.VMEM((d, f), BF16), pltpu.VMEM((f, d), BF16)],
    )
    return pl.pallas_call(
        _moe_kernel,
        grid_spec=grid_spec,
        out_shape=jax.ShapeDtypeStruct((n_tiles * MOE_TILE * s_per, V7X_LANES), F32),
        compiler_params=_params("arbitrary"),
        name="moe",
    )(tile_expert, n_valid, src_tok, h_tiles, w_gate, w_up, w_down)


def _dispatch(route):
    n = route.shape[0]
    e = jnp.concatenate([route[:, ROUTE_E1], route[:, ROUTE_E2]]).astype(jnp.int32)
    tok = jnp.concatenate([jnp.arange(n, dtype=jnp.int32)] * 2)
    onehot = (e[:, None] == jnp.arange(N_EXPERTS, dtype=jnp.int32)[None, :]).astype(jnp.int32)
    before = jnp.cumsum(onehot, axis=0) - onehot
    rank = jnp.sum(before * onehot, axis=1)
    counts = jnp.sum(onehot, axis=0)
    padded = ((counts + MOE_TILE - 1) // MOE_TILE) * MOE_TILE
    ends = jnp.cumsum(padded)
    pos = (ends - padded)[e] + rank
    n_tiles = -(-(2 * n + N_EXPERTS * (MOE_TILE - 1)) // MOE_TILE)
    tile_start = jnp.arange(n_tiles, dtype=jnp.int32) * MOE_TILE
    n_valid = (ends[-1] // MOE_TILE).astype(jnp.int32)
    tile_expert = jnp.sum((ends[None, :] <= tile_start[:, None]).astype(jnp.int32), axis=1)
    last_expert = tile_expert[jnp.maximum(n_valid - 1, 0)]
    tile_expert = jnp.where(tile_start < ends[-1], tile_expert, last_expert)
    src_tok = jnp.zeros((n_tiles * MOE_TILE,), jnp.int32).at[pos].set(tok)
    return pos[:n], pos[n:], src_tok, tile_expert, n_valid.reshape(1)


def _final_kernel(p1_ref, p2_ref, x_ref, y_hbm, route_ref, gate_ref, o_ref, ybuf, sems, *, first_token):
    i = pl.program_id(0)
    gb, rg, d = x_ref.shape
    tm = gb * rg
    s_per = _tile_rows(d)

    def start(step, slot):
        for which, pos_ref in enumerate((p1_ref, p2_ref)):
            _start_token_gather(y_hbm, pos_ref, first_token + step * tm, ybuf.at[slot, which],
                                lambda half, which=which: sems.at[slot, which, half], tm, s_per)

    @pl.when(i == 0)
    def _():
        start(0, 0)

    @pl.when(i + 1 < pl.num_programs(0))
    def _():
        start(i + 1, (i + 1) % 2)

    slot = i % 2
    for which in range(2):
        _wait_token_gather(ybuf.at[slot, which], lambda half, which=which: sems.at[slot, which, half], tm, s_per)
    r = route_ref[...]
    y = (r[:, ROUTE_W1:ROUTE_W1 + 1] * _load_token_tiles(ybuf.at[slot, 0], tm, d)
         + r[:, ROUTE_W2:ROUTE_W2 + 1] * _load_token_tiles(ybuf.at[slot, 1], tm, d))
    o_ref[...] = x_ref[...] + gate_ref[0] * y.reshape(gb, rg, d)


def _final(x3, y_tiles, pos1, pos2, route, first_token, mods, gb):
    ng, _, d = x3.shape
    tm = gb * ROW_GROUP
    s_per = _tile_rows(d)
    assert first_token % tm == 0
    b0 = first_token // tm
    blk = pl.BlockSpec((gb, ROW_GROUP, d), lambda i, p1, p2: (i, 0, 0))
    grid_spec = pltpu.PrefetchScalarGridSpec(
        num_scalar_prefetch=2,
        grid=(ng // gb,),
        in_specs=[
            blk,
            pl.BlockSpec(memory_space=pl.ANY),
            pl.BlockSpec((tm, V7X_LANES), lambda i, p1, p2: (b0 + i, 0)),
            pl.BlockSpec((1, gb, 1, d), lambda i, p1, p2: (MOD_GATE_F, i, 0, 0)),
        ],
        out_specs=blk,
        scratch_shapes=[pltpu.VMEM((2, 2, tm * s_per, V7X_LANES), F32),
                        pltpu.SemaphoreType.DMA((2, 2, tm // GATHER_HALF))],
    )
    return pl.pallas_call(
        functools.partial(_final_kernel, first_token=first_token),
        grid_spec=grid_spec,
        out_shape=jax.ShapeDtypeStruct(x3.shape, F32),
        compiler_params=_params("arbitrary"),
        name="final",
    )(pos1, pos2, x3, y_tiles, route, mods)


def _layer(xp, xs, csk, csv, cbk, cbv, c_prompt, c_sample, norm_mix, norm_ffn, w_ada, b_ada, w_in, q_norm,
           k_norm, rel_table, w_proj_sb, w_proj_band, w_out, w_rg, b_rg, w_re, b_re, w_gate, w_up, w_down):
    bp, sp_len, d = xp.shape
    bs, ts, _ = xs.shape
    d_in = w_in.shape[1]
    assert sp_len % BAND_QB == 0 and ts == ROW_GROUP and sp_len % ROW_GROUP == 0

    n_c = bp + bs
    c_pad = jnp.concatenate([c_prompt, c_sample, jnp.zeros((-n_c % 8, d), F32)], axis=0)
    mod = _ada(c_pad, w_ada, b_ada)
    gp = sp_len // ROW_GROUP
    mod_p = jnp.repeat(mod[:, :bp], gp, axis=1)[:, :, None, :]
    mod_s = mod[:, bp:n_c][:, :, None, :]

    w_in_bf = w_in.astype(BF16)
    w_sb_bf = w_proj_sb.astype(BF16)
    w_band_bf = w_proj_band.astype(BF16)
    w_out_bf = w_out.astype(BF16)
    qn = q_norm.reshape(1, W_HEADS)
    kn = k_norm.reshape(1, W_HEADS)
    g_mix = norm_mix.reshape(1, d)
    g_ffn = norm_ffn.reshape(1, d)
    lb = cbk.shape[1]
    assert lb % CHUNK == 0 and ts == CHUNK
    bias_blocks = _bias_blocks(rel_table)
    bias = _band_bias(bias_blocks, rel_table)
    bias_step = _band_step_bias(bias_blocks, rel_table, lb)
    u = _lower_twice(SB_T)

    w_router = jnp.zeros((d, V7X_LANES), F32).at[:, :N_GROUPS].set(w_rg)
    w_router = w_router.at[:, N_GROUPS:N_GROUPS + N_EXPERTS].set(w_re)
    w_router_hi = w_router.astype(BF16)
    w_router_lo = (w_router - w_router_hi.astype(F32)).astype(BF16)
    b_router = jnp.zeros((1, V7X_LANES), F32).at[0, :N_GROUPS].set(b_rg)
    b_router = b_router.at[0, N_GROUPS:N_GROUPS + N_EXPERTS].set(b_re.reshape(-1))

    xp3 = xp.reshape(bp * gp, ROW_GROUP, d)
    xs3 = xs.reshape(bs, ROW_GROUP, d)
    gb_p = min(ROWWISE_GROUPS, bp * gp)
    gb_s = min(ROWWISE_GROUPS, bs)

    def mixer(x3, mods, gb, attend):
        proj, ka, va, kb, vb = _proj(x3, g_mix, mods, w_in_bf, qn, kn, gb)
        o_sb, o_band = attend(proj)
        merged = _merge(o_sb, o_band, w_sb_bf, w_band_bf, proj, gb * ROW_GROUP)
        x1 = _outproj(merged, w_out_bf, x3, mods, gb)
        return x1, ka, va, kb, vb

    def attend_prompt(proj):
        p3 = proj.reshape(bp, sp_len, d_in)
        return (_sb_prompt(p3, u).reshape(bp * sp_len, W_HEADS),
                _band_prompt(p3, bias).reshape(bp * sp_len, W_HEADS))

    def attend_sample(proj):
        p3 = proj.reshape(bs, ts, d_in)
        def rows(cache):
            return cache.reshape(bs, -1, HEAD_DIM)

        o_sb = _sb_step(p3, rows(csk), rows(csv), u)
        o_band = _band_step(p3, rows(cbk), rows(cbv), bias_step)
        return o_sb.reshape(bs * ts, W_HEADS), o_band.reshape(bs * ts, W_HEADS)

    x1p, kap, vap, kbp, vbp = mixer(xp3, mod_p, min(MATMUL_GROUPS, bp * gp), attend_prompt)
    x1s, kas, vas, kbs, vbs = mixer(xs3, mod_s, min(MATMUL_GROUPS, bs), attend_sample)

    gb_r = min(gb_p, gb_s)
    h2, route = _router(x1p, mod_p, x1s, mod_s, g_ffn, w_router_hi, w_router_lo, b_router, gb_r)
    pos1, pos2, src_tok, tile_expert, n_valid = _dispatch(route)
    y_sorted = _moe(tile_expert, n_valid, src_tok, h2, w_gate, w_up, w_down)
    gb_f = min(FINAL_GROUPS, bs)
    yp = _final(x1p, y_sorted, pos1, pos2, route, 0, mod_p, gb_f).reshape(bp, sp_len, d)
    ys = _final(x1s, y_sorted, pos1, pos2, route, bp * sp_len, mod_s, gb_f).reshape(bs, ts, d)

    def heads(a, b):
        return a.reshape(b, -1, N_HEADS, HEAD_DIM)

    n_band = min(BAND_PAST, sp_len)
    return (yp, ys, heads(kap, bp), heads(vap, bp), heads(kbp, bp)[:, -n_band:], heads(vbp, bp)[:, -n_band:],
            heads(kas, bs), heads(vas, bs), heads(kbs, bs), heads(vbs, bs))


def kernel(x_prompt, x_sample, cache_sb_k, cache_sb_v, cache_band_k, cache_band_v, c_prompt, c_sample, norm_mix, norm_ffn, w_ada, b_ada, w_in, q_norm_band, k_norm_band, rel_bias_band, w_proj_sb, w_proj_band, w_out, w_router_group, b_router_group, w_router_expert, b_router_expert, w_gate, w_up, w_down):
    depth = w_in.shape[0]
    xp, xs = x_prompt, x_sample
    outs = [[] for _ in range(8)]
    for l in range(depth):
        res = _layer(xp, xs, cache_sb_k[l], cache_sb_v[l], cache_band_k[l], cache_band_v[l], c_prompt, c_sample,
                     norm_mix[l], norm_ffn[l], w_ada[l], b_ada[l], w_in[l], q_norm_band[l], k_norm_band[l],
                     rel_bias_band[l], w_proj_sb[l], w_proj_band[l], w_out[l], w_router_group[l],
                     b_router_group[l], w_router_expert[l], b_router_expert[l], w_gate[l], w_up[l], w_down[l])
        xp, xs = res[0], res[1]
        for acc, r in zip(outs, res[2:]):
            acc.append(r)
    return (xp, xs) + tuple(jnp.stack(o, axis=0) for o in outs)
```
